```python
import math
import jax, jax.numpy as jnp
from jax import lax
import numpy as np

D_MODEL = 1024
BATCH = 2
SEQ = 16384
DEPTH = 2

CHUNK = 64
N_MIXERS = 4
HEAD_DIM = 64
GROUP_WIDTH = D_MODEL // N_MIXERS
HEADS_PER_GROUP = GROUP_WIDTH // HEAD_DIM
LEFT_CHUNKS = 8
BAND = (LEFT_CHUNKS + 1) * CHUNK
REL_CLIP = 128
CONV_WIDTH = 4
DIFF_QK_DIM = HEAD_DIM // 2
Q_BLOCK = 128
D_FF = 2816
EPS = 1e-6
NEG = -1e30

IN_PROJ_SIZES = [GROUP_WIDTH, GROUP_WIDTH, GROUP_WIDTH,
                 2 * GROUP_WIDTH, GROUP_WIDTH, GROUP_WIDTH,
                 HEADS_PER_GROUP, HEADS_PER_GROUP,
                 GROUP_WIDTH, GROUP_WIDTH, GROUP_WIDTH,
                 GROUP_WIDTH, GROUP_WIDTH, GROUP_WIDTH]
IN_COLS = sum(IN_PROJ_SIZES)
IN_PROJ_OFFSETS = [int(o) for o in np.cumsum(IN_PROJ_SIZES)[:-1]]

kernel_name = "hybrid_chunk_causal_encoder"


def rms_norm(x, g):
    xf = x.astype(jnp.float32)
    y = xf * lax.rsqrt(jnp.mean(xf * xf, axis=-1, keepdims=True) + EPS)
    return (y * g.astype(jnp.float32)).astype(x.dtype)


def swiglu(x, wg, wu, wd):
    return (jax.nn.silu(x @ wg) * (x @ wu)) @ wd


def split_heads(t, n_heads):
    b, s, w = t.shape
    return t.reshape(b, s, n_heads, w // n_heads).transpose(0, 2, 1, 3)


def merge_heads(t):
    b, h, s, d = t.shape
    return t.transpose(0, 2, 1, 3).reshape(b, s, h * d)


def causal_depthwise_conv(x, w, bias):
    width = w.shape[0]
    s = x.shape[1]
    xp = jnp.pad(x, ((0, 0), (width - 1, 0), (0, 0)))
    acc = bias
    for j in range(width):
        acc = acc + xp[:, j:j + s] * w[j]
    return acc


def chunk_relpos_attention(q, k, v, gq, gk, rel_bias):
    b, h, s, d = q.shape
    nc = s // CHUNK
    pad = LEFT_CHUNKS * CHUNK
    q = rms_norm(q, gq)
    k = rms_norm(k, gk)
    kp = jnp.pad(k, ((0, 0), (0, 0), (pad, 0), (0, 0))).reshape(b, h, nc + LEFT_CHUNKS, CHUNK, d)
    vp = jnp.pad(v, ((0, 0), (0, 0), (pad, 0), (0, 0))).reshape(b, h, nc + LEFT_CHUNKS, CHUNK, d)
    kb = jnp.concatenate([kp[:, :, i:i + nc] for i in range(LEFT_CHUNKS + 1)], axis=3)
    vb = jnp.concatenate([vp[:, :, i:i + nc] for i in range(LEFT_CHUNKS + 1)], axis=3)
    qc = q.reshape(b, h, nc, CHUNK, d)
    scores = jnp.einsum("bhcid,bhcjd->bhcij", qc, kb).astype(jnp.float32) * (d ** -0.5)
    rel = pad + jnp.arange(CHUNK)[:, None] - jnp.arange(BAND)[None, :]
    idx = jnp.clip(rel, -REL_CLIP, REL_CLIP) + REL_CLIP
    bias = rel_bias[:, idx].astype(jnp.float32)
    key_pos = jnp.arange(nc)[:, None] * CHUNK - pad + jnp.arange(BAND)[None, :]
    valid = key_pos >= 0
    scores = jnp.where(valid[None, None, :, None, :], scores + bias[None, :, None], NEG)
    probs = jax.nn.softmax(scores, axis=-1).astype(v.dtype)
    out = jnp.einsum("bhcij,bhcjd->bhcid", probs, vb)
    return out.reshape(b, h, s, d)


def mlstm_chunkwise(q, k, v, i_pre, f_pre):
    b, h, s, d = q.shape
    nc = s // CHUNK
    f32 = jnp.float32
    q = q.astype(f32)
    k = k.astype(f32) * (d ** -0.5)
    v = v.astype(f32)
    i_pre = i_pre.astype(f32)
    log_f = jax.nn.log_sigmoid(f_pre.astype(f32))

    def chunks(t):
        return jnp.moveaxis(t.reshape(b, h, nc, CHUNK, *t.shape[3:]), 2, 0)

    causal = jnp.tril(jnp.ones((CHUNK, CHUNK), dtype=bool))

    def step(carry, xs):
        c_mat, n_vec, m = carry
        qc, kc, vc, ic, lfc = xs
        bcum = jnp.cumsum(lfc, axis=-1)
        dmat = jnp.where(causal, bcum[..., :, None] - bcum[..., None, :] + ic[..., None, :], -jnp.inf)
        inter = bcum + m[..., None]
        m_t = jnp.maximum(inter, jnp.max(dmat, axis=-1))
        w_intra = jnp.exp(dmat - m_t[..., None])
        s_inter = jnp.exp(inter - m_t)
        qk = jnp.einsum("bhtd,bhsd->bhts", qc, kc) * w_intra
        num = s_inter[..., None] * jnp.einsum("bhtd,bhde->bhte", qc, c_mat) + jnp.einsum("bhts,bhse->bhte", qk, vc)
        den = s_inter * jnp.einsum("bhtd,bhd->bht", qc, n_vec) + jnp.sum(qk, axis=-1)
        h_out = num / jnp.maximum(jnp.abs(den), jnp.exp(-m_t))[..., None]
        b_tot = bcum[..., -1]
        g = b_tot[..., None] - bcum + ic
        m_new = jnp.maximum(b_tot + m, jnp.max(g, axis=-1))
        a = jnp.exp(b_tot + m - m_new)
        wg = jnp.exp(g - m_new[..., None])
        c_new = a[..., None, None] * c_mat + jnp.einsum("bhs,bhsd,bhse->bhde", wg, kc, vc)
        n_new = a[..., None] * n_vec + jnp.einsum("bhs,bhsd->bhd", wg, kc)
        return (c_new, n_new, m_new), h_out

    init = (jnp.zeros((b, h, d, d), f32), jnp.zeros((b, h, d), f32), jnp.zeros((b, h), f32))
    _, hs = lax.scan(step, init, (chunks(q), chunks(k), chunks(v), chunks(i_pre), chunks(log_f)))
    return jnp.moveaxis(hs, 0, 2).reshape(b, h, s, d)


def diff_attention(q, k, v, lam):
    b, h, _, s, dq = q.shape
    nb = s // Q_BLOCK
    qb = jnp.moveaxis(q.reshape(b, h, 2, nb, Q_BLOCK, dq), 3, 0)
    k_chunk = jnp.arange(s) // CHUNK

    def block(args):
        qblk, bi = args
        sc = jnp.einsum("bhmqd,bhmkd->bhmqk", qblk, k).astype(jnp.float32) * (dq ** -0.5)
        q_chunk = (bi * Q_BLOCK + jnp.arange(Q_BLOCK)) // CHUNK
        mask = k_chunk[None, :] <= q_chunk[:, None]
        p = jax.nn.softmax(jnp.where(mask, sc, NEG), axis=-1)
        w = p[:, :, 0] - lam * p[:, :, 1]
        return jnp.einsum("bhqk,bhke->bhqe", w.astype(v.dtype), v)

    out = lax.map(block, (qb, jnp.arange(nb)))
    return jnp.moveaxis(out, 0, 2).reshape(b, h, s, v.shape[-1])


def stick_breaking_attention(q, k, v):
    b, h, s, d = q.shape
    nb = s // Q_BLOCK
    qb = jnp.moveaxis(q.reshape(b, h, nb, Q_BLOCK, d), 2, 0)
    k_pos = jnp.arange(s)

    def block(args):
        qblk, bi = args
        z = jnp.einsum("bhqd,bhkd->bhqk", qblk, k).astype(jnp.float32) * (d ** -0.5)
        q_pos = bi * Q_BLOCK + jnp.arange(Q_BLOCK)
        before = k_pos[None, :] < q_pos[:, None]
        log_keep = jnp.where(before, jax.nn.log_sigmoid(-z), 0.0)
        between = lax.cumsum(log_keep, axis=3, reverse=True) - log_keep
        a = jnp.where(before, jnp.exp(jax.nn.log_sigmoid(z) + between), 0.0)
        return jnp.einsum("bhqk,bhke->bhqe", a.astype(v.dtype), v)

    out = lax.map(block, (qb, jnp.arange(nb)))
    return jnp.moveaxis(out, 0, 2).reshape(b, h, s, d)


def hybrid_mix(h, w_in, a_q_norm, a_k_norm, a_rel_bias, b_conv_w, b_conv_b, b_gate_bias,
               b_out_norm, c_q_norm, c_k_norm, c_lambda, c_out_norm, lam_init):
    nh = HEADS_PER_GROUP
    p = h @ w_in
    (aq, ak, av, bqk, bv, bo, bi, bf, cq, ck, cv, dq, dk, dv) = jnp.split(p, IN_PROJ_OFFSETS, axis=-1)
    b, s, _ = h.shape

    ya = merge_heads(chunk_relpos_attention(split_heads(aq, nh), split_heads(ak, nh), split_heads(av, nh),
                                            a_q_norm, a_k_norm, a_rel_bias))

    bqk = jax.nn.silu(causal_depthwise_conv(bqk, b_conv_w, b_conv_b))
    bq, bk = jnp.split(bqk, 2, axis=-1)
    i_pre = (bi + b_gate_bias[0]).transpose(0, 2, 1)
    f_pre = (bf + b_gate_bias[1]).transpose(0, 2, 1)
    hb = mlstm_chunkwise(split_heads(bq, nh), split_heads(bk, nh), split_heads(bv, nh), i_pre, f_pre)
    hb = rms_norm(hb, b_out_norm[:, None, :])
    yb = merge_heads(hb).astype(h.dtype) * jax.nn.sigmoid(bo)

    cq = cq.reshape(b, s, nh, 2, DIFF_QK_DIM).transpose(0, 2, 3, 1, 4)
    ck = ck.reshape(b, s, nh, 2, DIFF_QK_DIM).transpose(0, 2, 3, 1, 4)
    lv = c_lambda.astype(jnp.float32)
    lam = jnp.exp(jnp.sum(lv[0] * lv[1])) - jnp.exp(jnp.sum(lv[2] * lv[3])) + lam_init
    hc = diff_attention(rms_norm(cq, c_q_norm), rms_norm(ck, c_k_norm), split_heads(cv, nh), lam)
    yc = merge_heads(rms_norm(hc, c_out_norm) * (1.0 - lam_init))

    yd = merge_heads(stick_breaking_attention(split_heads(dq, nh), split_heads(dk, nh), split_heads(dv, nh)))

    return jnp.concatenate([ya, yb, yc, yd], axis=-1)


def setup_inputs(seed: int = 0) -> dict:
    key = jax.random.key(seed)
    ks = jax.random.split(key, 32)
    f32 = jnp.float32

    def nrm(k, shape, scale):
        return jax.random.normal(k, shape, f32) * scale

    def gain(k, shape):
        return 1.0 + 0.05 * jax.random.normal(k, shape, f32)

    i_bias = nrm(ks[13], (DEPTH, HEADS_PER_GROUP), 0.1)
    f_bias = jnp.linspace(3.0, 6.0, HEADS_PER_GROUP, dtype=f32)[None, :] + nrm(ks[14], (DEPTH, HEADS_PER_GROUP), 0.1)
    return {
        "x": jax.random.normal(ks[0], (BATCH, SEQ, D_MODEL), f32),
        "ffn1_norm": gain(ks[1], (DEPTH, D_MODEL)),
        "ffn1_wg": nrm(ks[2], (DEPTH, D_MODEL, D_FF), D_MODEL ** -0.5),
        "ffn1_wu": nrm(ks[3], (DEPTH, D_MODEL, D_FF), D_MODEL ** -0.5),
        "ffn1_wd": nrm(ks[4], (DEPTH, D_FF, D_MODEL), D_FF ** -0.5),
        "mix_norm": gain(ks[5], (DEPTH, D_MODEL)),
        "w_in": nrm(ks[6], (DEPTH, D_MODEL, IN_COLS), D_MODEL ** -0.5),
        "a_q_norm": gain(ks[7], (DEPTH, HEAD_DIM)),
        "a_k_norm": gain(ks[8], (DEPTH, HEAD_DIM)),
        "a_rel_bias": nrm(ks[9], (DEPTH, HEADS_PER_GROUP, 2 * REL_CLIP + 1), 0.2),
        "b_conv_w": nrm(ks[10], (DEPTH, CONV_WIDTH, 2 * GROUP_WIDTH), CONV_WIDTH ** -0.5),
        "b_conv_b": nrm(ks[11], (DEPTH, 2 * GROUP_WIDTH), 0.02),
        "b_gate_bias": jnp.stack([i_bias, f_bias], axis=1),
        "b_out_norm": gain(ks[12], (DEPTH, HEADS_PER_GROUP, HEAD_DIM)),
        "c_q_norm": gain(ks[15], (DEPTH, DIFF_QK_DIM)),
        "c_k_norm": gain(ks[16], (DEPTH, DIFF_QK_DIM)),
        "c_lambda": nrm(ks[17], (DEPTH, 4, DIFF_QK_DIM), 0.1),
        "c_out_norm": gain(ks[18], (DEPTH, HEAD_DIM)),
        "w_out": nrm(ks[19], (DEPTH, D_MODEL, D_MODEL), D_MODEL ** -0.5),
        "ffn2_norm": gain(ks[20], (DEPTH, D_MODEL)),
        "ffn2_wg": nrm(ks[21], (DEPTH, D_MODEL, D_FF), D_MODEL ** -0.5),
        "ffn2_wu": nrm(ks[22], (DEPTH, D_MODEL, D_FF), D_MODEL ** -0.5),
        "ffn2_wd": nrm(ks[23], (DEPTH, D_FF, D_MODEL), D_FF ** -0.5),
    }


def reference(x, ffn1_norm, ffn1_wg, ffn1_wu, ffn1_wd, mix_norm, w_in, a_q_norm, a_k_norm, a_rel_bias,
              b_conv_w, b_conv_b, b_gate_bias, b_out_norm, c_q_norm, c_k_norm, c_lambda, c_out_norm,
              w_out, ffn2_norm, ffn2_wg, ffn2_wu, ffn2_wd):
    for l in range(DEPTH):
        lam_init = 0.8 - 0.6 * math.exp(-0.3 * l)
        x = x + 0.5 * swiglu(rms_norm(x, ffn1_norm[l]), ffn1_wg[l], ffn1_wu[l], ffn1_wd[l])
        h = rms_norm(x, mix_norm[l])
        y = hybrid_mix(h, w_in[l], a_q_norm[l], a_k_norm[l], a_rel_bias[l], b_conv_w[l], b_conv_b[l],
                       b_gate_bias[l], b_out_norm[l], c_q_norm[l], c_k_norm[l], c_lambda[l], c_out_norm[l],
                       lam_init)
        x = x + y @ w_out[l]
        x = x + 0.5 * swiglu(rms_norm(x, ffn2_norm[l]), ffn2_wg[l], ffn2_wu[l], ffn2_wd[l])
    return x
```

```python
import functools
import math

import jax
import jax.numpy as jnp
import numpy as np
from jax import lax
from jax.experimental import pallas as pl
from jax.experimental.pallas import tpu as pltpu

F32 = jnp.float32
BF16 = jnp.bfloat16

D_MODEL = 1024
D_FF = 2816
CHUNK = 64
HEAD_DIM = 64
N_HEADS = 4
GROUP = N_HEADS * HEAD_DIM
LEFT_CHUNKS = 8
REL_CLIP = 128
CONV_WIDTH = 4
DIFF_QK_DIM = HEAD_DIM // 2
EPS = 1e-6
NEG = -1e30
LOG2E = 1.4426950408889634

OFF_AQ, OFF_AK, OFF_AV = 0, 256, 512
OFF_BQK, OFF_BV, OFF_BO = 768, 1280, 1536
OFF_CQ, OFF_CK, OFF_CV = 1792, 2048, 2304
OFF_DQ, OFF_DK, OFF_DV = 2560, 2816, 3072
MAIN_COLS = 3328
GATE_LANES = 128
GATE_ROWS = 16

VMEM_LIMIT = 56 * 1024 * 1024

TM_FFN = 512
TM_PROJ = 256
TQ_A = 256
LB = 256
TQ_C = 256
TK_C = 256
TQ_D = 256
TK_D = 256
SB_SKIP = -110.0


def _cparams(sem):
    return pltpu.CompilerParams(dimension_semantics=sem, vmem_limit_bytes=VMEM_LIMIT)


def _const_spec(shape):
    nd = len(shape)
    return pl.BlockSpec(shape, lambda *_: (0,) * nd, pipeline_mode=pl.Buffered(1))


def _split_dot(x, mat, terms, x_is_lhs=True):
    acc = None
    rem = x
    for _ in range(terms):
        part = rem.astype(BF16)
        rem = rem - part.astype(F32)
        d = (jnp.dot(part, mat, preferred_element_type=F32) if x_is_lhs
             else jnp.dot(mat, part, preferred_element_type=F32))
        acc = d if acc is None else acc + d
    return acc


def _group_ones(width):
    r = lax.broadcasted_iota(jnp.int32, (GROUP, GROUP), 0) // width
    c = lax.broadcasted_iota(jnp.int32, (GROUP, GROUP), 1) // width
    return jnp.where(r == c, 1.0, 0.0).astype(BF16)


def _group_rms(x, gain, width):
    ss = _split_dot(x * x, _group_ones(width), 2)
    return x * lax.rsqrt(ss * (1.0 / width) + EPS) * gain


def _log_sigmoid(x):
    return jnp.minimum(x, 0.0) - jnp.log1p(jnp.exp(-jnp.abs(x)))


def _head_mask(shape, h, axis=1, width=HEAD_DIM):
    lane = lax.broadcasted_iota(jnp.int32, shape, axis)
    return (lane // width) == h


def _ffn_kernel(x_ref, g_ref, wg_ref, wu_ref, wd_ref, o_ref):
    x = x_ref[...]
    xn = x * lax.rsqrt(jnp.mean(x * x, axis=-1, keepdims=True) + EPS) * g_ref[...]
    xb = xn.astype(BF16)
    g = jnp.dot(xb, wg_ref[...], preferred_element_type=F32)
    u = jnp.dot(xb, wu_ref[...], preferred_element_type=F32)
    h = (g * jax.nn.sigmoid(g) * u).astype(BF16)
    o_ref[...] = x + 0.5 * jnp.dot(h, wd_ref[...], preferred_element_type=F32)


def _ffn(x, gain, wg, wu, wd):
    t = x.shape[0]
    tm = min(TM_FFN, t)
    return pl.pallas_call(
        _ffn_kernel,
        grid=(t // tm,),
        in_specs=[pl.BlockSpec((tm, D_MODEL), lambda i: (i, 0)),
                  _const_spec((1, D_MODEL)),
                  _const_spec((D_MODEL, D_FF)),
                  _const_spec((D_MODEL, D_FF)),
                  _const_spec((D_FF, D_MODEL))],
        out_specs=pl.BlockSpec((tm, D_MODEL), lambda i: (i, 0)),
        out_shape=jax.ShapeDtypeStruct((t, D_MODEL), F32),
        compiler_params=_cparams(("parallel",)),
        name="ffn_half_step",
    )(x, gain.reshape(1, D_MODEL), wg, wu, wd)


def _proj_kernel(x_ref, g_ref, w_ref, wgc_ref, wgr_ref, gbc_ref, gbr_ref,
                 aqn_ref, akn_ref, cqn_ref, ckn_ref,
                 aq_ref, ak_ref, av_ref, bqk_ref, bv_ref, bo_ref, gc_ref, gr_ref,
                 cq_ref, ck_ref, cv_ref, dq_ref, dk_ref, dv_ref):
    x = x_ref[...]
    hn = x * lax.rsqrt(jnp.mean(x * x, axis=-1, keepdims=True) + EPS) * g_ref[...]
    hb = hn.astype(BF16)

    def cols(off, width=GROUP):
        return jnp.dot(hb, w_ref[:, off:off + width], preferred_element_type=F32)

    aq_ref[...] = (_group_rms(cols(OFF_AQ), aqn_ref[...], HEAD_DIM) * (HEAD_DIM ** -0.5)).astype(BF16)
    ak_ref[...] = _group_rms(cols(OFF_AK), akn_ref[...], HEAD_DIM).astype(BF16)
    av_ref[...] = cols(OFF_AV).astype(BF16)
    bqk_ref[...] = cols(OFF_BQK, 2 * GROUP)
    bv_ref[...] = cols(OFF_BV).astype(BF16)
    bo_ref[...] = jax.nn.sigmoid(cols(OFF_BO))
    gcol = jnp.dot(hb, wgc_ref[...], preferred_element_type=F32) + gbc_ref[...]
    lane = lax.broadcasted_iota(jnp.int32, gcol.shape, 1)
    gc_ref[...] = jnp.where(lane >= N_HEADS, _log_sigmoid(gcol), gcol)
    grow = lax.dot_general(wgr_ref[...], hb, (((1,), (1,)), ((), ())),
                           preferred_element_type=F32) + gbr_ref[...]
    row = lax.broadcasted_iota(jnp.int32, grow.shape, 0)
    gr_ref[...] = jnp.where(row >= N_HEADS, _log_sigmoid(grow), grow)
    cq_ref[...] = (_group_rms(cols(OFF_CQ), cqn_ref[...], DIFF_QK_DIM)
                   * (DIFF_QK_DIM ** -0.5 * LOG2E)).astype(BF16)
    ck_ref[...] = _group_rms(cols(OFF_CK), ckn_ref[...], DIFF_QK_DIM).astype(BF16)
    cv_ref[...] = cols(OFF_CV).astype(BF16)
    dq_ref[...] = (cols(OFF_DQ) * (HEAD_DIM ** -0.5)).astype(BF16)
    dk_ref[...] = cols(OFF_DK).astype(BF16)
    dv_ref[...] = cols(OFF_DV).astype(BF16)


def _proj(x, gain, w_main, w_gc, w_gr, gb_col, gb_row, aqn, akn, cqn, ckn):
    t = x.shape[0]
    tm = min(TM_PROJ, t)
    row_spec = lambda w: pl.BlockSpec((tm, w), lambda i: (i, 0))
    bf = lambda w: jax.ShapeDtypeStruct((t, w), BF16)
    f32 = lambda w: jax.ShapeDtypeStruct((t, w), F32)
    out_shape = [bf(GROUP), bf(GROUP), bf(GROUP),
                 f32(2 * GROUP), bf(GROUP), f32(GROUP),
                 f32(GATE_LANES), jax.ShapeDtypeStruct((GATE_ROWS, t), F32),
                 bf(GROUP), bf(GROUP), bf(GROUP),
                 bf(GROUP), bf(GROUP), bf(GROUP)]
    out_specs = [row_spec(GROUP)] * 3 + [row_spec(2 * GROUP), row_spec(GROUP), row_spec(GROUP),
                                         row_spec(GATE_LANES),
                                         pl.BlockSpec((GATE_ROWS, tm), lambda i: (0, i))] + [row_spec(GROUP)] * 6
    return pl.pallas_call(
        _proj_kernel,
        grid=(t // tm,),
        in_specs=[row_spec(D_MODEL), _const_spec((1, D_MODEL)),
                  _const_spec((D_MODEL, MAIN_COLS)), _const_spec((D_MODEL, GATE_LANES)),
                  _const_spec((GATE_ROWS, D_MODEL)), _const_spec((1, GATE_LANES)),
                  _const_spec((GATE_ROWS, 1)),
                  _const_spec((1, GROUP)), _const_spec((1, GROUP)),
                  _const_spec((1, GROUP)), _const_spec((1, GROUP))],
        out_specs=out_specs,
        out_shape=out_shape,
        compiler_params=_cparams(("parallel",)),
        name="mix_in_proj",
    )(x, gain.reshape(1, D_MODEL), w_main, w_gc, w_gr, gb_col, gb_row, aqn, akn, cqn, ckn)


def _mix_a_kernel(q_ref, k0_ref, k1_ref, k2_ref, v0_ref, v1_ref, v2_ref, bias_ref, o_ref):
    t = pl.program_id(1)
    q = q_ref[...]
    k = jnp.concatenate([k0_ref[...], k1_ref[...], k2_ref[...]], axis=0)
    v = jnp.concatenate([v0_ref[...], v1_ref[...], v2_ref[...]], axis=0)
    nk = k.shape[0]
    key_pos = t * TQ_A - LEFT_CHUNKS * CHUNK + lax.broadcasted_iota(jnp.int32, (TQ_A, nk), 1)
    valid = key_pos >= 0
    out = jnp.zeros((TQ_A, GROUP), F32)
    for h in range(N_HEADS):
        qh = jnp.where(_head_mask(q.shape, h), q, jnp.zeros_like(q))
        s = lax.dot_general(qh, k, (((1,), (1,)), ((), ())), preferred_element_type=F32)
        s = jnp.where(valid, s + bias_ref[h], NEG)
        m = jnp.max(s, axis=-1, keepdims=True)
        e = jnp.exp(s - m)
        p = (e / jnp.sum(e, axis=-1, keepdims=True)).astype(BF16)
        vh = jnp.where(_head_mask(v.shape, h), v, jnp.zeros_like(v))
        out = out + jnp.dot(p, vh, preferred_element_type=F32)
    o_ref[...] = out.astype(BF16)


def _mix_a(q, k, v, bias_tile, nb, s):
    nt = s // TQ_A
    blk = lambda back: pl.BlockSpec((TQ_A, GROUP), lambda b, t: (b * nt + jnp.maximum(t - back, 0), 0))
    return pl.pallas_call(
        _mix_a_kernel,
        grid=(nb, nt),
        in_specs=[blk(0), blk(2), blk(1), blk(0), blk(2), blk(1), blk(0),
                  _const_spec(bias_tile.shape)],
        out_specs=blk(0),
        out_shape=jax.ShapeDtypeStruct((nb * s, GROUP), BF16),
        compiler_params=_cparams(("parallel", "parallel")),
        name="mix_a_chunk_attn",
    )(q, k, k, k, v, v, v, bias_tile)


def _mix_b_kernel(qk_ref, v_ref, og_ref, gc_ref, gr_ref, cw_ref, cb_ref, on_ref, o_ref,
                  xs_ref, c_ref, n_ref, m_ref):
    c_idx = pl.program_id(1)

    @pl.when(c_idx == 0)
    def _():
        xs_ref[0:8, :] = jnp.zeros((8, 2 * GROUP), F32)
        c_ref[...] = jnp.zeros_like(c_ref)
        n_ref[...] = jnp.zeros_like(n_ref)
        m_ref[...] = jnp.zeros_like(m_ref)

    xs_ref[8:8 + LB, :] = qk_ref[...]
    acc = jnp.broadcast_to(cb_ref[...], (LB, 2 * GROUP))
    for j in range(CONV_WIDTH):
        start = 8 - (CONV_WIDTH - 1) + j
        acc = acc + xs_ref[start:start + LB, :] * cw_ref[j:j + 1, :]
    xs_ref[0:8, :] = xs_ref[LB:LB + 8, :]
    qk = acc * jax.nn.sigmoid(acc)
    q = qk[:, :GROUP]
    k = qk[:, GROUP:] * (HEAD_DIM ** -0.5)
    qb = q.astype(BF16)
    kb = k.astype(BF16)
    v = v_ref[...]

    r = lax.broadcasted_iota(jnp.int32, (LB, LB), 0)
    c = lax.broadcasted_iota(jnp.int32, (LB, LB), 1)
    causal = c <= r
    tri = jnp.where(causal, 1.0, 0.0).astype(BF16)
    tri_t = jnp.where(r <= c, 1.0, 0.0).astype(BF16)
    gcol = gc_ref[...]
    grow = gr_ref[...]
    bcum_col = _split_dot(gcol, tri, 3, x_is_lhs=False)
    bcum_row = _split_dot(grow, tri_t, 3)

    c_state = c_ref[...]
    n_state = n_ref[...]
    q_c = jnp.dot(qb, c_state.astype(BF16), preferred_element_type=F32)
    qn = qb.astype(F32) * n_state.astype(BF16).astype(F32)
    q_n = _split_dot(qn, _group_ones(HEAD_DIM), 2)

    lane = lax.broadcasted_iota(jnp.int32, (LB, GROUP), 1) // HEAD_DIM
    num = jnp.zeros((LB, GROUP), F32)
    den = jnp.zeros((LB, GROUP), F32)
    floor = jnp.zeros((LB, GROUP), F32)
    wg_full = jnp.zeros((LB, GROUP), F32)
    a_full = jnp.zeros((1, GROUP), F32)
    lane1 = lax.broadcasted_iota(jnp.int32, (1, GROUP), 1) // HEAD_DIM
    a_list = []
    for h in range(N_HEADS):
        i_col = gcol[:, h:h + 1]
        i_row = grow[h:h + 1, :]
        b_col = bcum_col[:, N_HEADS + h:N_HEADS + h + 1]
        b_row = bcum_row[N_HEADS + h:N_HEADS + h + 1, :]
        m_prev = m_ref[h:h + 1, 0:1]
        dmat = jnp.where(causal, b_col - b_row + i_row, NEG)
        inter = b_col + m_prev
        m_t = jnp.maximum(inter, jnp.max(dmat, axis=-1, keepdims=True))
        w_intra = jnp.exp(dmat - m_t)
        s_inter = jnp.exp(inter - m_t)
        qh = jnp.where(_head_mask(qb.shape, h), qb, jnp.zeros_like(qb))
        sc = lax.dot_general(qh, kb, (((1,), (1,)), ((), ())), preferred_element_type=F32) * w_intra
        vh = jnp.where(_head_mask(v.shape, h), v, jnp.zeros_like(v))
        num_h = s_inter * q_c + jnp.dot(sc.astype(BF16), vh, preferred_element_type=F32)
        den_h = s_inter * q_n + jnp.sum(sc, axis=-1, keepdims=True)
        sel = lane == h
        num = jnp.where(sel, num_h, num)
        den = jnp.where(sel, den_h, den)
        floor = jnp.where(sel, jnp.exp(-m_t), floor)
        b_tot = b_col[LB - 1:LB, :]
        g = b_tot - b_col + i_col
        m_new = jnp.maximum(b_tot + m_prev, jnp.max(g, axis=0, keepdims=True))
        a_h = jnp.exp(b_tot + m_prev - m_new)
        a_list.append(a_h)
        a_full = jnp.where(lane1 == h, a_h, a_full)
        wg_full = jnp.where(sel, jnp.exp(g - m_new), wg_full)
        m_ref[h:h + 1, :] = jnp.broadcast_to(m_new, (1, m_ref.shape[1]))

    hb = num / jnp.maximum(jnp.abs(den), floor)
    o_ref[...] = (_group_rms(hb, on_ref[...], HEAD_DIM) * og_ref[...]).astype(BF16)

    kw = k * wg_full
    kv = jnp.dot(kw.T.astype(BF16), v, preferred_element_type=F32)
    r2 = lax.broadcasted_iota(jnp.int32, (GROUP, GROUP), 0) // HEAD_DIM
    c2 = lax.broadcasted_iota(jnp.int32, (GROUP, GROUP), 1) // HEAD_DIM
    a_mat = jnp.zeros((GROUP, GROUP), F32)
    for h in range(N_HEADS):
        a_mat = jnp.where(r2 == h, a_list[h], a_mat)
    c_ref[...] = a_mat * c_state + jnp.where(r2 == c2, kv, 0.0)
    n_ref[...] = a_full * n_state + jnp.sum(kw, axis=0, keepdims=True)


def _mix_b(bqk, bv, bo, gcol, grow, conv_w, conv_b, out_norm, nb, s):
    nc = s // LB
    row = lambda w: pl.BlockSpec((LB, w), lambda b, c: (b * nc + c, 0))
    return pl.pallas_call(
        _mix_b_kernel,
        grid=(nb, nc),
        in_specs=[row(2 * GROUP), row(GROUP), row(GROUP), row(GATE_LANES),
                  pl.BlockSpec((GATE_ROWS, LB), lambda b, c: (0, b * nc + c)),
                  _const_spec((CONV_WIDTH, 2 * GROUP)), _const_spec((1, 2 * GROUP)),
                  _const_spec((1, GROUP))],
        out_specs=row(GROUP),
        out_shape=jax.ShapeDtypeStruct((nb * s, GROUP), BF16),
        scratch_shapes=[pltpu.VMEM((LB + 8, 2 * GROUP), F32),
                        pltpu.VMEM((GROUP, GROUP), F32),
                        pltpu.VMEM((1, GROUP), F32),
                        pltpu.VMEM((8, 128), F32)],
        compiler_params=_cparams(("parallel", "arbitrary")),
        name="mix_b_mlstm",
    )(bqk, bv, bo, gcol, grow, conv_w, conv_b, out_norm)


def _mix_c_kernel(lam_ref, q_ref, k_ref, v_ref, on_ref, o_ref, m_ref, l_ref, acc_ref, *, lam_init):
    qt = pl.program_id(1)
    q = q_ref[...]
    n_maps = 2 * N_HEADS
    qm = [jnp.where(_head_mask(q.shape, j, width=DIFF_QK_DIM), q, jnp.zeros_like(q)) for j in range(n_maps)]
    m_ref[...] = jnp.full(m_ref.shape, NEG, F32)
    l_ref[...] = jnp.zeros(l_ref.shape, F32)
    acc_ref[...] = jnp.zeros(acc_ref.shape, F32)
    lane = lax.broadcasted_iota(jnp.int32, (TQ_C, GROUP), 1) // HEAD_DIM

    def tile(kt, masked):
        start = pl.multiple_of(kt * TK_C, TK_C)
        k = k_ref[pl.ds(start, TK_C), :]
        v = v_ref[pl.ds(start, TK_C), :]
        vh = [jnp.where(_head_mask(v.shape, h), v, jnp.zeros_like(v)) for h in range(N_HEADS)]
        if masked:
            rq = lax.broadcasted_iota(jnp.int32, (TQ_C, TK_C), 0) // CHUNK
            ck = lax.broadcasted_iota(jnp.int32, (TQ_C, TK_C), 1) // CHUNK
            vis = ck <= rq
        for comp in range(2):
            alpha_full = jnp.zeros((TQ_C, GROUP), F32)
            pv = jnp.zeros((TQ_C, GROUP), F32)
            for h in range(N_HEADS):
                j = 2 * h + comp
                s = lax.dot_general(qm[j], k, (((1,), (1,)), ((), ())), preferred_element_type=F32)
                if masked:
                    s = jnp.where(vis, s, NEG)
                m_prev = m_ref[j]
                m_new = jnp.maximum(m_prev, jnp.max(s, axis=-1, keepdims=True))
                alpha = jnp.exp2(m_prev - m_new)
                p = jnp.exp2(s - m_new[:, 0:1])
                l_ref[j] = alpha * l_ref[j] + jnp.sum(p, axis=-1, keepdims=True)
                m_ref[j] = m_new
                alpha_full = jnp.where(lane == h, alpha[:, 0:1], alpha_full)
                pv = pv + jnp.dot(p.astype(BF16), vh[h], preferred_element_type=F32)
            acc_ref[comp] = alpha_full * acc_ref[comp] + pv

    def body(kt, carry):
        tile(kt, False)
        return carry

    lax.fori_loop(0, qt, body, 0)
    tile(qt, True)

    lam = lam_ref[0]
    inv = []
    for comp in range(2):
        l_full = jnp.zeros((TQ_C, GROUP), F32)
        for h in range(N_HEADS):
            l_full = jnp.where(lane == h, l_ref[2 * h + comp][:, 0:1], l_full)
        inv.append(1.0 / l_full)
    out = acc_ref[0] * inv[0] - lam * (acc_ref[1] * inv[1])
    o_ref[...] = (_group_rms(out, on_ref[...], HEAD_DIM) * (1.0 - lam_init)).astype(BF16)


def _mix_c(lam, q, k, v, out_norm, nb, s, lam_init):
    nt = s // TQ_C
    full = pl.BlockSpec((s, GROUP), lambda b, t, lam_ref: (b, 0), pipeline_mode=pl.Buffered(1))
    tile = pl.BlockSpec((TQ_C, GROUP), lambda b, t, lam_ref: (b * nt + t, 0))
    grid_spec = pltpu.PrefetchScalarGridSpec(
        num_scalar_prefetch=1,
        grid=(nb, nt),
        in_specs=[tile, full, full,
                  pl.BlockSpec((1, GROUP), lambda b, t, lam_ref: (0, 0), pipeline_mode=pl.Buffered(1))],
        out_specs=tile,
        scratch_shapes=[pltpu.VMEM((2 * N_HEADS, TQ_C, 128), F32),
                        pltpu.VMEM((2 * N_HEADS, TQ_C, 128), F32),
                        pltpu.VMEM((2, TQ_C, GROUP), F32)],
    )
    return pl.pallas_call(
        functools.partial(_mix_c_kernel, lam_init=lam_init),
        grid_spec=grid_spec,
        out_shape=jax.ShapeDtypeStruct((nb * s, GROUP), BF16),
        compiler_params=_cparams(("parallel", "parallel")),
        name="mix_c_diff_attn",
    )(lam, q, k, v, out_norm)


def _mix_d_kernel(q_ref, k_ref, v_ref, o_ref):
    qt = pl.program_id(1)
    q = q_ref[...]
    r = lax.broadcasted_iota(jnp.int32, (TK_D, TK_D), 0)
    c = lax.broadcasted_iota(jnp.int32, (TK_D, TK_D), 1)
    upper = jnp.where(r > c, 1.0, 0.0).astype(BF16)
    before_diag = c < r
    lane = lax.broadcasted_iota(jnp.int32, (TQ_D, GROUP), 1) // HEAD_DIM
    out = jnp.zeros((TQ_D, GROUP), F32)

    for h in range(N_HEADS):
        qh = jnp.where(_head_mask(q.shape, h), q, jnp.zeros_like(q))

        def tile(kt, run, acc, masked, qh=qh, h=h):
            start = pl.multiple_of(kt * TK_D, TK_D)
            k = k_ref[pl.ds(start, TK_D), :]
            v = v_ref[pl.ds(start, TK_D), :]
            z = lax.dot_general(qh, k, (((1,), (1,)), ((), ())), preferred_element_type=F32)
            log_keep = _log_sigmoid(-z)
            log_take = z + log_keep
            if masked:
                log_keep = jnp.where(before_diag, log_keep, 0.0)
            between = run + _split_dot(log_keep, upper, 2)
            a = jnp.exp(log_take + between)
            if masked:
                a = jnp.where(before_diag, a, 0.0)
            vh = jnp.where(_head_mask(v.shape, h), v, jnp.zeros_like(v))
            acc = acc + jnp.dot(a.astype(BF16), vh, preferred_element_type=F32)
            run = run + jnp.sum(log_keep, axis=-1, keepdims=True)
            return run, acc

        run0 = jnp.zeros((TQ_D, 1), F32)
        run, acc = tile(qt, run0, jnp.zeros((TQ_D, GROUP), F32), True)

        def cond(state):
            kt, run, _ = state
            return jnp.logical_and(kt >= 0, jnp.max(run) > SB_SKIP)

        def body(state):
            kt, run, acc = state
            run, acc = tile(kt, run, acc, False)
            return kt - 1, run, acc

        _, _, acc = lax.while_loop(cond, body, (qt - 1, run, acc))
        out = jnp.where(lane == h, acc, out)

    o_ref[...] = out.astype(BF16)


def _mix_d(q, k, v, nb, s):
    nt = s // TQ_D
    full = pl.BlockSpec((s, GROUP), lambda b, t: (b, 0), pipeline_mode=pl.Buffered(1))
    tile = pl.BlockSpec((TQ_D, GROUP), lambda b, t: (b * nt + t, 0))
    return pl.pallas_call(
        _mix_d_kernel,
        grid=(nb, nt),
        in_specs=[tile, full, full],
        out_specs=tile,
        out_shape=jax.ShapeDtypeStruct((nb * s, GROUP), BF16),
        compiler_params=_cparams(("parallel", "parallel")),
        name="mix_d_stick_breaking",
    )(q, k, v)


def _out_proj_kernel(x_ref, ya_ref, yb_ref, yc_ref, yd_ref, w_ref, o_ref):
    acc = x_ref[...]
    for g, y_ref in enumerate((ya_ref, yb_ref, yc_ref, yd_ref)):
        acc = acc + jnp.dot(y_ref[...], w_ref[g * GROUP:(g + 1) * GROUP, :], preferred_element_type=F32)
    o_ref[...] = acc


def _out_proj(x, ya, yb, yc, yd, w_out):
    t = x.shape[0]
    tm = min(TM_FFN, t)
    row = lambda w: pl.BlockSpec((tm, w), lambda i: (i, 0))
    return pl.pallas_call(
        _out_proj_kernel,
        grid=(t // tm,),
        in_specs=[row(D_MODEL), row(GROUP), row(GROUP), row(GROUP), row(GROUP),
                  _const_spec((D_MODEL, D_MODEL))],
        out_specs=row(D_MODEL),
        out_shape=jax.ShapeDtypeStruct((t, D_MODEL), F32),
        compiler_params=_cparams(("parallel",)),
        name="mix_out_proj",
    )(x, ya, yb, yc, yd, w_out)


def _rel_bias_tile(rel_bias):
    nk = TQ_A + LEFT_CHUNKS * CHUNK
    i = np.arange(TQ_A)[:, None]
    j = np.arange(nk)[None, :]
    rel = LEFT_CHUNKS * CHUNK + i - j
    idx = np.clip(rel, -REL_CLIP, REL_CLIP) + REL_CLIP
    dc = j // CHUNK - i // CHUNK
    visible = (dc >= 0) & (dc <= LEFT_CHUNKS)
    return jnp.where(jnp.asarray(visible)[None], rel_bias[:, idx].astype(F32), NEG)


def _permute_w_in(w_in):
    sizes = [GROUP, GROUP, GROUP, 2 * GROUP, GROUP, GROUP, N_HEADS, N_HEADS,
             GROUP, GROUP, GROUP, GROUP, GROUP, GROUP]
    offs = np.concatenate([[0], np.cumsum(sizes)])
    seg = [w_in[:, offs[n]:offs[n + 1]] for n in range(len(sizes))]
    main = jnp.concatenate(seg[0:6] + seg[8:14], axis=1).astype(BF16)
    gates = jnp.concatenate([seg[6], seg[7]], axis=1)
    g_col = jnp.pad(gates, ((0, 0), (0, GATE_LANES - 2 * N_HEADS))).astype(BF16)
    g_row = jnp.pad(gates.T, ((0, GATE_ROWS - 2 * N_HEADS), (0, 0))).astype(BF16)
    return main, g_col, g_row


def _layer(x, nb, s, lam_init, p):
    x = _ffn(x, p["ffn1_norm"], p["ffn1_wg"].astype(BF16), p["ffn1_wu"].astype(BF16), p["ffn1_wd"].astype(BF16))

    w_main, w_gc, w_gr = _permute_w_in(p["w_in"])
    gate_bias = p["b_gate_bias"].reshape(2 * N_HEADS).astype(F32)
    gb_col = jnp.pad(gate_bias, (0, GATE_LANES - 2 * N_HEADS)).reshape(1, GATE_LANES)
    gb_row = jnp.pad(gate_bias, (0, GATE_ROWS - 2 * N_HEADS)).reshape(GATE_ROWS, 1)
    tile4 = lambda g: jnp.tile(g.astype(F32), GROUP // g.shape[0]).reshape(1, GROUP)
    (aq, ak, av, bqk, bv, bo, gcol, grow, cq, ck, cv, dq, dk, dv) = _proj(
        x, p["mix_norm"], w_main, w_gc, w_gr, gb_col, gb_row,
        tile4(p["a_q_norm"]), tile4(p["a_k_norm"]), tile4(p["c_q_norm"]), tile4(p["c_k_norm"]))

    ya = _mix_a(aq, ak, av, _rel_bias_tile(p["a_rel_bias"]), nb, s)
    yb = _mix_b(bqk, bv, bo, gcol, grow, p["b_conv_w"].astype(F32), p["b_conv_b"].astype(F32).reshape(1, -1),
                p["b_out_norm"].astype(F32).reshape(1, GROUP), nb, s)
    lv = p["c_lambda"].astype(F32)
    lam = jnp.exp(jnp.sum(lv[0] * lv[1])) - jnp.exp(jnp.sum(lv[2] * lv[3])) + lam_init
    yc = _mix_c(lam.reshape(1), cq, ck, cv, tile4(p["c_out_norm"]), nb, s, lam_init)
    yd = _mix_d(dq, dk, dv, nb, s)

    x = _out_proj(x, ya, yb, yc, yd, p["w_out"].astype(BF16))
    return _ffn(x, p["ffn2_norm"], p["ffn2_wg"].astype(BF16), p["ffn2_wu"].astype(BF16), p["ffn2_wd"].astype(BF16))


_PARAM_NAMES = ("ffn1_norm", "ffn1_wg", "ffn1_wu", "ffn1_wd", "mix_norm", "w_in", "a_q_norm", "a_k_norm",
                "a_rel_bias", "b_conv_w", "b_conv_b", "b_gate_bias", "b_out_norm", "c_q_norm", "c_k_norm",
                "c_lambda", "c_out_norm", "w_out", "ffn2_norm", "ffn2_wg", "ffn2_wu", "ffn2_wd")


def kernel(x, ffn1_norm, ffn1_wg, ffn1_wu, ffn1_wd, mix_norm, w_in, a_q_norm, a_k_norm, a_rel_bias,
           b_conv_w, b_conv_b, b_gate_bias, b_out_norm, c_q_norm, c_k_norm, c_lambda, c_out_norm,
           w_out, ffn2_norm, ffn2_wg, ffn2_wu, ffn2_wd):
    params = dict(zip(_PARAM_NAMES, (ffn1_norm, ffn1_wg, ffn1_wu, ffn1_wd, mix_norm, w_in, a_q_norm, a_k_norm,
                                     a_rel_bias, b_conv_w, b_conv_b, b_gate_bias, b_out_norm, c_q_norm,
                                     c_k_norm, c_lambda, c_out_norm, w_out, ffn2_norm, ffn2_wg, ffn2_wu,
                                     ffn2_wd)))
    nb, s, d = x.shape
    depth = ffn1_norm.shape[0]
    h = x.reshape(nb * s, d)
    for l in range(depth):
        lam_init = 0.8 - 0.6 * math.exp(-0.3 * l)
        h = _layer(h, nb, s, lam_init, {k: v[l] for k, v in params.items()})
    return h.reshape(nb, s, d)
```

```python
import functools
import math

import jax
import jax.numpy as jnp
import numpy as np
from jax import lax
from jax.experimental import pallas as pl
from jax.experimental.pallas import tpu as pltpu

F32 = jnp.float32
BF16 = jnp.bfloat16

D_MODEL = 1024
D_FF = 2816
CHUNK = 64
HEAD_DIM = 64
N_HEADS = 4
GROUP = N_HEADS * HEAD_DIM
LEFT_CHUNKS = 8
REL_CLIP = 128
CONV_WIDTH = 4
DIFF_QK_DIM = HEAD_DIM // 2
EPS = 1e-6
NEG = -1e30
LOG2E = 1.4426950408889634

OFF_AQ, OFF_AK, OFF_AV = 0, 256, 512
OFF_BQK, OFF_BV, OFF_BO = 768, 1280, 1536
OFF_CK = 1792
OFF_DQ, OFF_DK, OFF_DV = 2048, 2304, 2560
MAIN_COLS = 2816
GATE_LANES = 128
GATE_ROWS = 16

VMEM_LIMIT = 56 * 1024 * 1024

TM_FFN = 512
TM_PROJ = 256
TQ_A = 256
LB = 256
TQ_C = 256
TK_C = 256
TQ_D = 256
TK_D = 256
SB_SKIP = -110.0


def _cparams(sem):
    return pltpu.CompilerParams(dimension_semantics=sem, vmem_limit_bytes=VMEM_LIMIT)


def _const_spec(shape):
    nd = len(shape)
    return pl.BlockSpec(shape, lambda *_: (0,) * nd, pipeline_mode=pl.Buffered(1))


def _split_dot(x, mat, terms, x_is_lhs=True):
    acc = None
    rem = x
    for _ in range(terms):
        part = rem.astype(BF16)
        rem = rem - part.astype(F32)
        d = (jnp.dot(part, mat, preferred_element_type=F32) if x_is_lhs
             else jnp.dot(mat, part, preferred_element_type=F32))
        acc = d if acc is None else acc + d
    return acc


def _group_ones(width):
    r = lax.broadcasted_iota(jnp.int32, (GROUP, GROUP), 0) // width
    c = lax.broadcasted_iota(jnp.int32, (GROUP, GROUP), 1) // width
    return jnp.where(r == c, 1.0, 0.0).astype(BF16)


def _group_rms(x, gain, width):
    ss = _split_dot(x * x, _group_ones(width), 2)
    return x * lax.rsqrt(ss * (1.0 / width) + EPS) * gain


def _log_sigmoid(x):
    return jnp.minimum(x, 0.0) - jnp.log1p(jnp.exp(-jnp.abs(x)))


def _head_mask(shape, h, axis=1, width=HEAD_DIM):
    lane = lax.broadcasted_iota(jnp.int32, shape, axis)
    return (lane // width) == h


def _ffn_kernel(x_ref, g_ref, wg_ref, wu_ref, wd_ref, o_ref):
    x = x_ref[...]
    xn = x * lax.rsqrt(jnp.mean(x * x, axis=-1, keepdims=True) + EPS) * g_ref[...]
    xb = xn.astype(BF16)
    g = jnp.dot(xb, wg_ref[...], preferred_element_type=F32)
    u = jnp.dot(xb, wu_ref[...], preferred_element_type=F32)
    h = (g * jax.nn.sigmoid(g) * u).astype(BF16)
    o_ref[...] = x + 0.5 * jnp.dot(h, wd_ref[...], preferred_element_type=F32)


def _ffn(x, gain, wg, wu, wd):
    t = x.shape[0]
    tm = min(TM_FFN, t)
    return pl.pallas_call(
        _ffn_kernel,
        grid=(t // tm,),
        in_specs=[pl.BlockSpec((tm, D_MODEL), lambda i: (i, 0)),
                  _const_spec((1, D_MODEL)),
                  _const_spec((D_MODEL, D_FF)),
                  _const_spec((D_MODEL, D_FF)),
                  _const_spec((D_FF, D_MODEL))],
        out_specs=pl.BlockSpec((tm, D_MODEL), lambda i: (i, 0)),
        out_shape=jax.ShapeDtypeStruct((t, D_MODEL), F32),
        compiler_params=_cparams(("parallel",)),
        name="ffn_half_step",
    )(x, gain.reshape(1, D_MODEL), wg, wu, wd)


def _proj_kernel(x_ref, g_ref, w_ref, wct_ref, wgc_ref, wgr_ref, gbc_ref, gbr_ref,
                 aqn_ref, akn_ref, cqn_ref, ckn_ref,
                 aq_ref, ak_ref, av_ref, bqk_ref, bv_ref, bo_ref, gc_ref, gr_ref,
                 cq_ref, ck_ref, cv_ref, dq_ref, dk_ref, dv_ref):
    x = x_ref[...]
    hn = x * lax.rsqrt(jnp.mean(x * x, axis=-1, keepdims=True) + EPS) * g_ref[...]
    hb = hn.astype(BF16)

    def cols(off, width=GROUP):
        return jnp.dot(hb, w_ref[:, off:off + width], preferred_element_type=F32)

    aq_ref[...] = (_group_rms(cols(OFF_AQ), aqn_ref[...], HEAD_DIM) * (HEAD_DIM ** -0.5)).astype(BF16)
    ak_ref[...] = _group_rms(cols(OFF_AK), akn_ref[...], HEAD_DIM).astype(BF16)
    av_ref[...] = cols(OFF_AV).astype(BF16)
    bqk_ref[...] = cols(OFF_BQK, 2 * GROUP)
    bv_ref[...] = cols(OFF_BV).astype(BF16)
    bo_ref[...] = jax.nn.sigmoid(cols(OFF_BO))
    gcol = jnp.dot(hb, wgc_ref[...], preferred_element_type=F32) + gbc_ref[...]
    lane = lax.broadcasted_iota(jnp.int32, gcol.shape, 1)
    gc_ref[...] = jnp.where(lane >= N_HEADS, _log_sigmoid(gcol), gcol)
    grow = lax.dot_general(wgr_ref[...], hb, (((1,), (1,)), ((), ())),
                           preferred_element_type=F32) + gbr_ref[...]
    row = lax.broadcasted_iota(jnp.int32, grow.shape, 0)
    gr_ref[...] = jnp.where(row >= N_HEADS, _log_sigmoid(grow), grow)
    ck_ref[...] = _group_rms(cols(OFF_CK), ckn_ref[...], DIFF_QK_DIM).astype(BF16)
    nt_dims = (((1,), (1,)), ((), ()))
    cqt = lax.dot_general(wct_ref[0:GROUP, :], hb, nt_dims, preferred_element_type=F32)
    ss = _split_dot(cqt * cqt, _group_ones(DIFF_QK_DIM), 2, x_is_lhs=False)
    cqt = cqt * lax.rsqrt(ss * (1.0 / DIFF_QK_DIM) + EPS) * cqn_ref[...]
    cq_ref[0] = (cqt * (DIFF_QK_DIM ** -0.5 * LOG2E)).astype(BF16)
    cv_ref[0] = lax.dot_general(wct_ref[GROUP:2 * GROUP, :], hb, nt_dims,
                                preferred_element_type=F32).astype(BF16)
    dq_ref[...] = (cols(OFF_DQ) * (HEAD_DIM ** -0.5)).astype(BF16)
    dk_ref[...] = cols(OFF_DK).astype(BF16)
    dv_ref[...] = cols(OFF_DV).astype(BF16)


def _proj(x, gain, w_main, w_ct, w_gc, w_gr, gb_col, gb_row, aqn, akn, cqn_col, ckn):
    t = x.shape[0]
    tm = TM_PROJ
    row_spec = lambda w: pl.BlockSpec((tm, w), lambda i: (i, 0))
    bf = lambda w: jax.ShapeDtypeStruct((t, w), BF16)
    f32 = lambda w: jax.ShapeDtypeStruct((t, w), F32)
    tr_shape = jax.ShapeDtypeStruct((t // tm, GROUP, tm), BF16)
    tr_spec = pl.BlockSpec((1, GROUP, tm), lambda i: (i, 0, 0))
    out_shape = [bf(GROUP), bf(GROUP), bf(GROUP),
                 f32(2 * GROUP), bf(GROUP), f32(GROUP),
                 f32(GATE_LANES), jax.ShapeDtypeStruct((GATE_ROWS, t), F32),
                 tr_shape, bf(GROUP), tr_shape,
                 bf(GROUP), bf(GROUP), bf(GROUP)]
    out_specs = ([row_spec(GROUP)] * 3 + [row_spec(2 * GROUP), row_spec(GROUP), row_spec(GROUP),
                                          row_spec(GATE_LANES), pl.BlockSpec((GATE_ROWS, tm), lambda i: (0, i))]
                 + [tr_spec, row_spec(GROUP), tr_spec] + [row_spec(GROUP)] * 3)
    return pl.pallas_call(
        _proj_kernel,
        grid=(t // tm,),
        in_specs=[row_spec(D_MODEL), _const_spec((1, D_MODEL)),
                  _const_spec((D_MODEL, MAIN_COLS)), _const_spec((2 * GROUP, D_MODEL)),
                  _const_spec((D_MODEL, GATE_LANES)),
                  _const_spec((GATE_ROWS, D_MODEL)), _const_spec((1, GATE_LANES)),
                  _const_spec((GATE_ROWS, 1)),
                  _const_spec((1, GROUP)), _const_spec((1, GROUP)),
                  _const_spec((GROUP, 1)), _const_spec((1, GROUP))],
        out_specs=out_specs,
        out_shape=out_shape,
        compiler_params=_cparams(("parallel",)),
        name="mix_in_proj",
    )(x, gain.reshape(1, D_MODEL), w_main, w_ct, w_gc, w_gr, gb_col, gb_row, aqn, akn, cqn_col, ckn)


def _mix_a_kernel(q_ref, k0_ref, k1_ref, k2_ref, v0_ref, v1_ref, v2_ref, bias_ref, o_ref):
    t = pl.program_id(1)
    q = q_ref[...]
    k = jnp.concatenate([k0_ref[...], k1_ref[...], k2_ref[...]], axis=0)
    v = jnp.concatenate([v0_ref[...], v1_ref[...], v2_ref[...]], axis=0)
    nk = k.shape[0]
    key_pos = t * TQ_A - LEFT_CHUNKS * CHUNK + lax.broadcasted_iota(jnp.int32, (TQ_A, nk), 1)
    valid = key_pos >= 0
    out = jnp.zeros((TQ_A, GROUP), F32)
    for h in range(N_HEADS):
        qh = jnp.where(_head_mask(q.shape, h), q, jnp.zeros_like(q))
        s = lax.dot_general(qh, k, (((1,), (1,)), ((), ())), preferred_element_type=F32)
        s = jnp.where(valid, s + bias_ref[h], NEG)
        m = jnp.max(s, axis=-1, keepdims=True)
        e = jnp.exp(s - m)
        p = (e / jnp.sum(e, axis=-1, keepdims=True)).astype(BF16)
        vh = jnp.where(_head_mask(v.shape, h), v, jnp.zeros_like(v))
        out = out + jnp.dot(p, vh, preferred_element_type=F32)
    o_ref[...] = out.astype(BF16)


def _mix_a(q, k, v, bias_tile, nb, s):
    nt = s // TQ_A
    blk = lambda back: pl.BlockSpec((TQ_A, GROUP), lambda b, t: (b * nt + jnp.maximum(t - back, 0), 0))
    return pl.pallas_call(
        _mix_a_kernel,
        grid=(nb, nt),
        in_specs=[blk(0), blk(2), blk(1), blk(0), blk(2), blk(1), blk(0),
                  _const_spec(bias_tile.shape)],
        out_specs=blk(0),
        out_shape=jax.ShapeDtypeStruct((nb * s, GROUP), BF16),
        compiler_params=_cparams(("parallel", "parallel")),
        name="mix_a_chunk_attn",
    )(q, k, k, k, v, v, v, bias_tile)


def _mix_b_kernel(qk_ref, v_ref, og_ref, gc_ref, gr_ref, cw_ref, cb_ref, on_ref, o_ref,
                  xs_ref, c_ref, n_ref, m_ref):
    c_idx = pl.program_id(1)

    @pl.when(c_idx == 0)
    def _():
        xs_ref[0:8, :] = jnp.zeros((8, 2 * GROUP), F32)
        c_ref[...] = jnp.zeros_like(c_ref)
        n_ref[...] = jnp.zeros_like(n_ref)
        m_ref[...] = jnp.zeros_like(m_ref)

    xs_ref[8:8 + LB, :] = qk_ref[...]
    acc = jnp.broadcast_to(cb_ref[...], (LB, 2 * GROUP))
    for j in range(CONV_WIDTH):
        start = 8 - (CONV_WIDTH - 1) + j
        acc = acc + xs_ref[start:start + LB, :] * cw_ref[j:j + 1, :]
    xs_ref[0:8, :] = xs_ref[LB:LB + 8, :]
    qk = acc * jax.nn.sigmoid(acc)
    q = qk[:, :GROUP]
    k = qk[:, GROUP:] * (HEAD_DIM ** -0.5)
    qb = q.astype(BF16)
    kb = k.astype(BF16)
    v = v_ref[...]

    r = lax.broadcasted_iota(jnp.int32, (LB, LB), 0)
    c = lax.broadcasted_iota(jnp.int32, (LB, LB), 1)
    causal = c <= r
    tri = jnp.where(causal, 1.0, 0.0).astype(BF16)
    tri_t = jnp.where(r <= c, 1.0, 0.0).astype(BF16)
    gcol = gc_ref[...]
    grow = gr_ref[...]
    bcum_col = _split_dot(gcol, tri, 3, x_is_lhs=False)
    bcum_row = _split_dot(grow, tri_t, 3)

    c_state = c_ref[...]
    n_state = n_ref[...]
    q_c = jnp.dot(qb, c_state.astype(BF16), preferred_element_type=F32)
    qn = qb.astype(F32) * n_state.astype(BF16).astype(F32)
    q_n = _split_dot(qn, _group_ones(HEAD_DIM), 2)

    lane = lax.broadcasted_iota(jnp.int32, (LB, GROUP), 1) // HEAD_DIM
    num = jnp.zeros((LB, GROUP), F32)
    den = jnp.zeros((LB, GROUP), F32)
    floor = jnp.zeros((LB, GROUP), F32)
    wg_full = jnp.zeros((LB, GROUP), F32)
    a_full = jnp.zeros((1, GROUP), F32)
    lane1 = lax.broadcasted_iota(jnp.int32, (1, GROUP), 1) // HEAD_DIM
    a_list = []
    for h in range(N_HEADS):
        i_col = gcol[:, h:h + 1]
        i_row = grow[h:h + 1, :]
        b_col = bcum_col[:, N_HEADS + h:N_HEADS + h + 1]
        b_row = bcum_row[N_HEADS + h:N_HEADS + h + 1, :]
        m_prev = m_ref[h:h + 1, 0:1]
        dmat = jnp.where(causal, b_col - b_row + i_row, NEG)
        inter = b_col + m_prev
        m_t = jnp.maximum(inter, jnp.max(dmat, axis=-1, keepdims=True))
        w_intra = jnp.exp(dmat - m_t)
        s_inter = jnp.exp(inter - m_t)
        qh = jnp.where(_head_mask(qb.shape, h), qb, jnp.zeros_like(qb))
        sc = lax.dot_general(qh, kb, (((1,), (1,)), ((), ())), preferred_element_type=F32) * w_intra
        vh = jnp.where(_head_mask(v.shape, h), v, jnp.zeros_like(v))
        num_h = s_inter * q_c + jnp.dot(sc.astype(BF16), vh, preferred_element_type=F32)
        den_h = s_inter * q_n + jnp.sum(sc, axis=-1, keepdims=True)
        sel = lane == h
        num = jnp.where(sel, num_h, num)
        den = jnp.where(sel, den_h, den)
        floor = jnp.where(sel, jnp.exp(-m_t), floor)
        b_tot = b_col[LB - 1:LB, :]
        g = b_tot - b_col + i_col
        m_new = jnp.maximum(b_tot + m_prev, jnp.max(g, axis=0, keepdims=True))
        a_h = jnp.exp(b_tot + m_prev - m_new)
        a_list.append(a_h)
        a_full = jnp.where(lane1 == h, a_h, a_full)
        wg_full = jnp.where(sel, jnp.exp(g - m_new), wg_full)
        m_ref[h:h + 1, :] = jnp.broadcast_to(m_new, (1, m_ref.shape[1]))

    hb = num / jnp.maximum(jnp.abs(den), floor)
    o_ref[...] = (_group_rms(hb, on_ref[...], HEAD_DIM) * og_ref[...]).astype(BF16)

    kw = k * wg_full
    kv = jnp.dot(kw.T.astype(BF16), v, preferred_element_type=F32)
    r2 = lax.broadcasted_iota(jnp.int32, (GROUP, GROUP), 0) // HEAD_DIM
    c2 = lax.broadcasted_iota(jnp.int32, (GROUP, GROUP), 1) // HEAD_DIM
    a_mat = jnp.zeros((GROUP, GROUP), F32)
    for h in range(N_HEADS):
        a_mat = jnp.where(r2 == h, a_list[h], a_mat)
    c_ref[...] = a_mat * c_state + jnp.where(r2 == c2, kv, 0.0)
    n_ref[...] = a_full * n_state + jnp.sum(kw, axis=0, keepdims=True)


def _mix_b(bqk, bv, bo, gcol, grow, conv_w, conv_b, out_norm, nb, s):
    nc = s // LB
    row = lambda w: pl.BlockSpec((LB, w), lambda b, c: (b * nc + c, 0))
    return pl.pallas_call(
        _mix_b_kernel,
        grid=(nb, nc),
        in_specs=[row(2 * GROUP), row(GROUP), row(GROUP), row(GATE_LANES),
                  pl.BlockSpec((GATE_ROWS, LB), lambda b, c: (0, b * nc + c)),
                  _const_spec((CONV_WIDTH, 2 * GROUP)), _const_spec((1, 2 * GROUP)),
                  _const_spec((1, GROUP))],
        out_specs=row(GROUP),
        out_shape=jax.ShapeDtypeStruct((nb * s, GROUP), BF16),
        scratch_shapes=[pltpu.VMEM((LB + 8, 2 * GROUP), F32),
                        pltpu.VMEM((GROUP, GROUP), F32),
                        pltpu.VMEM((1, GROUP), F32),
                        pltpu.VMEM((8, 128), F32)],
        compiler_params=_cparams(("parallel", "arbitrary")),
        name="mix_b_mlstm",
    )(bqk, bv, bo, gcol, grow, conv_w, conv_b, out_norm)


def _mix_c_kernel(lam_ref, qt_ref, k_ref, vt_ref, on_ref, o_ref, qm_ref, acc_ref, *, lam_init):
    qt = pl.program_id(1)
    q_t = qt_ref[0]
    n_maps = 2 * N_HEADS
    comp_row = lax.broadcasted_iota(jnp.int32, q_t.shape, 0) // DIFF_QK_DIM
    for j in range(n_maps):
        qm_ref[j] = jnp.where(comp_row == j, q_t, jnp.zeros_like(q_t))
    acc_ref[...] = jnp.zeros(acc_ref.shape, F32)

    def tile(kt, m, l, masked):
        start = pl.multiple_of(kt * TK_C, TK_C)
        k = k_ref[pl.ds(start, TK_C), :]
        v_t = vt_ref[kt]
        if masked:
            key_chunk = lax.broadcasted_iota(jnp.int32, (TK_C, TQ_C), 0) // CHUNK
            q_chunk = lax.broadcasted_iota(jnp.int32, (TK_C, TQ_C), 1) // CHUNK
            vis = key_chunk <= q_chunk
        scores = [jnp.dot(k, qm_ref[j], preferred_element_type=F32) for j in range(n_maps)]
        m_out, l_out, alphas, probs = [], [], [], []
        for j in range(n_maps):
            s = jnp.where(vis, scores[j], NEG) if masked else scores[j]
            m_new = jnp.maximum(m[j], jnp.max(s, axis=0, keepdims=True))
            alpha = jnp.exp2(m[j] - m_new)
            p = jnp.exp2(s - m_new)
            l_out.append(alpha * l[j] + jnp.sum(p, axis=0, keepdims=True))
            m_out.append(m_new)
            alphas.append(alpha)
            probs.append(p.astype(BF16))
        for j in range(n_maps):
            h = j // 2
            pv = jnp.dot(v_t[h * HEAD_DIM:(h + 1) * HEAD_DIM, :], probs[j],
                         preferred_element_type=F32)
            acc_ref[j] = alphas[j] * acc_ref[j] + pv
        return tuple(m_out), tuple(l_out)

    def body(kt, carry):
        return tile(kt, carry[0], carry[1], False)

    m0 = tuple(jnp.full((1, TQ_C), NEG, F32) for _ in range(n_maps))
    l0 = tuple(jnp.zeros((1, TQ_C), F32) for _ in range(n_maps))
    m, l = lax.fori_loop(0, qt, body, (m0, l0))
    m, l = tile(qt, m, l, True)

    lam = lam_ref[0]
    heads = []
    for h in range(N_HEADS):
        o_h = acc_ref[2 * h] * (1.0 / l[2 * h]) - lam * (acc_ref[2 * h + 1] * (1.0 / l[2 * h + 1]))
        ms = jnp.mean(o_h * o_h, axis=0, keepdims=True)
        heads.append(o_h * lax.rsqrt(ms + EPS))
    out_t = jnp.concatenate(heads, axis=0) * (on_ref[...] * (1.0 - lam_init))
    o_ref[...] = out_t.T.astype(BF16)


def _mix_c(lam, q_t, k, v_t, out_norm_col, nb, s, lam_init):
    nt = s // TQ_C
    tile_t = pl.BlockSpec((1, GROUP, TQ_C), lambda b, t, lam_ref: (b * nt + t, 0, 0))
    full_k = pl.BlockSpec((s, GROUP), lambda b, t, lam_ref: (b, 0), pipeline_mode=pl.Buffered(1))
    full_vt = pl.BlockSpec((nt, GROUP, TK_C), lambda b, t, lam_ref: (b, 0, 0), pipeline_mode=pl.Buffered(1))
    grid_spec = pltpu.PrefetchScalarGridSpec(
        num_scalar_prefetch=1,
        grid=(nb, nt),
        in_specs=[tile_t, full_k, full_vt,
                  pl.BlockSpec((GROUP, 1), lambda b, t, lam_ref: (0, 0), pipeline_mode=pl.Buffered(1))],
        out_specs=pl.BlockSpec((TQ_C, GROUP), lambda b, t, lam_ref: (b * nt + t, 0)),
        scratch_shapes=[pltpu.VMEM((2 * N_HEADS, GROUP, TQ_C), BF16),
                        pltpu.VMEM((2 * N_HEADS, HEAD_DIM, TQ_C), F32)],
    )
    return pl.pallas_call(
        functools.partial(_mix_c_kernel, lam_init=lam_init),
        grid_spec=grid_spec,
        out_shape=jax.ShapeDtypeStruct((nb * s, GROUP), BF16),
        compiler_params=_cparams(("parallel", "parallel")),
        name="mix_c_diff_attn",
    )(lam, q_t, k, v_t, out_norm_col)


def _mix_d_kernel(q_ref, k_ref, v_ref, o_ref):
    qt = pl.program_id(1)
    q = q_ref[...]
    r = lax.broadcasted_iota(jnp.int32, (TK_D, TK_D), 0)
    c = lax.broadcasted_iota(jnp.int32, (TK_D, TK_D), 1)
    upper = jnp.where(r > c, 1.0, 0.0).astype(BF16)
    before_diag = c < r
    lane = lax.broadcasted_iota(jnp.int32, (TQ_D, GROUP), 1) // HEAD_DIM
    out = jnp.zeros((TQ_D, GROUP), F32)

    for h in range(N_HEADS):
        qh = jnp.where(_head_mask(q.shape, h), q, jnp.zeros_like(q))

        def tile(kt, run, acc, masked, qh=qh, h=h):
            start = pl.multiple_of(kt * TK_D, TK_D)
            k = k_ref[pl.ds(start, TK_D), :]
            v = v_ref[pl.ds(start, TK_D), :]
            z = lax.dot_general(qh, k, (((1,), (1,)), ((), ())), preferred_element_type=F32)
            log_keep = _log_sigmoid(-z)
            log_take = z + log_keep
            if masked:
                log_keep = jnp.where(before_diag, log_keep, 0.0)
            between = run + _split_dot(log_keep, upper, 2)
            a = jnp.exp(log_take + between)
            if masked:
                a = jnp.where(before_diag, a, 0.0)
            vh = jnp.where(_head_mask(v.shape, h), v, jnp.zeros_like(v))
            acc = acc + jnp.dot(a.astype(BF16), vh, preferred_element_type=F32)
            run = run + jnp.sum(log_keep, axis=-1, keepdims=True)
            return run, acc

        run0 = jnp.zeros((TQ_D, 1), F32)
        run, acc = tile(qt, run0, jnp.zeros((TQ_D, GROUP), F32), True)

        def cond(state):
            kt, run, _ = state
            return jnp.logical_and(kt >= 0, jnp.max(run) > SB_SKIP)

        def body(state):
            kt, run, acc = state
            run, acc = tile(kt, run, acc, False)
            return kt - 1, run, acc

        _, _, acc = lax.while_loop(cond, body, (qt - 1, run, acc))
        out = jnp.where(lane == h, acc, out)

    o_ref[...] = out.astype(BF16)


def _mix_d(q, k, v, nb, s):
    nt = s // TQ_D
    full = pl.BlockSpec((s, GROUP), lambda b, t: (b, 0), pipeline_mode=pl.Buffered(1))
    tile = pl.BlockSpec((TQ_D, GROUP), lambda b, t: (b * nt + t, 0))
    return pl.pallas_call(
        _mix_d_kernel,
        grid=(nb, nt),
        in_specs=[tile, full, full],
        out_specs=tile,
        out_shape=jax.ShapeDtypeStruct((nb * s, GROUP), BF16),
        compiler_params=_cparams(("parallel", "parallel")),
        name="mix_d_stick_breaking",
    )(q, k, v)


def _out_proj_kernel(x_ref, ya_ref, yb_ref, yc_ref, yd_ref, w_ref, o_ref):
    acc = x_ref[...]
    for g, y_ref in enumerate((ya_ref, yb_ref, yc_ref, yd_ref)):
        acc = acc + jnp.dot(y_ref[...], w_ref[g * GROUP:(g + 1) * GROUP, :], preferred_element_type=F32)
    o_ref[...] = acc


def _out_proj(x, ya, yb, yc, yd, w_out):
    t = x.shape[0]
    tm = min(TM_FFN, t)
    row = lambda w: pl.BlockSpec((tm, w), lambda i: (i, 0))
    return pl.pallas_call(
        _out_proj_kernel,
        grid=(t // tm,),
        in_specs=[row(D_MODEL), row(GROUP), row(GROUP), row(GROUP), row(GROUP),
                  _const_spec((D_MODEL, D_MODEL))],
        out_specs=row(D_MODEL),
        out_shape=jax.ShapeDtypeStruct((t, D_MODEL), F32),
        compiler_params=_cparams(("parallel",)),
        name="mix_out_proj",
    )(x, ya, yb, yc, yd, w_out)


def _rel_bias_tile(rel_bias):
    nk = TQ_A + LEFT_CHUNKS * CHUNK
    i = np.arange(TQ_A)[:, None]
    j = np.arange(nk)[None, :]
    dc = j // CHUNK - i // CHUNK
    visible = (dc >= 0) & (dc <= LEFT_CHUNKS)
    p = TQ_A + nk
    d = np.arange(p)
    rel = d - (nk - 1) + LEFT_CHUNKS * CHUNK
    idx = np.clip(rel, -REL_CLIP, REL_CLIP) + REL_CLIP
    diag = rel_bias[:, idx].astype(F32)
    nh = rel_bias.shape[0]
    skew = jnp.tile(diag, (1, TQ_A + 1))[:, :TQ_A * (p + 1)].reshape(nh, TQ_A, p + 1)
    toeplitz = skew[:, :, :nk][:, :, ::-1]
    return jnp.where(jnp.asarray(visible)[None], toeplitz, NEG)


def _permute_w_in(w_in):
    sizes = [GROUP, GROUP, GROUP, 2 * GROUP, GROUP, GROUP, N_HEADS, N_HEADS,
             GROUP, GROUP, GROUP, GROUP, GROUP, GROUP]
    offs = np.concatenate([[0], np.cumsum(sizes)])
    seg = [w_in[:, offs[n]:offs[n + 1]] for n in range(len(sizes))]
    main = jnp.concatenate(seg[0:6] + [seg[9]] + seg[11:14], axis=1).astype(BF16)
    c_t = jnp.concatenate([seg[8].T, seg[10].T], axis=0).astype(BF16)
    gates = jnp.concatenate([seg[6], seg[7]], axis=1)
    g_col = jnp.pad(gates, ((0, 0), (0, GATE_LANES - 2 * N_HEADS))).astype(BF16)
    g_row = jnp.pad(gates.T, ((0, GATE_ROWS - 2 * N_HEADS), (0, 0))).astype(BF16)
    return main, c_t, g_col, g_row


def _layer(x, nb, s, lam_init, p):
    x = _ffn(x, p["ffn1_norm"], p["ffn1_wg"].astype(BF16), p["ffn1_wu"].astype(BF16), p["ffn1_wd"].astype(BF16))

    w_main, w_ct, w_gc, w_gr = _permute_w_in(p["w_in"])
    gate_bias = p["b_gate_bias"].reshape(2 * N_HEADS).astype(F32)
    gb_col = jnp.pad(gate_bias, (0, GATE_LANES - 2 * N_HEADS)).reshape(1, GATE_LANES)
    gb_row = jnp.pad(gate_bias, (0, GATE_ROWS - 2 * N_HEADS)).reshape(GATE_ROWS, 1)
    tile4 = lambda g: jnp.tile(g.astype(F32), GROUP // g.shape[0]).reshape(1, GROUP)
    (aq, ak, av, bqk, bv, bo, gcol, grow, cq_t, ck, cv_t, dq, dk, dv) = _proj(
        x, p["mix_norm"], w_main, w_ct, w_gc, w_gr, gb_col, gb_row,
        tile4(p["a_q_norm"]), tile4(p["a_k_norm"]), tile4(p["c_q_norm"]).reshape(GROUP, 1), tile4(p["c_k_norm"]))

    ya = _mix_a(aq, ak, av, _rel_bias_tile(p["a_rel_bias"]), nb, s)
    yb = _mix_b(bqk, bv, bo, gcol, grow, p["b_conv_w"].astype(F32), p["b_conv_b"].astype(F32).reshape(1, -1),
                p["b_out_norm"].astype(F32).reshape(1, GROUP), nb, s)
    lv = p["c_lambda"].astype(F32)
    lam = jnp.exp(jnp.sum(lv[0] * lv[1])) - jnp.exp(jnp.sum(lv[2] * lv[3])) + lam_init
    yc = _mix_c(lam.reshape(1), cq_t, ck, cv_t, tile4(p["c_out_norm"]).reshape(GROUP, 1), nb, s, lam_init)
    yd = _mix_d(dq, dk, dv, nb, s)

    x = _out_proj(x, ya, yb, yc, yd, p["w_out"].astype(BF16))
    return _ffn(x, p["ffn2_norm"], p["ffn2_wg"].astype(BF16), p["ffn2_wu"].astype(BF16), p["ffn2_wd"].astype(BF16))


_PARAM_NAMES = ("ffn1_norm", "ffn1_wg", "ffn1_wu", "ffn1_wd", "mix_norm", "w_in", "a_q_norm", "a_k_norm",
                "a_rel_bias", "b_conv_w", "b_conv_b", "b_gate_bias", "b_out_norm", "c_q_norm", "c_k_norm",
                "c_lambda", "c_out_norm", "w_out", "ffn2_norm", "ffn2_wg", "ffn2_wu", "ffn2_wd")


def kernel(x, ffn1_norm, ffn1_wg, ffn1_wu, ffn1_wd, mix_norm, w_in, a_q_norm, a_k_norm, a_rel_bias,
           b_conv_w, b_conv_b, b_gate_bias, b_out_norm, c_q_norm, c_k_norm, c_lambda, c_out_norm,
           w_out, ffn2_norm, ffn2_wg, ffn2_wu, ffn2_wd):
    params = dict(zip(_PARAM_NAMES, (ffn1_norm, ffn1_wg, ffn1_wu, ffn1_wd, mix_norm, w_in, a_q_norm, a_k_norm,
                                     a_rel_bias, b_conv_w, b_conv_b, b_gate_bias, b_out_norm, c_q_norm,
                                     c_k_norm, c_lambda, c_out_norm, w_out, ffn2_norm, ffn2_wg, ffn2_wu,
                                     ffn2_wd)))
    nb, s, d = x.shape
    depth = ffn1_norm.shape[0]
    h = x.reshape(nb * s, d)
    for l in range(depth):
        lam_init = 0.8 - 0.6 * math.exp(-0.3 * l)
        h = _layer(h, nb, s, lam_init, {k: v[l] for k, v in params.items()})
    return h.reshape(nb, s, d)
```

```python
import functools
import math

import jax
import jax.numpy as jnp
import numpy as np
from jax import lax
from jax.experimental import pallas as pl
from jax.experimental.pallas import tpu as pltpu

F32 = jnp.float32
BF16 = jnp.bfloat16

D_MODEL = 1024
D_FF = 2816
CHUNK = 64
HEAD_DIM = 64
N_HEADS = 4
GROUP = N_HEADS * HEAD_DIM
LEFT_CHUNKS = 8
REL_CLIP = 128
CONV_WIDTH = 4
DIFF_QK_DIM = HEAD_DIM // 2
EPS = 1e-6
NEG = -1e30
LOG2E = 1.4426950408889634

OFF_AQ, OFF_AK, OFF_AV = 0, 256, 512
OFF_BQK, OFF_BV, OFF_BO = 768, 1280, 1536
OFF_CK = 1792
OFF_DQ, OFF_DK, OFF_DV = 2048, 2304, 2560
MAIN_COLS = 2816
GATE_LANES = 128
GATE_ROWS = 16
ONES_ROWS = 16
VROWS = HEAD_DIM + ONES_ROWS

VMEM_LIMIT = 56 * 1024 * 1024

TM_FFN = 512
TM_PROJ = 256
TQ_A = 256
LB = 256
TQ_C = 256
TK_C = 256
TQ_D = 256
TK_D = 256
SB_SKIP = -110.0


def _cparams(sem):
    return pltpu.CompilerParams(dimension_semantics=sem, vmem_limit_bytes=VMEM_LIMIT)


def _const_spec(shape):
    nd = len(shape)
    return pl.BlockSpec(shape, lambda *_: (0,) * nd, pipeline_mode=pl.Buffered(1))


def _split_dot(x, mat, terms, x_is_lhs=True):
    acc = None
    rem = x
    for _ in range(terms):
        part = rem.astype(BF16)
        rem = rem - part.astype(F32)
        d = (jnp.dot(part, mat, preferred_element_type=F32) if x_is_lhs
             else jnp.dot(mat, part, preferred_element_type=F32))
        acc = d if acc is None else acc + d
    return acc


def _group_ones(width):
    r = lax.broadcasted_iota(jnp.int32, (GROUP, GROUP), 0) // width
    c = lax.broadcasted_iota(jnp.int32, (GROUP, GROUP), 1) // width
    return jnp.where(r == c, 1.0, 0.0).astype(BF16)


def _group_rms(x, gain, width):
    ss = _split_dot(x * x, _group_ones(width), 2)
    return x * lax.rsqrt(ss * (1.0 / width) + EPS) * gain


def _log_sigmoid(x):
    return jnp.minimum(x, 0.0) - jnp.log1p(jnp.exp(-jnp.abs(x)))


def _head_mask(shape, h, axis=1, width=HEAD_DIM):
    lane = lax.broadcasted_iota(jnp.int32, shape, axis)
    return (lane // width) == h


def _ffn_kernel(x_ref, g_ref, wg_ref, wu_ref, wd_ref, o_ref):
    x = x_ref[...]
    xn = x * lax.rsqrt(jnp.mean(x * x, axis=-1, keepdims=True) + EPS) * g_ref[...]
    xb = xn.astype(BF16)
    g = jnp.dot(xb, wg_ref[...], preferred_element_type=F32)
    u = jnp.dot(xb, wu_ref[...], preferred_element_type=F32)
    h = (g * jax.nn.sigmoid(g) * u).astype(BF16)
    o_ref[...] = x + 0.5 * jnp.dot(h, wd_ref[...], preferred_element_type=F32)


def _ffn(x, gain, wg, wu, wd):
    t = x.shape[0]
    tm = min(TM_FFN, t)
    return pl.pallas_call(
        _ffn_kernel,
        grid=(t // tm,),
        in_specs=[pl.BlockSpec((tm, D_MODEL), lambda i: (i, 0)),
                  _const_spec((1, D_MODEL)),
                  _const_spec((D_MODEL, D_FF)),
                  _const_spec((D_MODEL, D_FF)),
                  _const_spec((D_FF, D_MODEL))],
        out_specs=pl.BlockSpec((tm, D_MODEL), lambda i: (i, 0)),
        out_shape=jax.ShapeDtypeStruct((t, D_MODEL), F32),
        compiler_params=_cparams(("parallel",)),
        name="ffn_half_step",
    )(x, gain.reshape(1, D_MODEL), wg, wu, wd)


def _proj_kernel(x_ref, g_ref, w_ref, wct_ref, wgc_ref, wgr_ref, gbc_ref, gbr_ref,
                 aqn_ref, akn_ref, cqn_ref, ckn_ref,
                 aq_ref, ak_ref, av_ref, bqk_ref, bv_ref, bo_ref, gc_ref, gr_ref,
                 cq_ref, ck_ref, cv_ref, dq_ref, dk_ref, dv_ref):
    x = x_ref[...]
    hn = x * lax.rsqrt(jnp.mean(x * x, axis=-1, keepdims=True) + EPS) * g_ref[...]
    hb = hn.astype(BF16)

    def cols(off, width=GROUP):
        return jnp.dot(hb, w_ref[:, off:off + width], preferred_element_type=F32)

    aq_ref[...] = (_group_rms(cols(OFF_AQ), aqn_ref[...], HEAD_DIM) * (HEAD_DIM ** -0.5)).astype(BF16)
    ak_ref[...] = _group_rms(cols(OFF_AK), akn_ref[...], HEAD_DIM).astype(BF16)
    av_ref[...] = cols(OFF_AV).astype(BF16)
    bqk_ref[...] = cols(OFF_BQK, 2 * GROUP)
    bv_ref[...] = cols(OFF_BV).astype(BF16)
    bo_ref[...] = jax.nn.sigmoid(cols(OFF_BO))
    gcol = jnp.dot(hb, wgc_ref[...], preferred_element_type=F32) + gbc_ref[...]
    lane = lax.broadcasted_iota(jnp.int32, gcol.shape, 1)
    gc_ref[...] = jnp.where(lane >= N_HEADS, _log_sigmoid(gcol), gcol)
    grow = lax.dot_general(wgr_ref[...], hb, (((1,), (1,)), ((), ())),
                           preferred_element_type=F32) + gbr_ref[...]
    row = lax.broadcasted_iota(jnp.int32, grow.shape, 0)
    gr_ref[...] = jnp.where(row >= N_HEADS, _log_sigmoid(grow), grow)
    ck_ref[...] = _group_rms(cols(OFF_CK), ckn_ref[...], DIFF_QK_DIM).astype(BF16)
    nt_dims = (((1,), (1,)), ((), ()))
    cqt = lax.dot_general(wct_ref[0:GROUP, :], hb, nt_dims, preferred_element_type=F32)
    ss = _split_dot(cqt * cqt, _group_ones(DIFF_QK_DIM), 2, x_is_lhs=False)
    cqt = cqt * lax.rsqrt(ss * (1.0 / DIFF_QK_DIM) + EPS) * cqn_ref[...]
    cq_ref[0] = (cqt * (DIFF_QK_DIM ** -0.5 * LOG2E)).astype(BF16)
    cvt = lax.dot_general(wct_ref[GROUP:2 * GROUP, :], hb, nt_dims, preferred_element_type=F32).astype(BF16)
    for h in range(N_HEADS):
        cv_ref[0, h * VROWS:h * VROWS + HEAD_DIM, :] = cvt[h * HEAD_DIM:(h + 1) * HEAD_DIM, :]
        cv_ref[0, h * VROWS + HEAD_DIM:(h + 1) * VROWS, :] = jnp.ones((ONES_ROWS, cvt.shape[1]), BF16)
    dq_ref[...] = (cols(OFF_DQ) * (HEAD_DIM ** -0.5)).astype(BF16)
    dk_ref[...] = cols(OFF_DK).astype(BF16)
    dv_ref[...] = cols(OFF_DV).astype(BF16)


def _proj(x, gain, w_main, w_ct, w_gc, w_gr, gb_col, gb_row, aqn, akn, cqn_col, ckn):
    t = x.shape[0]
    tm = TM_PROJ
    row_spec = lambda w: pl.BlockSpec((tm, w), lambda i: (i, 0))
    bf = lambda w: jax.ShapeDtypeStruct((t, w), BF16)
    f32 = lambda w: jax.ShapeDtypeStruct((t, w), F32)
    tr_shape = jax.ShapeDtypeStruct((t // tm, GROUP, tm), BF16)
    tr_spec = pl.BlockSpec((1, GROUP, tm), lambda i: (i, 0, 0))
    vt_shape = jax.ShapeDtypeStruct((t // tm, N_HEADS * VROWS, tm), BF16)
    vt_spec = pl.BlockSpec((1, N_HEADS * VROWS, tm), lambda i: (i, 0, 0))
    out_shape = [bf(GROUP), bf(GROUP), bf(GROUP),
                 f32(2 * GROUP), bf(GROUP), f32(GROUP),
                 f32(GATE_LANES), jax.ShapeDtypeStruct((GATE_ROWS, t), F32),
                 tr_shape, bf(GROUP), vt_shape,
                 bf(GROUP), bf(GROUP), bf(GROUP)]
    out_specs = ([row_spec(GROUP)] * 3 + [row_spec(2 * GROUP), row_spec(GROUP), row_spec(GROUP),
                                          row_spec(GATE_LANES), pl.BlockSpec((GATE_ROWS, tm), lambda i: (0, i))]
                 + [tr_spec, row_spec(GROUP), vt_spec] + [row_spec(GROUP)] * 3)
    return pl.pallas_call(
        _proj_kernel,
        grid=(t // tm,),
        in_specs=[row_spec(D_MODEL), _const_spec((1, D_MODEL)),
                  _const_spec((D_MODEL, MAIN_COLS)), _const_spec((2 * GROUP, D_MODEL)),
                  _const_spec((D_MODEL, GATE_LANES)),
                  _const_spec((GATE_ROWS, D_MODEL)), _const_spec((1, GATE_LANES)),
                  _const_spec((GATE_ROWS, 1)),
                  _const_spec((1, GROUP)), _const_spec((1, GROUP)),
                  _const_spec((GROUP, 1)), _const_spec((1, GROUP))],
        out_specs=out_specs,
        out_shape=out_shape,
        compiler_params=_cparams(("parallel",)),
        name="mix_in_proj",
    )(x, gain.reshape(1, D_MODEL), w_main, w_ct, w_gc, w_gr, gb_col, gb_row, aqn, akn, cqn_col, ckn)


def _mix_a_kernel(q_ref, k0_ref, k1_ref, k2_ref, v0_ref, v1_ref, v2_ref, bias_ref, o_ref):
    t = pl.program_id(1)
    q = q_ref[...]
    k = jnp.concatenate([k0_ref[...], k1_ref[...], k2_ref[...]], axis=0)
    v = jnp.concatenate([v0_ref[...], v1_ref[...], v2_ref[...]], axis=0)
    nk = k.shape[0]
    key_pos = t * TQ_A - LEFT_CHUNKS * CHUNK + lax.broadcasted_iota(jnp.int32, (TQ_A, nk), 1)
    valid = key_pos >= 0
    out = jnp.zeros((TQ_A, GROUP), F32)
    for h in range(N_HEADS):
        qh = jnp.where(_head_mask(q.shape, h), q, jnp.zeros_like(q))
        s = lax.dot_general(qh, k, (((1,), (1,)), ((), ())), preferred_element_type=F32)
        s = jnp.where(valid, s + bias_ref[h], NEG)
        m = jnp.max(s, axis=-1, keepdims=True)
        e = jnp.exp(s - m)
        p = (e / jnp.sum(e, axis=-1, keepdims=True)).astype(BF16)
        vh = jnp.where(_head_mask(v.shape, h), v, jnp.zeros_like(v))
        out = out + jnp.dot(p, vh, preferred_element_type=F32)
    o_ref[...] = out.astype(BF16)


def _mix_a(q, k, v, bias_tile, nb, s):
    nt = s // TQ_A
    blk = lambda back: pl.BlockSpec((TQ_A, GROUP), lambda b, t: (b * nt + jnp.maximum(t - back, 0), 0))
    return pl.pallas_call(
        _mix_a_kernel,
        grid=(nb, nt),
        in_specs=[blk(0), blk(2), blk(1), blk(0), blk(2), blk(1), blk(0),
                  _const_spec(bias_tile.shape)],
        out_specs=blk(0),
        out_shape=jax.ShapeDtypeStruct((nb * s, GROUP), BF16),
        compiler_params=_cparams(("parallel", "parallel")),
        name="mix_a_chunk_attn",
    )(q, k, k, k, v, v, v, bias_tile)


def _mix_b_kernel(qk_ref, v_ref, og_ref, gc_ref, gr_ref, cw_ref, cb_ref, on_ref, o_ref,
                  xs_ref, c_ref, n_ref, m_ref):
    c_idx = pl.program_id(1)

    @pl.when(c_idx == 0)
    def _():
        xs_ref[0:8, :] = jnp.zeros((8, 2 * GROUP), F32)
        c_ref[...] = jnp.zeros_like(c_ref)
        n_ref[...] = jnp.zeros_like(n_ref)
        m_ref[...] = jnp.zeros_like(m_ref)

    xs_ref[8:8 + LB, :] = qk_ref[...]
    acc = jnp.broadcast_to(cb_ref[...], (LB, 2 * GROUP))
    for j in range(CONV_WIDTH):
        start = 8 - (CONV_WIDTH - 1) + j
        acc = acc + xs_ref[start:start + LB, :] * cw_ref[j:j + 1, :]
    xs_ref[0:8, :] = xs_ref[LB:LB + 8, :]
    qk = acc * jax.nn.sigmoid(acc)
    q = qk[:, :GROUP]
    k = qk[:, GROUP:] * (HEAD_DIM ** -0.5)
    qb = q.astype(BF16)
    kb = k.astype(BF16)
    v = v_ref[...]

    r = lax.broadcasted_iota(jnp.int32, (LB, LB), 0)
    c = lax.broadcasted_iota(jnp.int32, (LB, LB), 1)
    causal = c <= r
    tri = jnp.where(causal, 1.0, 0.0).astype(BF16)
    tri_t = jnp.where(r <= c, 1.0, 0.0).astype(BF16)
    gcol = gc_ref[...]
    grow = gr_ref[...]
    bcum_col = _split_dot(gcol, tri, 3, x_is_lhs=False)
    bcum_row = _split_dot(grow, tri_t, 3)

    c_state = c_ref[...]
    n_state = n_ref[...]
    q_c = jnp.dot(qb, c_state.astype(BF16), preferred_element_type=F32)
    qn = qb.astype(F32) * n_state.astype(BF16).astype(F32)
    q_n = _split_dot(qn, _group_ones(HEAD_DIM), 2)

    lane = lax.broadcasted_iota(jnp.int32, (LB, GROUP), 1) // HEAD_DIM
    num = jnp.zeros((LB, GROUP), F32)
    den = jnp.zeros((LB, GROUP), F32)
    floor = jnp.zeros((LB, GROUP), F32)
    wg_full = jnp.zeros((LB, GROUP), F32)
    a_full = jnp.zeros((1, GROUP), F32)
    lane1 = lax.broadcasted_iota(jnp.int32, (1, GROUP), 1) // HEAD_DIM
    a_list = []
    for h in range(N_HEADS):
        i_col = gcol[:, h:h + 1]
        i_row = grow[h:h + 1, :]
        b_col = bcum_col[:, N_HEADS + h:N_HEADS + h + 1]
        b_row = bcum_row[N_HEADS + h:N_HEADS + h + 1, :]
        m_prev = m_ref[h:h + 1, 0:1]
        dmat = jnp.where(causal, b_col - b_row + i_row, NEG)
        inter = b_col + m_prev
        m_t = jnp.maximum(inter, jnp.max(dmat, axis=-1, keepdims=True))
        w_intra = jnp.exp(dmat - m_t)
        s_inter = jnp.exp(inter - m_t)
        qh = jnp.where(_head_mask(qb.shape, h), qb, jnp.zeros_like(qb))
        sc = lax.dot_general(qh, kb, (((1,), (1,)), ((), ())), preferred_element_type=F32) * w_intra
        vh = jnp.where(_head_mask(v.shape, h), v, jnp.zeros_like(v))
        num_h = s_inter * q_c + jnp.dot(sc.astype(BF16), vh, preferred_element_type=F32)
        den_h = s_inter * q_n + jnp.sum(sc, axis=-1, keepdims=True)
        sel = lane == h
        num = jnp.where(sel, num_h, num)
        den = jnp.where(sel, den_h, den)
        floor = jnp.where(sel, jnp.exp(-m_t), floor)
        b_tot = b_col[LB - 1:LB, :]
        g = b_tot - b_col + i_col
        m_new = jnp.maximum(b_tot + m_prev, jnp.max(g, axis=0, keepdims=True))
        a_h = jnp.exp(b_tot + m_prev - m_new)
        a_list.append(a_h)
        a_full = jnp.where(lane1 == h, a_h, a_full)
        wg_full = jnp.where(sel, jnp.exp(g - m_new), wg_full)
        m_ref[h:h + 1, :] = jnp.broadcast_to(m_new, (1, m_ref.shape[1]))

    hb = num / jnp.maximum(jnp.abs(den), floor)
    o_ref[...] = (_group_rms(hb, on_ref[...], HEAD_DIM) * og_ref[...]).astype(BF16)

    kw = k * wg_full
    kv = jnp.dot(kw.T.astype(BF16), v, preferred_element_type=F32)
    r2 = lax.broadcasted_iota(jnp.int32, (GROUP, GROUP), 0) // HEAD_DIM
    c2 = lax.broadcasted_iota(jnp.int32, (GROUP, GROUP), 1) // HEAD_DIM
    a_mat = jnp.zeros((GROUP, GROUP), F32)
    for h in range(N_HEADS):
        a_mat = jnp.where(r2 == h, a_list[h], a_mat)
    c_ref[...] = a_mat * c_state + jnp.where(r2 == c2, kv, 0.0)
    n_ref[...] = a_full * n_state + jnp.sum(kw, axis=0, keepdims=True)


def _mix_b(bqk, bv, bo, gcol, grow, conv_w, conv_b, out_norm, nb, s):
    nc = s // LB
    row = lambda w: pl.BlockSpec((LB, w), lambda b, c: (b * nc + c, 0))
    return pl.pallas_call(
        _mix_b_kernel,
        grid=(nb, nc),
        in_specs=[row(2 * GROUP), row(GROUP), row(GROUP), row(GATE_LANES),
                  pl.BlockSpec((GATE_ROWS, LB), lambda b, c: (0, b * nc + c)),
                  _const_spec((CONV_WIDTH, 2 * GROUP)), _const_spec((1, 2 * GROUP)),
                  _const_spec((1, GROUP))],
        out_specs=row(GROUP),
        out_shape=jax.ShapeDtypeStruct((nb * s, GROUP), BF16),
        scratch_shapes=[pltpu.VMEM((LB + 8, 2 * GROUP), F32),
                        pltpu.VMEM((GROUP, GROUP), F32),
                        pltpu.VMEM((1, GROUP), F32),
                        pltpu.VMEM((8, 128), F32)],
        compiler_params=_cparams(("parallel", "arbitrary")),
        name="mix_b_mlstm",
    )(bqk, bv, bo, gcol, grow, conv_w, conv_b, out_norm)


def _mix_c_kernel(lam_ref, qt_ref, k_ref, vt_ref, on_ref, o_ref, qm_ref, p_ref, acc_ref, m_ref, al_ref, *,
                  lam_init):
    qt = pl.program_id(1)
    q_t = qt_ref[0]
    n_maps = 2 * N_HEADS
    comp_row = lax.broadcasted_iota(jnp.int32, q_t.shape, 0) // DIFF_QK_DIM
    for j in range(n_maps):
        qm_ref[j] = jnp.where(comp_row == j, q_t, jnp.zeros_like(q_t))
    acc_ref[...] = jnp.zeros(acc_ref.shape, F32)

    def scores(kt):
        start = pl.multiple_of(kt * TK_C, TK_C)
        k = k_ref[pl.ds(start, TK_C), :]
        return [jnp.dot(k, qm_ref[j], preferred_element_type=F32) for j in range(n_maps)]

    def softmax(sc, slot, masked=False):
        if masked:
            key_chunk = lax.broadcasted_iota(jnp.int32, (TK_C, TQ_C), 0) // CHUNK
            q_chunk = lax.broadcasted_iota(jnp.int32, (TK_C, TQ_C), 1) // CHUNK
            vis = key_chunk <= q_chunk
        for j in range(n_maps):
            s = jnp.where(vis, sc[j], NEG) if masked else sc[j]
            m_old = m_ref[j]
            m_new = jnp.maximum(m_old, jnp.max(s, axis=0, keepdims=True))
            al_ref[j] = jnp.exp2(m_old - m_new)
            m_ref[j] = m_new
            p_ref[slot, j] = jnp.exp2(s - m_new).astype(BF16)

    def pv_update(kt, slot):
        v_ext = vt_ref[kt]
        for j in range(n_maps):
            h = j // 2
            pv = jnp.dot(v_ext[h * VROWS:(h + 1) * VROWS, :], p_ref[slot, j], preferred_element_type=F32)
            acc_ref[j] = al_ref[j] * acc_ref[j] + pv

    def step(kt, prev, slot):
        sc = scores(kt)
        pv_update(prev, slot)
        softmax(sc, 1 - slot)

    m_ref[...] = jnp.full(m_ref.shape, NEG, F32)
    softmax(scores(qt), 0, masked=True)
    n_pairs = qt // 2

    def body(ip, carry):
        a = 2 * ip
        step(a, jnp.where(ip == 0, qt, a - 1), 0)
        step(a + 1, a, 1)
        return carry

    lax.fori_loop(0, n_pairs, body, 0)
    last = jnp.where(n_pairs > 0, 2 * n_pairs - 1, qt)
    odd = qt - 2 * n_pairs

    @pl.when(odd == 1)
    def _():
        step(qt - 1, last, 0)
        pv_update(qt - 1, 1)

    @pl.when(odd == 0)
    def _():
        pv_update(last, 0)

    lam = lam_ref[0]
    heads = []
    for h in range(N_HEADS):
        a0 = acc_ref[2 * h]
        a1 = acc_ref[2 * h + 1]
        o_h = (a0[:HEAD_DIM] * (1.0 / a0[HEAD_DIM:HEAD_DIM + 1])
               - lam * (a1[:HEAD_DIM] * (1.0 / a1[HEAD_DIM:HEAD_DIM + 1])))
        ms = jnp.mean(o_h * o_h, axis=0, keepdims=True)
        heads.append(o_h * lax.rsqrt(ms + EPS))
    out_t = jnp.concatenate(heads, axis=0) * (on_ref[...] * (1.0 - lam_init))
    o_ref[...] = out_t.T.astype(BF16)


def _mix_c(lam, q_t, k, v_t, out_norm_col, nb, s, lam_init):
    nt = s // TQ_C
    tile_t = pl.BlockSpec((1, GROUP, TQ_C), lambda b, t, lam_ref: (b * nt + t, 0, 0))
    full_k = pl.BlockSpec((s, GROUP), lambda b, t, lam_ref: (b, 0), pipeline_mode=pl.Buffered(1))
    full_vt = pl.BlockSpec((nt, N_HEADS * VROWS, TK_C), lambda b, t, lam_ref: (b, 0, 0),
                           pipeline_mode=pl.Buffered(1))
    grid_spec = pltpu.PrefetchScalarGridSpec(
        num_scalar_prefetch=1,
        grid=(nb, nt),
        in_specs=[tile_t, full_k, full_vt,
                  pl.BlockSpec((GROUP, 1), lambda b, t, lam_ref: (0, 0), pipeline_mode=pl.Buffered(1))],
        out_specs=pl.BlockSpec((TQ_C, GROUP), lambda b, t, lam_ref: (b * nt + t, 0)),
        scratch_shapes=[pltpu.VMEM((2 * N_HEADS, GROUP, TQ_C), BF16),
                        pltpu.VMEM((2, 2 * N_HEADS, TK_C, TQ_C), BF16),
                        pltpu.VMEM((2 * N_HEADS, VROWS, TQ_C), F32),
                        pltpu.VMEM((2 * N_HEADS, 1, TQ_C), F32),
                        pltpu.VMEM((2 * N_HEADS, 1, TQ_C), F32)],
    )
    return pl.pallas_call(
        functools.partial(_mix_c_kernel, lam_init=lam_init),
        grid_spec=grid_spec,
        out_shape=jax.ShapeDtypeStruct((nb * s, GROUP), BF16),
        compiler_params=_cparams(("parallel", "parallel")),
        name="mix_c_diff_attn",
    )(lam, q_t, k, v_t, out_norm_col)


def _mix_d_kernel(q_ref, k_ref, v_ref, o_ref):
    qt = pl.program_id(1)
    q = q_ref[...]
    r = lax.broadcasted_iota(jnp.int32, (TK_D, TK_D), 0)
    c = lax.broadcasted_iota(jnp.int32, (TK_D, TK_D), 1)
    upper = jnp.where(r > c, 1.0, 0.0).astype(BF16)
    before_diag = c < r
    lane = lax.broadcasted_iota(jnp.int32, (TQ_D, GROUP), 1) // HEAD_DIM
    out = jnp.zeros((TQ_D, GROUP), F32)

    for h in range(N_HEADS):
        qh = jnp.where(_head_mask(q.shape, h), q, jnp.zeros_like(q))

        def tile(kt, run, acc, masked, qh=qh, h=h):
            start = pl.multiple_of(kt * TK_D, TK_D)
            k = k_ref[pl.ds(start, TK_D), :]
            v = v_ref[pl.ds(start, TK_D), :]
            z = lax.dot_general(qh, k, (((1,), (1,)), ((), ())), preferred_element_type=F32)
            log_keep = _log_sigmoid(-z)
            log_take = z + log_keep
            if masked:
                log_keep = jnp.where(before_diag, log_keep, 0.0)
            between = run + _split_dot(log_keep, upper, 2)
            a = jnp.exp(log_take + between)
            if masked:
                a = jnp.where(before_diag, a, 0.0)
            vh = jnp.where(_head_mask(v.shape, h), v, jnp.zeros_like(v))
            acc = acc + jnp.dot(a.astype(BF16), vh, preferred_element_type=F32)
            run = run + jnp.sum(log_keep, axis=-1, keepdims=True)
            return run, acc

        run0 = jnp.zeros((TQ_D, 1), F32)
        run, acc = tile(qt, run0, jnp.zeros((TQ_D, GROUP), F32), True)

        def cond(state):
            kt, run, _ = state
            return jnp.logical_and(kt >= 0, jnp.max(run) > SB_SKIP)

        def body(state):
            kt, run, acc = state
            run, acc = tile(kt, run, acc, False)
            return kt - 1, run, acc

        _, _, acc = lax.while_loop(cond, body, (qt - 1, run, acc))
        out = jnp.where(lane == h, acc, out)

    o_ref[...] = out.astype(BF16)


def _mix_d(q, k, v, nb, s):
    nt = s // TQ_D
    full = pl.BlockSpec((s, GROUP), lambda b, t: (b, 0), pipeline_mode=pl.Buffered(1))
    tile = pl.BlockSpec((TQ_D, GROUP), lambda b, t: (b * nt + t, 0))
    return pl.pallas_call(
        _mix_d_kernel,
        grid=(nb, nt),
        in_specs=[tile, full, full],
        out_specs=tile,
        out_shape=jax.ShapeDtypeStruct((nb * s, GROUP), BF16),
        compiler_params=_cparams(("parallel", "parallel")),
        name="mix_d_stick_breaking",
    )(q, k, v)


def _out_proj_kernel(x_ref, ya_ref, yb_ref, yc_ref, yd_ref, w_ref, o_ref):
    acc = x_ref[...]
    for g, y_ref in enumerate((ya_ref, yb_ref, yc_ref, yd_ref)):
        acc = acc + jnp.dot(y_ref[...], w_ref[g * GROUP:(g + 1) * GROUP, :], preferred_element_type=F32)
    o_ref[...] = acc


def _out_proj(x, ya, yb, yc, yd, w_out):
    t = x.shape[0]
    tm = min(TM_FFN, t)
    row = lambda w: pl.BlockSpec((tm, w), lambda i: (i, 0))
    return pl.pallas_call(
        _out_proj_kernel,
        grid=(t // tm,),
        in_specs=[row(D_MODEL), row(GROUP), row(GROUP), row(GROUP), row(GROUP),
                  _const_spec((D_MODEL, D_MODEL))],
        out_specs=row(D_MODEL),
        out_shape=jax.ShapeDtypeStruct((t, D_MODEL), F32),
        compiler_params=_cparams(("parallel",)),
        name="mix_out_proj",
    )(x, ya, yb, yc, yd, w_out)


def _rel_bias_tile(rel_bias):
    nk = TQ_A + LEFT_CHUNKS * CHUNK
    i = np.arange(TQ_A)[:, None]
    j = np.arange(nk)[None, :]
    dc = j // CHUNK - i // CHUNK
    visible = (dc >= 0) & (dc <= LEFT_CHUNKS)
    p = TQ_A + nk
    d = np.arange(p)
    rel = d - (nk - 1) + LEFT_CHUNKS * CHUNK
    idx = np.clip(rel, -REL_CLIP, REL_CLIP) + REL_CLIP
    diag = rel_bias[:, idx].astype(F32)
    nh = rel_bias.shape[0]
    skew = jnp.tile(diag, (1, TQ_A + 1))[:, :TQ_A * (p + 1)].reshape(nh, TQ_A, p + 1)
    toeplitz = skew[:, :, :nk][:, :, ::-1]
    return jnp.where(jnp.asarray(visible)[None], toeplitz, NEG)


def _permute_w_in(w_in):
    sizes = [GROUP, GROUP, GROUP, 2 * GROUP, GROUP, GROUP, N_HEADS, N_HEADS,
             GROUP, GROUP, GROUP, GROUP, GROUP, GROUP]
    offs = np.concatenate([[0], np.cumsum(sizes)])
    seg = [w_in[:, offs[n]:offs[n + 1]] for n in range(len(sizes))]
    main = jnp.concatenate(seg[0:6] + [seg[9]] + seg[11:14], axis=1).astype(BF16)
    c_t = jnp.concatenate([seg[8].T, seg[10].T], axis=0).astype(BF16)
    gates = jnp.concatenate([seg[6], seg[7]], axis=1)
    g_col = jnp.pad(gates, ((0, 0), (0, GATE_LANES - 2 * N_HEADS))).astype(BF16)
    g_row = jnp.pad(gates.T, ((0, GATE_ROWS - 2 * N_HEADS), (0, 0))).astype(BF16)
    return main, c_t, g_col, g_row


def _layer(x, nb, s, lam_init, p):
    x = _ffn(x, p["ffn1_norm"], p["ffn1_wg"].astype(BF16), p["ffn1_wu"].astype(BF16), p["ffn1_wd"].astype(BF16))

    w_main, w_ct, w_gc, w_gr = _permute_w_in(p["w_in"])
    gate_bias = p["b_gate_bias"].reshape(2 * N_HEADS).astype(F32)
    gb_col = jnp.pad(gate_bias, (0, GATE_LANES - 2 * N_HEADS)).reshape(1, GATE_LANES)
    gb_row = jnp.pad(gate_bias, (0, GATE_ROWS - 2 * N_HEADS)).reshape(GATE_ROWS, 1)
    tile4 = lambda g: jnp.tile(g.astype(F32), GROUP // g.shape[0]).reshape(1, GROUP)
    (aq, ak, av, bqk, bv, bo, gcol, grow, cq_t, ck, cv_t, dq, dk, dv) = _proj(
        x, p["mix_norm"], w_main, w_ct, w_gc, w_gr, gb_col, gb_row,
        tile4(p["a_q_norm"]), tile4(p["a_k_norm"]), tile4(p["c_q_norm"]).reshape(GROUP, 1), tile4(p["c_k_norm"]))

    ya = _mix_a(aq, ak, av, _rel_bias_tile(p["a_rel_bias"]), nb, s)
    yb = _mix_b(bqk, bv, bo, gcol, grow, p["b_conv_w"].astype(F32), p["b_conv_b"].astype(F32).reshape(1, -1),
                p["b_out_norm"].astype(F32).reshape(1, GROUP), nb, s)
    lv = p["c_lambda"].astype(F32)
    lam = jnp.exp(jnp.sum(lv[0] * lv[1])) - jnp.exp(jnp.sum(lv[2] * lv[3])) + lam_init
    yc = _mix_c(lam.reshape(1), cq_t, ck, cv_t, tile4(p["c_out_norm"]).reshape(GROUP, 1), nb, s, lam_init)
    yd = _mix_d(dq, dk, dv, nb, s)

    x = _out_proj(x, ya, yb, yc, yd, p["w_out"].astype(BF16))
    return _ffn(x, p["ffn2_norm"], p["ffn2_wg"].astype(BF16), p["ffn2_wu"].astype(BF16), p["ffn2_wd"].astype(BF16))


_PARAM_NAMES = ("ffn1_norm", "ffn1_wg", "ffn1_wu", "ffn1_wd", "mix_norm", "w_in", "a_q_norm", "a_k_norm",
                "a_rel_bias", "b_conv_w", "b_conv_b", "b_gate_bias", "b_out_norm", "c_q_norm", "c_k_norm",
                "c_lambda", "c_out_norm", "w_out", "ffn2_norm", "ffn2_wg", "ffn2_wu", "ffn2_wd")


def kernel(x, ffn1_norm, ffn1_wg, ffn1_wu, ffn1_wd, mix_norm, w_in, a_q_norm, a_k_norm, a_rel_bias,
           b_conv_w, b_conv_b, b_gate_bias, b_out_norm, c_q_norm, c_k_norm, c_lambda, c_out_norm,
           w_out, ffn2_norm, ffn2_wg, ffn2_wu, ffn2_wd):
    params = dict(zip(_PARAM_NAMES, (ffn1_norm, ffn1_wg, ffn1_wu, ffn1_wd, mix_norm, w_in, a_q_norm, a_k_norm,
                                     a_rel_bias, b_conv_w, b_conv_b, b_gate_bias, b_out_norm, c_q_norm,
                                     c_k_norm, c_lambda, c_out_norm, w_out, ffn2_norm, ffn2_wg, ffn2_wu,
                                     ffn2_wd)))
    nb, s, d = x.shape
    depth = ffn1_norm.shape[0]
    h = x.reshape(nb * s, d)
    for l in range(depth):
        lam_init = 0.8 - 0.6 * math.exp(-0.3 * l)
        h = _layer(h, nb, s, lam_init, {k: v[l] for k, v in params.items()})
    return h.reshape(nb, s, d)
```

```python
import functools
import math

import jax
import jax.numpy as jnp
import numpy as np
from jax import lax
from jax.experimental import pallas as pl
from jax.experimental.pallas import tpu as pltpu

F32 = jnp.float32
BF16 = jnp.bfloat16

D_MODEL = 1024
D_FF = 2816
CHUNK = 64
HEAD_DIM = 64
N_HEADS = 4
GROUP = N_HEADS * HEAD_DIM
LEFT_CHUNKS = 8
REL_CLIP = 128
CONV_WIDTH = 4
DIFF_QK_DIM = HEAD_DIM // 2
EPS = 1e-6
NEG = -1e30
LOG2E = 1.4426950408889634

OFF_AQ, OFF_AK, OFF_AV = 0, 256, 512
OFF_BQK, OFF_BV, OFF_BO = 768, 1280, 1536
OFF_CK = 1792
OFF_DK = 2048
MAIN_COLS = 2304
GATE_LANES = 128
GATE_ROWS = 16
ONES_ROWS = 16
VROWS = HEAD_DIM + ONES_ROWS

VMEM_LIMIT = 56 * 1024 * 1024

TM_FFN = 512
TM_PROJ = 256
TQ_A = 256
LB = 256
TQ_C = 256
TK_C = 256
TQ_D = 256
TK_D = 256
SB_SKIP_LOG2 = -160.0


def _cparams(sem):
    return pltpu.CompilerParams(dimension_semantics=sem, vmem_limit_bytes=VMEM_LIMIT)


def _const_spec(shape):
    nd = len(shape)
    return pl.BlockSpec(shape, lambda *_: (0,) * nd, pipeline_mode=pl.Buffered(1))


def _split_dot(x, mat, terms, x_is_lhs=True):
    acc = None
    rem = x
    for term in range(terms):
        part = rem.astype(BF16)
        if term + 1 < terms:
            rem = rem - part.astype(F32)
        d = (jnp.dot(part, mat, preferred_element_type=F32) if x_is_lhs
             else jnp.dot(mat, part, preferred_element_type=F32))
        acc = d if acc is None else acc + d
    return acc


def _group_ones(width):
    r = lax.broadcasted_iota(jnp.int32, (GROUP, GROUP), 0) // width
    c = lax.broadcasted_iota(jnp.int32, (GROUP, GROUP), 1) // width
    return jnp.where(r == c, 1.0, 0.0).astype(BF16)


def _group_rms(x, gain, width):
    ss = _split_dot(x * x, _group_ones(width), 2)
    return x * lax.rsqrt(ss * (1.0 / width) + EPS) * gain


def _log_sigmoid(x):
    return jnp.minimum(x, 0.0) - jnp.log1p(jnp.exp(-jnp.abs(x)))


def _head_mask(shape, h, axis=1, width=HEAD_DIM):
    lane = lax.broadcasted_iota(jnp.int32, shape, axis)
    return (lane // width) == h


def _ffn_kernel(x_ref, g_ref, wg_ref, wu_ref, wd_ref, o_ref):
    x = x_ref[...]
    xn = x * lax.rsqrt(jnp.mean(x * x, axis=-1, keepdims=True) + EPS) * g_ref[...]
    xb = xn.astype(BF16)
    g = jnp.dot(xb, wg_ref[...], preferred_element_type=F32)
    u = jnp.dot(xb, wu_ref[...], preferred_element_type=F32)
    h = (g * jax.nn.sigmoid(g) * u).astype(BF16)
    o_ref[...] = x + 0.5 * jnp.dot(h, wd_ref[...], preferred_element_type=F32)


def _ffn(x, gain, wg, wu, wd):
    t = x.shape[0]
    tm = min(TM_FFN, t)
    return pl.pallas_call(
        _ffn_kernel,
        grid=(t // tm,),
        in_specs=[pl.BlockSpec((tm, D_MODEL), lambda i: (i, 0)),
                  _const_spec((1, D_MODEL)),
                  _const_spec((D_MODEL, D_FF)),
                  _const_spec((D_MODEL, D_FF)),
                  _const_spec((D_FF, D_MODEL))],
        out_specs=pl.BlockSpec((tm, D_MODEL), lambda i: (i, 0)),
        out_shape=jax.ShapeDtypeStruct((t, D_MODEL), F32),
        compiler_params=_cparams(("parallel",)),
        name="ffn_half_step",
    )(x, gain.reshape(1, D_MODEL), wg, wu, wd)


def _proj_kernel(x_ref, g_ref, w_ref, wct_ref, wgc_ref, wgr_ref, gbc_ref, gbr_ref,
                 aqn_ref, akn_ref, cqn_ref, ckn_ref,
                 aq_ref, ak_ref, av_ref, bqk_ref, bv_ref, bo_ref, gc_ref, gr_ref,
                 cq_ref, ck_ref, cv_ref, dq_ref, dk_ref, dv_ref):
    x = x_ref[...]
    hn = x * lax.rsqrt(jnp.mean(x * x, axis=-1, keepdims=True) + EPS) * g_ref[...]
    hb = hn.astype(BF16)

    def cols(off, width=GROUP):
        return jnp.dot(hb, w_ref[:, off:off + width], preferred_element_type=F32)

    aq_ref[...] = (_group_rms(cols(OFF_AQ), aqn_ref[...], HEAD_DIM) * (HEAD_DIM ** -0.5)).astype(BF16)
    ak_ref[...] = _group_rms(cols(OFF_AK), akn_ref[...], HEAD_DIM).astype(BF16)
    av_ref[...] = cols(OFF_AV).astype(BF16)
    bqk_ref[...] = cols(OFF_BQK, 2 * GROUP)
    bv_ref[...] = cols(OFF_BV).astype(BF16)
    bo_ref[...] = jax.nn.sigmoid(cols(OFF_BO))
    gcol = jnp.dot(hb, wgc_ref[...], preferred_element_type=F32) + gbc_ref[...]
    lane = lax.broadcasted_iota(jnp.int32, gcol.shape, 1)
    gc_ref[...] = jnp.where(lane >= N_HEADS, _log_sigmoid(gcol), gcol)
    grow = lax.dot_general(wgr_ref[...], hb, (((1,), (1,)), ((), ())),
                           preferred_element_type=F32) + gbr_ref[...]
    row = lax.broadcasted_iota(jnp.int32, grow.shape, 0)
    gr_ref[...] = jnp.where(row >= N_HEADS, _log_sigmoid(grow), grow)
    ck_ref[...] = _group_rms(cols(OFF_CK), ckn_ref[...], DIFF_QK_DIM).astype(BF16)
    nt_dims = (((1,), (1,)), ((), ()))
    cqt = lax.dot_general(wct_ref[0:GROUP, :], hb, nt_dims, preferred_element_type=F32)
    ss = _split_dot(cqt * cqt, _group_ones(DIFF_QK_DIM), 2, x_is_lhs=False)
    cqt = cqt * lax.rsqrt(ss * (1.0 / DIFF_QK_DIM) + EPS) * cqn_ref[...]
    cq_ref[0] = (cqt * (DIFF_QK_DIM ** -0.5 * LOG2E)).astype(BF16)
    cvt = lax.dot_general(wct_ref[GROUP:2 * GROUP, :], hb, nt_dims, preferred_element_type=F32).astype(BF16)
    for h in range(N_HEADS):
        cv_ref[0, h * VROWS:h * VROWS + HEAD_DIM, :] = cvt[h * HEAD_DIM:(h + 1) * HEAD_DIM, :]
        cv_ref[0, h * VROWS + HEAD_DIM:(h + 1) * VROWS, :] = jnp.ones((ONES_ROWS, cvt.shape[1]), BF16)
    dk_ref[...] = cols(OFF_DK).astype(BF16)
    dq_ref[0] = (lax.dot_general(wct_ref[2 * GROUP:3 * GROUP, :], hb, nt_dims, preferred_element_type=F32)
                 * (HEAD_DIM ** -0.5 * LOG2E)).astype(BF16)
    dv_ref[0] = lax.dot_general(wct_ref[3 * GROUP:4 * GROUP, :], hb, nt_dims,
                                preferred_element_type=F32).astype(BF16)


def _proj(x, gain, w_main, w_ct, w_gc, w_gr, gb_col, gb_row, aqn, akn, cqn_col, ckn):
    t = x.shape[0]
    tm = TM_PROJ
    row_spec = lambda w: pl.BlockSpec((tm, w), lambda i: (i, 0))
    bf = lambda w: jax.ShapeDtypeStruct((t, w), BF16)
    f32 = lambda w: jax.ShapeDtypeStruct((t, w), F32)
    tr_shape = jax.ShapeDtypeStruct((t // tm, GROUP, tm), BF16)
    tr_spec = pl.BlockSpec((1, GROUP, tm), lambda i: (i, 0, 0))
    vt_shape = jax.ShapeDtypeStruct((t // tm, N_HEADS * VROWS, tm), BF16)
    vt_spec = pl.BlockSpec((1, N_HEADS * VROWS, tm), lambda i: (i, 0, 0))
    out_shape = [bf(GROUP), bf(GROUP), bf(GROUP),
                 f32(2 * GROUP), bf(GROUP), f32(GROUP),
                 f32(GATE_LANES), jax.ShapeDtypeStruct((GATE_ROWS, t), F32),
                 tr_shape, bf(GROUP), vt_shape,
                 tr_shape, bf(GROUP), tr_shape]
    out_specs = ([row_spec(GROUP)] * 3 + [row_spec(2 * GROUP), row_spec(GROUP), row_spec(GROUP),
                                          row_spec(GATE_LANES), pl.BlockSpec((GATE_ROWS, tm), lambda i: (0, i))]
                 + [tr_spec, row_spec(GROUP), vt_spec, tr_spec, row_spec(GROUP), tr_spec])
    return pl.pallas_call(
        _proj_kernel,
        grid=(t // tm,),
        in_specs=[row_spec(D_MODEL), _const_spec((1, D_MODEL)),
                  _const_spec((D_MODEL, MAIN_COLS)), _const_spec((4 * GROUP, D_MODEL)),
                  _const_spec((D_MODEL, GATE_LANES)),
                  _const_spec((GATE_ROWS, D_MODEL)), _const_spec((1, GATE_LANES)),
                  _const_spec((GATE_ROWS, 1)),
                  _const_spec((1, GROUP)), _const_spec((1, GROUP)),
                  _const_spec((GROUP, 1)), _const_spec((1, GROUP))],
        out_specs=out_specs,
        out_shape=out_shape,
        compiler_params=_cparams(("parallel",)),
        name="mix_in_proj",
    )(x, gain.reshape(1, D_MODEL), w_main, w_ct, w_gc, w_gr, gb_col, gb_row, aqn, akn, cqn_col, ckn)


def _mix_a_kernel(q_ref, k0_ref, k1_ref, k2_ref, v0_ref, v1_ref, v2_ref, bias_ref, o_ref):
    t = pl.program_id(1)
    q = q_ref[...]
    k = jnp.concatenate([k0_ref[...], k1_ref[...], k2_ref[...]], axis=0)
    v = jnp.concatenate([v0_ref[...], v1_ref[...], v2_ref[...]], axis=0)
    nk = k.shape[0]
    key_pos = t * TQ_A - LEFT_CHUNKS * CHUNK + lax.broadcasted_iota(jnp.int32, (TQ_A, nk), 1)
    valid = key_pos >= 0
    out = jnp.zeros((TQ_A, GROUP), F32)
    for h in range(N_HEADS):
        qh = jnp.where(_head_mask(q.shape, h), q, jnp.zeros_like(q))
        s = lax.dot_general(qh, k, (((1,), (1,)), ((), ())), preferred_element_type=F32)
        s = jnp.where(valid, s + bias_ref[h], NEG)
        m = jnp.max(s, axis=-1, keepdims=True)
        e = jnp.exp(s - m)
        p = (e / jnp.sum(e, axis=-1, keepdims=True)).astype(BF16)
        vh = jnp.where(_head_mask(v.shape, h), v, jnp.zeros_like(v))
        out = out + jnp.dot(p, vh, preferred_element_type=F32)
    o_ref[...] = out.astype(BF16)


def _mix_a(q, k, v, bias_tile, nb, s):
    nt = s // TQ_A
    blk = lambda back: pl.BlockSpec((TQ_A, GROUP), lambda b, t: (b * nt + jnp.maximum(t - back, 0), 0))
    return pl.pallas_call(
        _mix_a_kernel,
        grid=(nb, nt),
        in_specs=[blk(0), blk(2), blk(1), blk(0), blk(2), blk(1), blk(0),
                  _const_spec(bias_tile.shape)],
        out_specs=blk(0),
        out_shape=jax.ShapeDtypeStruct((nb * s, GROUP), BF16),
        compiler_params=_cparams(("parallel", "parallel")),
        name="mix_a_chunk_attn",
    )(q, k, k, k, v, v, v, bias_tile)


def _mix_b_kernel(qk_ref, v_ref, og_ref, gc_ref, gr_ref, cw_ref, cb_ref, on_ref, o_ref,
                  xs_ref, c_ref, n_ref, m_ref):
    c_idx = pl.program_id(1)

    @pl.when(c_idx == 0)
    def _():
        xs_ref[0:8, :] = jnp.zeros((8, 2 * GROUP), F32)
        c_ref[...] = jnp.zeros_like(c_ref)
        n_ref[...] = jnp.zeros_like(n_ref)
        m_ref[...] = jnp.zeros_like(m_ref)

    xs_ref[8:8 + LB, :] = qk_ref[...]
    acc = jnp.broadcast_to(cb_ref[...], (LB, 2 * GROUP))
    for j in range(CONV_WIDTH):
        start = 8 - (CONV_WIDTH - 1) + j
        acc = acc + xs_ref[start:start + LB, :] * cw_ref[j:j + 1, :]
    xs_ref[0:8, :] = xs_ref[LB:LB + 8, :]
    qk = acc * jax.nn.sigmoid(acc)
    q = qk[:, :GROUP]
    k = qk[:, GROUP:] * (HEAD_DIM ** -0.5)
    qb = q.astype(BF16)
    kb = k.astype(BF16)
    v = v_ref[...]

    r = lax.broadcasted_iota(jnp.int32, (LB, LB), 0)
    c = lax.broadcasted_iota(jnp.int32, (LB, LB), 1)
    causal = c <= r
    tri = jnp.where(causal, 1.0, 0.0).astype(BF16)
    tri_t = jnp.where(r <= c, 1.0, 0.0).astype(BF16)
    gcol = gc_ref[...]
    grow = gr_ref[...]
    bcum_col = _split_dot(gcol, tri, 3, x_is_lhs=False)
    bcum_row = _split_dot(grow, tri_t, 3)

    c_state = c_ref[...]
    n_state = n_ref[...]
    q_c = jnp.dot(qb, c_state.astype(BF16), preferred_element_type=F32)
    qn = qb.astype(F32) * n_state.astype(BF16).astype(F32)
    q_n = _split_dot(qn, _group_ones(HEAD_DIM), 2)

    lane = lax.broadcasted_iota(jnp.int32, (LB, GROUP), 1) // HEAD_DIM
    num = jnp.zeros((LB, GROUP), F32)
    den = jnp.zeros((LB, GROUP), F32)
    floor = jnp.zeros((LB, GROUP), F32)
    wg_full = jnp.zeros((LB, GROUP), F32)
    a_full = jnp.zeros((1, GROUP), F32)
    lane1 = lax.broadcasted_iota(jnp.int32, (1, GROUP), 1) // HEAD_DIM
    a_list = []
    for h in range(N_HEADS):
        i_col = gcol[:, h:h + 1]
        i_row = grow[h:h + 1, :]
        b_col = bcum_col[:, N_HEADS + h:N_HEADS + h + 1]
        b_row = bcum_row[N_HEADS + h:N_HEADS + h + 1, :]
        m_prev = m_ref[h:h + 1, 0:1]
        dmat = jnp.where(causal, b_col - b_row + i_row, NEG)
        inter = b_col + m_prev
        m_t = jnp.maximum(inter, jnp.max(dmat, axis=-1, keepdims=True))
        w_intra = jnp.exp(dmat - m_t)
        s_inter = jnp.exp(inter - m_t)
        qh = jnp.where(_head_mask(qb.shape, h), qb, jnp.zeros_like(qb))
        sc = lax.dot_general(qh, kb, (((1,), (1,)), ((), ())), preferred_element_type=F32) * w_intra
        vh = jnp.where(_head_mask(v.shape, h), v, jnp.zeros_like(v))
        num_h = s_inter * q_c + jnp.dot(sc.astype(BF16), vh, preferred_element_type=F32)
        den_h = s_inter * q_n + jnp.sum(sc, axis=-1, keepdims=True)
        sel = lane == h
        num = jnp.where(sel, num_h, num)
        den = jnp.where(sel, den_h, den)
        floor = jnp.where(sel, jnp.exp(-m_t), floor)
        b_tot = b_col[LB - 1:LB, :]
        g = b_tot - b_col + i_col
        m_new = jnp.maximum(b_tot + m_prev, jnp.max(g, axis=0, keepdims=True))
        a_h = jnp.exp(b_tot + m_prev - m_new)
        a_list.append(a_h)
        a_full = jnp.where(lane1 == h, a_h, a_full)
        wg_full = jnp.where(sel, jnp.exp(g - m_new), wg_full)
        m_ref[h:h + 1, :] = jnp.broadcast_to(m_new, (1, m_ref.shape[1]))

    hb = num / jnp.maximum(jnp.abs(den), floor)
    o_ref[...] = (_group_rms(hb, on_ref[...], HEAD_DIM) * og_ref[...]).astype(BF16)

    kw = k * wg_full
    kv = jnp.dot(kw.T.astype(BF16), v, preferred_element_type=F32)
    r2 = lax.broadcasted_iota(jnp.int32, (GROUP, GROUP), 0) // HEAD_DIM
    c2 = lax.broadcasted_iota(jnp.int32, (GROUP, GROUP), 1) // HEAD_DIM
    a_mat = jnp.zeros((GROUP, GROUP), F32)
    for h in range(N_HEADS):
        a_mat = jnp.where(r2 == h, a_list[h], a_mat)
    c_ref[...] = a_mat * c_state + jnp.where(r2 == c2, kv, 0.0)
    n_ref[...] = a_full * n_state + jnp.sum(kw, axis=0, keepdims=True)


def _mix_b(bqk, bv, bo, gcol, grow, conv_w, conv_b, out_norm, nb, s):
    nc = s // LB
    row = lambda w: pl.BlockSpec((LB, w), lambda b, c: (b * nc + c, 0))
    return pl.pallas_call(
        _mix_b_kernel,
        grid=(nb, nc),
        in_specs=[row(2 * GROUP), row(GROUP), row(GROUP), row(GATE_LANES),
                  pl.BlockSpec((GATE_ROWS, LB), lambda b, c: (0, b * nc + c)),
                  _const_spec((CONV_WIDTH, 2 * GROUP)), _const_spec((1, 2 * GROUP)),
                  _const_spec((1, GROUP))],
        out_specs=row(GROUP),
        out_shape=jax.ShapeDtypeStruct((nb * s, GROUP), BF16),
        scratch_shapes=[pltpu.VMEM((LB + 8, 2 * GROUP), F32),
                        pltpu.VMEM((GROUP, GROUP), F32),
                        pltpu.VMEM((1, GROUP), F32),
                        pltpu.VMEM((8, 128), F32)],
        compiler_params=_cparams(("parallel", "arbitrary")),
        name="mix_b_mlstm",
    )(bqk, bv, bo, gcol, grow, conv_w, conv_b, out_norm)


def _mix_c_kernel(lam_ref, qt_ref, k_ref, vt_ref, on_ref, o_ref, qm_ref, p_ref, acc_ref, m_ref, al_ref, *,
                  lam_init):
    qt = pl.program_id(1)
    q_t = qt_ref[0]
    n_maps = 2 * N_HEADS
    comp_row = lax.broadcasted_iota(jnp.int32, q_t.shape, 0) // DIFF_QK_DIM
    for j in range(n_maps):
        qm_ref[j] = jnp.where(comp_row == j, q_t, jnp.zeros_like(q_t))
    acc_ref[...] = jnp.zeros(acc_ref.shape, F32)

    def scores(kt):
        start = pl.multiple_of(kt * TK_C, TK_C)
        k = k_ref[pl.ds(start, TK_C), :]
        return [jnp.dot(k, qm_ref[j], preferred_element_type=F32) for j in range(n_maps)]

    def softmax(sc, slot, masked=False):
        if masked:
            key_chunk = lax.broadcasted_iota(jnp.int32, (TK_C, TQ_C), 0) // CHUNK
            q_chunk = lax.broadcasted_iota(jnp.int32, (TK_C, TQ_C), 1) // CHUNK
            vis = key_chunk <= q_chunk
        for j in range(n_maps):
            s = jnp.where(vis, sc[j], NEG) if masked else sc[j]
            m_old = m_ref[j]
            m_new = jnp.maximum(m_old, jnp.max(s, axis=0, keepdims=True))
            al_ref[j] = jnp.exp2(m_old - m_new)
            m_ref[j] = m_new
            p_ref[slot, j] = jnp.exp2(s - m_new).astype(BF16)

    def pv_update(kt, slot):
        v_ext = vt_ref[kt]
        for j in range(n_maps):
            h = j // 2
            pv = jnp.dot(v_ext[h * VROWS:(h + 1) * VROWS, :], p_ref[slot, j], preferred_element_type=F32)
            acc_ref[j] = al_ref[j] * acc_ref[j] + pv

    def step(kt, prev, slot):
        sc = scores(kt)
        pv_update(prev, slot)
        softmax(sc, 1 - slot)

    m_ref[...] = jnp.full(m_ref.shape, NEG, F32)
    softmax(scores(qt), 0, masked=True)
    n_pairs = qt // 2

    def body(ip, carry):
        a = 2 * ip
        step(a, jnp.where(ip == 0, qt, a - 1), 0)
        step(a + 1, a, 1)
        return carry

    lax.fori_loop(0, n_pairs, body, 0)
    last = jnp.where(n_pairs > 0, 2 * n_pairs - 1, qt)
    odd = qt - 2 * n_pairs

    @pl.when(odd == 1)
    def _():
        step(qt - 1, last, 0)
        pv_update(qt - 1, 1)

    @pl.when(odd == 0)
    def _():
        pv_update(last, 0)

    lam = lam_ref[0]
    heads = []
    for h in range(N_HEADS):
        a0 = acc_ref[2 * h]
        a1 = acc_ref[2 * h + 1]
        o_h = (a0[:HEAD_DIM] * (1.0 / a0[HEAD_DIM:HEAD_DIM + 1])
               - lam * (a1[:HEAD_DIM] * (1.0 / a1[HEAD_DIM:HEAD_DIM + 1])))
        ms = jnp.mean(o_h * o_h, axis=0, keepdims=True)
        heads.append(o_h * lax.rsqrt(ms + EPS))
    out_t = jnp.concatenate(heads, axis=0) * (on_ref[...] * (1.0 - lam_init))
    o_ref[...] = out_t.T.astype(BF16)


def _mix_c(lam, q_t, k, v_t, out_norm_col, nb, s, lam_init):
    nt = s // TQ_C
    tile_t = pl.BlockSpec((1, GROUP, TQ_C), lambda b, t, lam_ref: (b * nt + t, 0, 0))
    full_k = pl.BlockSpec((s, GROUP), lambda b, t, lam_ref: (b, 0), pipeline_mode=pl.Buffered(1))
    full_vt = pl.BlockSpec((nt, N_HEADS * VROWS, TK_C), lambda b, t, lam_ref: (b, 0, 0),
                           pipeline_mode=pl.Buffered(1))
    grid_spec = pltpu.PrefetchScalarGridSpec(
        num_scalar_prefetch=1,
        grid=(nb, nt),
        in_specs=[tile_t, full_k, full_vt,
                  pl.BlockSpec((GROUP, 1), lambda b, t, lam_ref: (0, 0), pipeline_mode=pl.Buffered(1))],
        out_specs=pl.BlockSpec((TQ_C, GROUP), lambda b, t, lam_ref: (b * nt + t, 0)),
        scratch_shapes=[pltpu.VMEM((2 * N_HEADS, GROUP, TQ_C), BF16),
                        pltpu.VMEM((2, 2 * N_HEADS, TK_C, TQ_C), BF16),
                        pltpu.VMEM((2 * N_HEADS, VROWS, TQ_C), F32),
                        pltpu.VMEM((2 * N_HEADS, 1, TQ_C), F32),
                        pltpu.VMEM((2 * N_HEADS, 1, TQ_C), F32)],
    )
    return pl.pallas_call(
        functools.partial(_mix_c_kernel, lam_init=lam_init),
        grid_spec=grid_spec,
        out_shape=jax.ShapeDtypeStruct((nb * s, GROUP), BF16),
        compiler_params=_cparams(("parallel", "parallel")),
        name="mix_c_diff_attn",
    )(lam, q_t, k, v_t, out_norm_col)


def _mix_d_kernel(qt_ref, k_ref, vt_ref, o_ref, qm_ref, acc_ref, run_ref):
    qt = pl.program_id(1)
    q_t = qt_ref[0]
    head_row = lax.broadcasted_iota(jnp.int32, q_t.shape, 0) // HEAD_DIM
    for h in range(N_HEADS):
        qm_ref[h] = jnp.where(head_row == h, q_t, jnp.zeros_like(q_t))
    acc_ref[...] = jnp.zeros(acc_ref.shape, F32)
    run_ref[...] = jnp.zeros(run_ref.shape, F32)
    r = lax.broadcasted_iota(jnp.int32, (TK_D, TK_D), 0)
    c = lax.broadcasted_iota(jnp.int32, (TK_D, TK_D), 1)
    later = jnp.where(c > r, 1.0, 0.0).astype(BF16)

    def tile(kt, masked):
        start = pl.multiple_of(kt * TK_D, TK_D)
        k = k_ref[pl.ds(start, TK_D), :]
        v_t = vt_ref[kt]
        zs = [jnp.dot(k, qm_ref[h], preferred_element_type=F32) for h in range(N_HEADS)]
        if masked:
            before = (lax.broadcasted_iota(jnp.int32, (TK_D, TQ_D), 0)
                      < lax.broadcasted_iota(jnp.int32, (TK_D, TQ_D), 1))
        worst = None
        log_take, between = [], []
        for h in range(N_HEADS):
            z = zs[h]
            log_keep = jnp.minimum(-z, 0.0) - jnp.log2(1.0 + jnp.exp2(-jnp.abs(z)))
            log_take.append(z + log_keep)
            if masked:
                log_keep = jnp.where(before, log_keep, 0.0)
            run = run_ref[h]
            between.append(run + _split_dot(log_keep, later, 2, x_is_lhs=False))
            run = run + jnp.sum(log_keep, axis=0, keepdims=True)
            run_ref[h] = run
            top = jnp.max(run)
            worst = top if worst is None else jnp.maximum(worst, top)
        for h in range(N_HEADS):
            a = jnp.exp2(log_take[h] + between[h])
            if masked:
                a = jnp.where(before, a, 0.0)
            acc_ref[h] += jnp.dot(v_t[h * HEAD_DIM:(h + 1) * HEAD_DIM, :], a.astype(BF16),
                                  preferred_element_type=F32)
        return worst

    worst = tile(qt, True)

    def cond(state):
        kt, worst = state
        return jnp.logical_and(kt >= 0, worst > SB_SKIP_LOG2)

    def body(state):
        kt, _ = state
        return kt - 1, tile(kt, False)

    lax.while_loop(cond, body, (qt - 1, worst))
    out_t = jnp.concatenate([acc_ref[h] for h in range(N_HEADS)], axis=0)
    o_ref[...] = out_t.T.astype(BF16)


def _mix_d(q_t, k, v_t, nb, s):
    nt = s // TQ_D
    tile_t = pl.BlockSpec((1, GROUP, TQ_D), lambda b, t: (b * nt + t, 0, 0))
    full_k = pl.BlockSpec((s, GROUP), lambda b, t: (b, 0), pipeline_mode=pl.Buffered(1))
    full_vt = pl.BlockSpec((nt, GROUP, TK_D), lambda b, t: (b, 0, 0), pipeline_mode=pl.Buffered(1))
    return pl.pallas_call(
        _mix_d_kernel,
        grid=(nb, nt),
        in_specs=[tile_t, full_k, full_vt],
        out_specs=pl.BlockSpec((TQ_D, GROUP), lambda b, t: (b * nt + t, 0)),
        out_shape=jax.ShapeDtypeStruct((nb * s, GROUP), BF16),
        scratch_shapes=[pltpu.VMEM((N_HEADS, GROUP, TQ_D), BF16),
                        pltpu.VMEM((N_HEADS, HEAD_DIM, TQ_D), F32),
                        pltpu.VMEM((N_HEADS, 1, TQ_D), F32)],
        compiler_params=_cparams(("parallel", "parallel")),
        name="mix_d_stick_breaking",
    )(q_t, k, v_t)


def _out_proj_kernel(x_ref, ya_ref, yb_ref, yc_ref, yd_ref, w_ref, o_ref):
    acc = x_ref[...]
    for g, y_ref in enumerate((ya_ref, yb_ref, yc_ref, yd_ref)):
        acc = acc + jnp.dot(y_ref[...], w_ref[g * GROUP:(g + 1) * GROUP, :], preferred_element_type=F32)
    o_ref[...] = acc


def _out_proj(x, ya, yb, yc, yd, w_out):
    t = x.shape[0]
    tm = min(TM_FFN, t)
    row = lambda w: pl.BlockSpec((tm, w), lambda i: (i, 0))
    return pl.pallas_call(
        _out_proj_kernel,
        grid=(t // tm,),
        in_specs=[row(D_MODEL), row(GROUP), row(GROUP), row(GROUP), row(GROUP),
                  _const_spec((D_MODEL, D_MODEL))],
        out_specs=row(D_MODEL),
        out_shape=jax.ShapeDtypeStruct((t, D_MODEL), F32),
        compiler_params=_cparams(("parallel",)),
        name="mix_out_proj",
    )(x, ya, yb, yc, yd, w_out)


def _rel_bias_tile(rel_bias):
    nk = TQ_A + LEFT_CHUNKS * CHUNK
    i = np.arange(TQ_A)[:, None]
    j = np.arange(nk)[None, :]
    dc = j // CHUNK - i // CHUNK
    visible = (dc >= 0) & (dc <= LEFT_CHUNKS)
    p = TQ_A + nk
    d = np.arange(p)
    rel = d - (nk - 1) + LEFT_CHUNKS * CHUNK
    idx = np.clip(rel, -REL_CLIP, REL_CLIP) + REL_CLIP
    diag = rel_bias[:, idx].astype(F32)
    nh = rel_bias.shape[0]
    skew = jnp.tile(diag, (1, TQ_A + 1))[:, :TQ_A * (p + 1)].reshape(nh, TQ_A, p + 1)
    toeplitz = skew[:, :, :nk][:, :, ::-1]
    return jnp.where(jnp.asarray(visible)[None], toeplitz, NEG)


def _permute_w_in(w_in):
    sizes = [GROUP, GROUP, GROUP, 2 * GROUP, GROUP, GROUP, N_HEADS, N_HEADS,
             GROUP, GROUP, GROUP, GROUP, GROUP, GROUP]
    offs = np.concatenate([[0], np.cumsum(sizes)])
    seg = [w_in[:, offs[n]:offs[n + 1]] for n in range(len(sizes))]
    main = jnp.concatenate(seg[0:6] + [seg[9], seg[12]], axis=1).astype(BF16)
    c_t = jnp.concatenate([seg[8].T, seg[10].T, seg[11].T, seg[13].T], axis=0).astype(BF16)
    gates = jnp.concatenate([seg[6], seg[7]], axis=1)
    g_col = jnp.pad(gates, ((0, 0), (0, GATE_LANES - 2 * N_HEADS))).astype(BF16)
    g_row = jnp.pad(gates.T, ((0, GATE_ROWS - 2 * N_HEADS), (0, 0))).astype(BF16)
    return main, c_t, g_col, g_row


def _layer(x, nb, s, lam_init, p):
    x = _ffn(x, p["ffn1_norm"], p["ffn1_wg"].astype(BF16), p["ffn1_wu"].astype(BF16), p["ffn1_wd"].astype(BF16))

    w_main, w_ct, w_gc, w_gr = _permute_w_in(p["w_in"])
    gate_bias = p["b_gate_bias"].reshape(2 * N_HEADS).astype(F32)
    gb_col = jnp.pad(gate_bias, (0, GATE_LANES - 2 * N_HEADS)).reshape(1, GATE_LANES)
    gb_row = jnp.pad(gate_bias, (0, GATE_ROWS - 2 * N_HEADS)).reshape(GATE_ROWS, 1)
    tile4 = lambda g: jnp.tile(g.astype(F32), GROUP // g.shape[0]).reshape(1, GROUP)
    (aq, ak, av, bqk, bv, bo, gcol, grow, cq_t, ck, cv_t, dq_t, dk, dv_t) = _proj(
        x, p["mix_norm"], w_main, w_ct, w_gc, w_gr, gb_col, gb_row,
        tile4(p["a_q_norm"]), tile4(p["a_k_norm"]), tile4(p["c_q_norm"]).reshape(GROUP, 1), tile4(p["c_k_norm"]))

    ya = _mix_a(aq, ak, av, _rel_bias_tile(p["a_rel_bias"]), nb, s)
    yb = _mix_b(bqk, bv, bo, gcol, grow, p["b_conv_w"].astype(F32), p["b_conv_b"].astype(F32).reshape(1, -1),
                p["b_out_norm"].astype(F32).reshape(1, GROUP), nb, s)
    lv = p["c_lambda"].astype(F32)
    lam = jnp.exp(jnp.sum(lv[0] * lv[1])) - jnp.exp(jnp.sum(lv[2] * lv[3])) + lam_init
    yc = _mix_c(lam.reshape(1), cq_t, ck, cv_t, tile4(p["c_out_norm"]).reshape(GROUP, 1), nb, s, lam_init)
    yd = _mix_d(dq_t, dk, dv_t, nb, s)

    x = _out_proj(x, ya, yb, yc, yd, p["w_out"].astype(BF16))
    return _ffn(x, p["ffn2_norm"], p["ffn2_wg"].astype(BF16), p["ffn2_wu"].astype(BF16), p["ffn2_wd"].astype(BF16))


_PARAM_NAMES = ("ffn1_norm", "ffn1_wg", "ffn1_wu", "ffn1_wd", "mix_norm", "w_in", "a_q_norm", "a_k_norm",
                "a_rel_bias", "b_conv_w", "b_conv_b", "b_gate_bias", "b_out_norm", "c_q_norm", "c_k_norm",
                "c_lambda", "c_out_norm", "w_out", "ffn2_norm", "ffn2_wg", "ffn2_wu", "ffn2_wd")


def kernel(x, ffn1_norm, ffn1_wg, ffn1_wu, ffn1_wd, mix_norm, w_in, a_q_norm, a_k_norm, a_rel_bias,
           b_conv_w, b_conv_b, b_gate_bias, b_out_norm, c_q_norm, c_k_norm, c_lambda, c_out_norm,
           w_out, ffn2_norm, ffn2_wg, ffn2_wu, ffn2_wd):
    params = dict(zip(_PARAM_NAMES, (ffn1_norm, ffn1_wg, ffn1_wu, ffn1_wd, mix_norm, w_in, a_q_norm, a_k_norm,
                                     a_rel_bias, b_conv_w, b_conv_b, b_gate_bias, b_out_norm, c_q_norm,
                                     c_k_norm, c_lambda, c_out_norm, w_out, ffn2_norm, ffn2_wg, ffn2_wu,
                                     ffn2_wd)))
    nb, s, d = x.shape
    depth = ffn1_norm.shape[0]
    h = x.reshape(nb * s, d)
    for l in range(depth):
        lam_init = 0.8 - 0.6 * math.exp(-0.3 * l)
        h = _layer(h, nb, s, lam_init, {k: v[l] for k, v in params.items()})
    return h.reshape(nb, s, d)
```

```python
import functools
import math

import jax
import jax.numpy as jnp
import numpy as np
from jax import lax
from jax.experimental import pallas as pl
from jax.experimental.pallas import tpu as pltpu

F32 = jnp.float32
BF16 = jnp.bfloat16

D_MODEL = 1024
D_FF = 2816
CHUNK = 64
HEAD_DIM = 64
N_HEADS = 4
GROUP = N_HEADS * HEAD_DIM
LEFT_CHUNKS = 8
REL_CLIP = 128
CONV_WIDTH = 4
DIFF_QK_DIM = HEAD_DIM // 2
EPS = 1e-6
NEG = -1e30
LOG2E = 1.4426950408889634

OFF_AQ, OFF_AK, OFF_AV = 0, 256, 512
OFF_BQK, OFF_BV, OFF_BO = 768, 1280, 1536
OFF_CK = 1792
OFF_DK = 2048
MAIN_COLS = 2304
GATE_LANES = 128
GATE_ROWS = 16
ONES_ROWS = 16
VROWS = HEAD_DIM + ONES_ROWS

VMEM_LIMIT = 56 * 1024 * 1024

TM_FFN = 512
TM_PROJ = 512
TILE_T = 256
TQ_A = 256
LB = 256
TQ_C = 256
TK_C = 256
TQ_D = 256
TK_D = 256
SB_SKIP_LOG2 = -160.0


def _cparams(sem):
    return pltpu.CompilerParams(dimension_semantics=sem, vmem_limit_bytes=VMEM_LIMIT)


def _const_spec(shape):
    nd = len(shape)
    return pl.BlockSpec(shape, lambda *_: (0,) * nd, pipeline_mode=pl.Buffered(1))


def _split_dot(x, mat, terms, x_is_lhs=True):
    acc = None
    rem = x
    for term in range(terms):
        part = rem.astype(BF16)
        if term + 1 < terms:
            rem = rem - part.astype(F32)
        d = (jnp.dot(part, mat, preferred_element_type=F32) if x_is_lhs
             else jnp.dot(mat, part, preferred_element_type=F32))
        acc = d if acc is None else acc + d
    return acc


def _group_ones(width):
    r = lax.broadcasted_iota(jnp.int32, (GROUP, GROUP), 0) // width
    c = lax.broadcasted_iota(jnp.int32, (GROUP, GROUP), 1) // width
    return jnp.where(r == c, 1.0, 0.0).astype(BF16)


def _group_rms(x, gain, width):
    ss = _split_dot(x * x, _group_ones(width), 2)
    return x * lax.rsqrt(ss * (1.0 / width) + EPS) * gain


def _log_sigmoid(x):
    return jnp.minimum(x, 0.0) - jnp.log1p(jnp.exp(-jnp.abs(x)))


def _head_mask(shape, h, axis=1, width=HEAD_DIM):
    lane = lax.broadcasted_iota(jnp.int32, shape, axis)
    return (lane // width) == h


def _ffn_half_step(x, g_ref, wg_ref, wu_ref, wd_ref):
    xn = x * lax.rsqrt(jnp.mean(x * x, axis=-1, keepdims=True) + EPS) * g_ref[...]
    xb = xn.astype(BF16)
    g = jnp.dot(xb, wg_ref[...], preferred_element_type=F32)
    u = jnp.dot(xb, wu_ref[...], preferred_element_type=F32)
    h = (g * jax.nn.sigmoid(g) * u).astype(BF16)
    return x + 0.5 * jnp.dot(h, wd_ref[...], preferred_element_type=F32)


def _ffn_kernel(x_ref, g_ref, wg_ref, wu_ref, wd_ref, o_ref):
    o_ref[...] = _ffn_half_step(x_ref[...], g_ref, wg_ref, wu_ref, wd_ref)


def _mix_out_ffn_kernel(x_ref, ya_ref, yb_ref, yc_ref, yd_ref, wo_ref, g_ref, wg_ref, wu_ref, wd_ref, o_ref):
    x = x_ref[...]
    for grp, y_ref in enumerate((ya_ref, yb_ref, yc_ref, yd_ref)):
        x = x + jnp.dot(y_ref[...], wo_ref[grp * GROUP:(grp + 1) * GROUP, :], preferred_element_type=F32)
    o_ref[...] = _ffn_half_step(x, g_ref, wg_ref, wu_ref, wd_ref)


def _ffn_specs():
    return [_const_spec((1, D_MODEL)), _const_spec((D_MODEL, D_FF)), _const_spec((D_MODEL, D_FF)),
            _const_spec((D_FF, D_MODEL))]


def _ffn(x, gain, wg, wu, wd):
    t = x.shape[0]
    tm = min(TM_FFN, t)
    row = pl.BlockSpec((tm, D_MODEL), lambda i: (i, 0))
    return pl.pallas_call(
        _ffn_kernel,
        grid=(t // tm,),
        in_specs=[row] + _ffn_specs(),
        out_specs=row,
        out_shape=jax.ShapeDtypeStruct((t, D_MODEL), F32),
        compiler_params=_cparams(("parallel",)),
        name="ffn_half_step",
    )(x, gain.reshape(1, D_MODEL), wg, wu, wd)


def _mix_out_ffn(x, ya, yb, yc, yd, w_out, gain, wg, wu, wd):
    t = x.shape[0]
    tm = min(TM_FFN, t)
    row = lambda w: pl.BlockSpec((tm, w), lambda i: (i, 0))
    return pl.pallas_call(
        _mix_out_ffn_kernel,
        grid=(t // tm,),
        in_specs=[row(D_MODEL), row(GROUP), row(GROUP), row(GROUP), row(GROUP),
                  _const_spec((D_MODEL, D_MODEL))] + _ffn_specs(),
        out_specs=row(D_MODEL),
        out_shape=jax.ShapeDtypeStruct((t, D_MODEL), F32),
        compiler_params=_cparams(("parallel",)),
        name="mix_out_proj_ffn",
    )(x, ya, yb, yc, yd, w_out, gain.reshape(1, D_MODEL), wg, wu, wd)


def _proj_kernel(x_ref, g_ref, w_ref, wct_ref, wgc_ref, wgr_ref, gbc_ref, gbr_ref,
                 aqn_ref, akn_ref, cqn_ref, ckn_ref,
                 aq_ref, ak_ref, av_ref, bqk_ref, bv_ref, bo_ref, gc_ref, gr_ref,
                 cq_ref, ck_ref, cv_ref, dq_ref, dk_ref, dv_ref):
    x = x_ref[...]
    hn = x * lax.rsqrt(jnp.mean(x * x, axis=-1, keepdims=True) + EPS) * g_ref[...]
    hb = hn.astype(BF16)

    def cols(off, width=GROUP):
        return jnp.dot(hb, w_ref[:, off:off + width], preferred_element_type=F32)

    aq_ref[...] = (_group_rms(cols(OFF_AQ), aqn_ref[...], HEAD_DIM) * (HEAD_DIM ** -0.5)).astype(BF16)
    ak_ref[...] = _group_rms(cols(OFF_AK), akn_ref[...], HEAD_DIM).astype(BF16)
    av_ref[...] = cols(OFF_AV).astype(BF16)
    bqk_ref[...] = cols(OFF_BQK, 2 * GROUP)
    bv_ref[...] = cols(OFF_BV).astype(BF16)
    bo_ref[...] = jax.nn.sigmoid(cols(OFF_BO))
    gcol = jnp.dot(hb, wgc_ref[...], preferred_element_type=F32) + gbc_ref[...]
    lane = lax.broadcasted_iota(jnp.int32, gcol.shape, 1)
    gc_ref[...] = jnp.where(lane >= N_HEADS, _log_sigmoid(gcol), gcol)
    grow = lax.dot_general(wgr_ref[...], hb, (((1,), (1,)), ((), ())),
                           preferred_element_type=F32) + gbr_ref[...]
    row = lax.broadcasted_iota(jnp.int32, grow.shape, 0)
    gr_ref[...] = jnp.where(row >= N_HEADS, _log_sigmoid(grow), grow)
    ck_ref[...] = _group_rms(cols(OFF_CK), ckn_ref[...], DIFF_QK_DIM).astype(BF16)
    nt_dims = (((1,), (1,)), ((), ()))
    cqt = lax.dot_general(wct_ref[0:GROUP, :], hb, nt_dims, preferred_element_type=F32)
    ss = _split_dot(cqt * cqt, _group_ones(DIFF_QK_DIM), 2, x_is_lhs=False)
    cqt = cqt * lax.rsqrt(ss * (1.0 / DIFF_QK_DIM) + EPS) * cqn_ref[...]
    cqt = (cqt * (DIFF_QK_DIM ** -0.5 * LOG2E)).astype(BF16)
    cvt = lax.dot_general(wct_ref[GROUP:2 * GROUP, :], hb, nt_dims, preferred_element_type=F32).astype(BF16)
    dk_ref[...] = cols(OFF_DK).astype(BF16)
    dqt = (lax.dot_general(wct_ref[2 * GROUP:3 * GROUP, :], hb, nt_dims, preferred_element_type=F32)
           * (HEAD_DIM ** -0.5 * LOG2E)).astype(BF16)
    dvt = lax.dot_general(wct_ref[3 * GROUP:4 * GROUP, :], hb, nt_dims, preferred_element_type=F32).astype(BF16)
    for sub in range(cq_ref.shape[0]):
        tok = slice(sub * TILE_T, (sub + 1) * TILE_T)
        cq_ref[sub] = cqt[:, tok]
        dq_ref[sub] = dqt[:, tok]
        dv_ref[sub] = dvt[:, tok]
        for h in range(N_HEADS):
            cv_ref[sub, h * VROWS:h * VROWS + HEAD_DIM, :] = cvt[h * HEAD_DIM:(h + 1) * HEAD_DIM, tok]
            cv_ref[sub, h * VROWS + HEAD_DIM:(h + 1) * VROWS, :] = jnp.ones((ONES_ROWS, TILE_T), BF16)


def _proj(x, gain, w_main, w_ct, w_gc, w_gr, gb_col, gb_row, aqn, akn, cqn_col, ckn):
    t = x.shape[0]
    tm = TM_PROJ
    row_spec = lambda w: pl.BlockSpec((tm, w), lambda i: (i, 0))
    bf = lambda w: jax.ShapeDtypeStruct((t, w), BF16)
    f32 = lambda w: jax.ShapeDtypeStruct((t, w), F32)
    n_sub = tm // TILE_T
    tr_shape = jax.ShapeDtypeStruct((t // TILE_T, GROUP, TILE_T), BF16)
    tr_spec = pl.BlockSpec((n_sub, GROUP, TILE_T), lambda i: (i, 0, 0))
    vt_shape = jax.ShapeDtypeStruct((t // TILE_T, N_HEADS * VROWS, TILE_T), BF16)
    vt_spec = pl.BlockSpec((n_sub, N_HEADS * VROWS, TILE_T), lambda i: (i, 0, 0))
    out_shape = [bf(GROUP), bf(GROUP), bf(GROUP),
                 f32(2 * GROUP), bf(GROUP), f32(GROUP),
                 f32(GATE_LANES), jax.ShapeDtypeStruct((GATE_ROWS, t), F32),
                 tr_shape, bf(GROUP), vt_shape,
                 tr_shape, bf(GROUP), tr_shape]
    out_specs = ([row_spec(GROUP)] * 3 + [row_spec(2 * GROUP), row_spec(GROUP), row_spec(GROUP),
                                          row_spec(GATE_LANES), pl.BlockSpec((GATE_ROWS, tm), lambda i: (0, i))]
                 + [tr_spec, row_spec(GROUP), vt_spec, tr_spec, row_spec(GROUP), tr_spec])
    return pl.pallas_call(
        _proj_kernel,
        grid=(t // tm,),
        in_specs=[row_spec(D_MODEL), _const_spec((1, D_MODEL)),
                  _const_spec((D_MODEL, MAIN_COLS)), _const_spec((4 * GROUP, D_MODEL)),
                  _const_spec((D_MODEL, GATE_LANES)),
                  _const_spec((GATE_ROWS, D_MODEL)), _const_spec((1, GATE_LANES)),
                  _const_spec((GATE_ROWS, 1)),
                  _const_spec((1, GROUP)), _const_spec((1, GROUP)),
                  _const_spec((GROUP, 1)), _const_spec((1, GROUP))],
        out_specs=out_specs,
        out_shape=out_shape,
        compiler_params=_cparams(("parallel",)),
        name="mix_in_proj",
    )(x, gain.reshape(1, D_MODEL), w_main, w_ct, w_gc, w_gr, gb_col, gb_row, aqn, akn, cqn_col, ckn)


def _mix_a_kernel(q_ref, k0_ref, k1_ref, k2_ref, v0_ref, v1_ref, v2_ref, bias_ref, o_ref):
    t = pl.program_id(1)
    q = q_ref[...]
    k = jnp.concatenate([k0_ref[...], k1_ref[...], k2_ref[...]], axis=0)
    v = jnp.concatenate([v0_ref[...], v1_ref[...], v2_ref[...]], axis=0)
    nk = k.shape[0]
    key_pos = t * TQ_A - LEFT_CHUNKS * CHUNK + lax.broadcasted_iota(jnp.int32, (TQ_A, nk), 1)
    valid = key_pos >= 0
    out = jnp.zeros((TQ_A, GROUP), F32)
    for h in range(N_HEADS):
        qh = jnp.where(_head_mask(q.shape, h), q, jnp.zeros_like(q))
        s = lax.dot_general(qh, k, (((1,), (1,)), ((), ())), preferred_element_type=F32)
        s = jnp.where(valid, s + bias_ref[h], NEG)
        m = jnp.max(s, axis=-1, keepdims=True)
        e = jnp.exp(s - m)
        p = (e / jnp.sum(e, axis=-1, keepdims=True)).astype(BF16)
        vh = jnp.where(_head_mask(v.shape, h), v, jnp.zeros_like(v))
        out = out + jnp.dot(p, vh, preferred_element_type=F32)
    o_ref[...] = out.astype(BF16)


def _mix_a(q, k, v, bias_tile, nb, s):
    nt = s // TQ_A
    blk = lambda back: pl.BlockSpec((TQ_A, GROUP), lambda b, t: (b * nt + jnp.maximum(t - back, 0), 0))
    return pl.pallas_call(
        _mix_a_kernel,
        grid=(nb, nt),
        in_specs=[blk(0), blk(2), blk(1), blk(0), blk(2), blk(1), blk(0),
                  _const_spec(bias_tile.shape)],
        out_specs=blk(0),
        out_shape=jax.ShapeDtypeStruct((nb * s, GROUP), BF16),
        compiler_params=_cparams(("parallel", "parallel")),
        name="mix_a_chunk_attn",
    )(q, k, k, k, v, v, v, bias_tile)


def _mix_b_kernel(qk_ref, v_ref, og_ref, gc_ref, gr_ref, cw_ref, cb_ref, on_ref, o_ref,
                  xs_ref, c_ref, n_ref, m_ref):
    c_idx = pl.program_id(1)

    @pl.when(c_idx == 0)
    def _():
        xs_ref[0:8, :] = jnp.zeros((8, 2 * GROUP), F32)
        c_ref[...] = jnp.zeros_like(c_ref)
        n_ref[...] = jnp.zeros_like(n_ref)
        m_ref[...] = jnp.zeros_like(m_ref)

    xs_ref[8:8 + LB, :] = qk_ref[...]
    acc = jnp.broadcast_to(cb_ref[...], (LB, 2 * GROUP))
    for j in range(CONV_WIDTH):
        start = 8 - (CONV_WIDTH - 1) + j
        acc = acc + xs_ref[start:start + LB, :] * cw_ref[j:j + 1, :]
    xs_ref[0:8, :] = xs_ref[LB:LB + 8, :]
    qk = acc * jax.nn.sigmoid(acc)
    q = qk[:, :GROUP]
    k = qk[:, GROUP:] * (HEAD_DIM ** -0.5)
    qb = q.astype(BF16)
    kb = k.astype(BF16)
    v = v_ref[...]

    r = lax.broadcasted_iota(jnp.int32, (LB, LB), 0)
    c = lax.broadcasted_iota(jnp.int32, (LB, LB), 1)
    causal = c <= r
    tri = jnp.where(causal, 1.0, 0.0).astype(BF16)
    tri_t = jnp.where(r <= c, 1.0, 0.0).astype(BF16)
    gcol = gc_ref[...]
    grow = gr_ref[...]
    bcum_col = _split_dot(gcol, tri, 3, x_is_lhs=False)
    bcum_row = _split_dot(grow, tri_t, 3)

    c_state = c_ref[...]
    n_state = n_ref[...]
    q_c = jnp.dot(qb, c_state.astype(BF16), preferred_element_type=F32)
    qn = qb.astype(F32) * n_state.astype(BF16).astype(F32)
    q_n = _split_dot(qn, _group_ones(HEAD_DIM), 2)

    lane = lax.broadcasted_iota(jnp.int32, (LB, GROUP), 1) // HEAD_DIM
    num = jnp.zeros((LB, GROUP), F32)
    den = jnp.zeros((LB, GROUP), F32)
    floor = jnp.zeros((LB, GROUP), F32)
    wg_full = jnp.zeros((LB, GROUP), F32)
    a_full = jnp.zeros((1, GROUP), F32)
    lane1 = lax.broadcasted_iota(jnp.int32, (1, GROUP), 1) // HEAD_DIM
    a_list = []
    for h in range(N_HEADS):
        i_col = gcol[:, h:h + 1]
        i_row = grow[h:h + 1, :]
        b_col = bcum_col[:, N_HEADS + h:N_HEADS + h + 1]
        b_row = bcum_row[N_HEADS + h:N_HEADS + h + 1, :]
        m_prev = m_ref[h:h + 1, 0:1]
        dmat = jnp.where(causal, b_col - b_row + i_row, NEG)
        inter = b_col + m_prev
        m_t = jnp.maximum(inter, jnp.max(dmat, axis=-1, keepdims=True))
        w_intra = jnp.exp(dmat - m_t)
        s_inter = jnp.exp(inter - m_t)
        qh = jnp.where(_head_mask(qb.shape, h), qb, jnp.zeros_like(qb))
        sc = lax.dot_general(qh, kb, (((1,), (1,)), ((), ())), preferred_element_type=F32) * w_intra
        vh = jnp.where(_head_mask(v.shape, h), v, jnp.zeros_like(v))
        num_h = s_inter * q_c + jnp.dot(sc.astype(BF16), vh, preferred_element_type=F32)
        den_h = s_inter * q_n + jnp.sum(sc, axis=-1, keepdims=True)
        sel = lane == h
        num = jnp.where(sel, num_h, num)
        den = jnp.where(sel, den_h, den)
        floor = jnp.where(sel, jnp.exp(-m_t), floor)
        b_tot = b_col[LB - 1:LB, :]
        g = b_tot - b_col + i_col
        m_new = jnp.maximum(b_tot + m_prev, jnp.max(g, axis=0, keepdims=True))
        a_h = jnp.exp(b_tot + m_prev - m_new)
        a_list.append(a_h)
        a_full = jnp.where(lane1 == h, a_h, a_full)
        wg_full = jnp.where(sel, jnp.exp(g - m_new), wg_full)
        m_ref[h:h + 1, :] = jnp.broadcast_to(m_new, (1, m_ref.shape[1]))

    hb = num / jnp.maximum(jnp.abs(den), floor)
    o_ref[...] = (_group_rms(hb, on_ref[...], HEAD_DIM) * og_ref[...]).astype(BF16)

    kw = k * wg_full
    kv = jnp.dot(kw.T.astype(BF16), v, preferred_element_type=F32)
    r2 = lax.broadcasted_iota(jnp.int32, (GROUP, GROUP), 0) // HEAD_DIM
    c2 = lax.broadcasted_iota(jnp.int32, (GROUP, GROUP), 1) // HEAD_DIM
    a_mat = jnp.zeros((GROUP, GROUP), F32)
    for h in range(N_HEADS):
        a_mat = jnp.where(r2 == h, a_list[h], a_mat)
    c_ref[...] = a_mat * c_state + jnp.where(r2 == c2, kv, 0.0)
    n_ref[...] = a_full * n_state + jnp.sum(kw, axis=0, keepdims=True)


def _mix_b(bqk, bv, bo, gcol, grow, conv_w, conv_b, out_norm, nb, s):
    nc = s // LB
    row = lambda w: pl.BlockSpec((LB, w), lambda b, c: (b * nc + c, 0))
    return pl.pallas_call(
        _mix_b_kernel,
        grid=(nb, nc),
        in_specs=[row(2 * GROUP), row(GROUP), row(GROUP), row(GATE_LANES),
                  pl.BlockSpec((GATE_ROWS, LB), lambda b, c: (0, b * nc + c)),
                  _const_spec((CONV_WIDTH, 2 * GROUP)), _const_spec((1, 2 * GROUP)),
                  _const_spec((1, GROUP))],
        out_specs=row(GROUP),
        out_shape=jax.ShapeDtypeStruct((nb * s, GROUP), BF16),
        scratch_shapes=[pltpu.VMEM((LB + 8, 2 * GROUP), F32),
                        pltpu.VMEM((GROUP, GROUP), F32),
                        pltpu.VMEM((1, GROUP), F32),
                        pltpu.VMEM((8, 128), F32)],
        compiler_params=_cparams(("parallel", "arbitrary")),
        name="mix_b_mlstm",
    )(bqk, bv, bo, gcol, grow, conv_w, conv_b, out_norm)


def _mix_c_kernel(lam_ref, qt_ref, k_ref, vt_ref, on_ref, o_ref, qm_ref, p_ref, acc_ref, m_ref, al_ref, *,
                  lam_init):
    qt = pl.program_id(1)
    q_t = qt_ref[0]
    n_maps = 2 * N_HEADS
    comp_row = lax.broadcasted_iota(jnp.int32, q_t.shape, 0) // DIFF_QK_DIM
    for j in range(n_maps):
        qm_ref[j] = jnp.where(comp_row == j, q_t, jnp.zeros_like(q_t))
    acc_ref[...] = jnp.zeros(acc_ref.shape, F32)

    def scores(kt):
        start = pl.multiple_of(kt * TK_C, TK_C)
        k = k_ref[pl.ds(start, TK_C), :]
        return [jnp.dot(k, qm_ref[j], preferred_element_type=F32) for j in range(n_maps)]

    def softmax(sc, slot, masked=False):
        if masked:
            key_chunk = lax.broadcasted_iota(jnp.int32, (TK_C, TQ_C), 0) // CHUNK
            q_chunk = lax.broadcasted_iota(jnp.int32, (TK_C, TQ_C), 1) // CHUNK
            vis = key_chunk <= q_chunk
        for j in range(n_maps):
            s = jnp.where(vis, sc[j], NEG) if masked else sc[j]
            m_old = m_ref[j]
            m_new = jnp.maximum(m_old, jnp.max(s, axis=0, keepdims=True))
            al_ref[j] = jnp.exp2(m_old - m_new)
            m_ref[j] = m_new
            p_ref[slot, j] = jnp.exp2(s - m_new).astype(BF16)

    def pv_update(kt, slot):
        v_ext = vt_ref[kt]
        for j in range(n_maps):
            h = j // 2
            pv = jnp.dot(v_ext[h * VROWS:(h + 1) * VROWS, :], p_ref[slot, j], preferred_element_type=F32)
            acc_ref[j] = al_ref[j] * acc_ref[j] + pv

    def step(kt, prev, slot):
        sc = scores(kt)
        pv_update(prev, slot)
        softmax(sc, 1 - slot)

    m_ref[...] = jnp.full(m_ref.shape, NEG, F32)
    softmax(scores(qt), 0, masked=True)
    n_pairs = qt // 2

    def body(ip, carry):
        a = 2 * ip
        step(a, jnp.where(ip == 0, qt, a - 1), 0)
        step(a + 1, a, 1)
        return carry

    lax.fori_loop(0, n_pairs, body, 0)
    last = jnp.where(n_pairs > 0, 2 * n_pairs - 1, qt)
    odd = qt - 2 * n_pairs

    @pl.when(odd == 1)
    def _():
        step(qt - 1, last, 0)
        pv_update(qt - 1, 1)

    @pl.when(odd == 0)
    def _():
        pv_update(last, 0)

    lam = lam_ref[0]
    heads = []
    for h in range(N_HEADS):
        a0 = acc_ref[2 * h]
        a1 = acc_ref[2 * h + 1]
        o_h = (a0[:HEAD_DIM] * (1.0 / a0[HEAD_DIM:HEAD_DIM + 1])
               - lam * (a1[:HEAD_DIM] * (1.0 / a1[HEAD_DIM:HEAD_DIM + 1])))
        ms = jnp.mean(o_h * o_h, axis=0, keepdims=True)
        heads.append(o_h * lax.rsqrt(ms + EPS))
    out_t = jnp.concatenate(heads, axis=0) * (on_ref[...] * (1.0 - lam_init))
    o_ref[...] = out_t.T.astype(BF16)


def _mix_c(lam, q_t, k, v_t, out_norm_col, nb, s, lam_init):
    nt = s // TQ_C
    tile_t = pl.BlockSpec((1, GROUP, TQ_C), lambda b, t, lam_ref: (b * nt + t, 0, 0))
    full_k = pl.BlockSpec((s, GROUP), lambda b, t, lam_ref: (b, 0), pipeline_mode=pl.Buffered(1))
    full_vt = pl.BlockSpec((nt, N_HEADS * VROWS, TK_C), lambda b, t, lam_ref: (b, 0, 0),
                           pipeline_mode=pl.Buffered(1))
    grid_spec = pltpu.PrefetchScalarGridSpec(
        num_scalar_prefetch=1,
        grid=(nb, nt),
        in_specs=[tile_t, full_k, full_vt,
                  pl.BlockSpec((GROUP, 1), lambda b, t, lam_ref: (0, 0), pipeline_mode=pl.Buffered(1))],
        out_specs=pl.BlockSpec((TQ_C, GROUP), lambda b, t, lam_ref: (b * nt + t, 0)),
        scratch_shapes=[pltpu.VMEM((2 * N_HEADS, GROUP, TQ_C), BF16),
                        pltpu.VMEM((2, 2 * N_HEADS, TK_C, TQ_C), BF16),
                        pltpu.VMEM((2 * N_HEADS, VROWS, TQ_C), F32),
                        pltpu.VMEM((2 * N_HEADS, 1, TQ_C), F32),
                        pltpu.VMEM((2 * N_HEADS, 1, TQ_C), F32)],
    )
    return pl.pallas_call(
        functools.partial(_mix_c_kernel, lam_init=lam_init),
        grid_spec=grid_spec,
        out_shape=jax.ShapeDtypeStruct((nb * s, GROUP), BF16),
        compiler_params=_cparams(("parallel", "parallel")),
        name="mix_c_diff_attn",
    )(lam, q_t, k, v_t, out_norm_col)


def _mix_d_kernel(qt_ref, k_ref, vt_ref, o_ref, qm_ref, acc_ref, run_ref):
    qt = pl.program_id(1)
    q_t = qt_ref[0]
    head_row = lax.broadcasted_iota(jnp.int32, q_t.shape, 0) // HEAD_DIM
    for h in range(N_HEADS):
        qm_ref[h] = jnp.where(head_row == h, q_t, jnp.zeros_like(q_t))
    acc_ref[...] = jnp.zeros(acc_ref.shape, F32)
    run_ref[...] = jnp.zeros(run_ref.shape, F32)
    r = lax.broadcasted_iota(jnp.int32, (TK_D, TK_D), 0)
    c = lax.broadcasted_iota(jnp.int32, (TK_D, TK_D), 1)
    later = jnp.where(c > r, 1.0, 0.0).astype(BF16)

    def tile(kt, masked):
        start = pl.multiple_of(kt * TK_D, TK_D)
        k = k_ref[pl.ds(start, TK_D), :]
        v_t = vt_ref[kt]
        zs = [jnp.dot(k, qm_ref[h], preferred_element_type=F32) for h in range(N_HEADS)]
        if masked:
            before = (lax.broadcasted_iota(jnp.int32, (TK_D, TQ_D), 0)
                      < lax.broadcasted_iota(jnp.int32, (TK_D, TQ_D), 1))
        worst = None
        log_take, between = [], []
        for h in range(N_HEADS):
            z = zs[h]
            log_keep = jnp.minimum(-z, 0.0) - jnp.log2(1.0 + jnp.exp2(-jnp.abs(z)))
            log_take.append(z + log_keep)
            if masked:
                log_keep = jnp.where(before, log_keep, 0.0)
            run = run_ref[h]
            between.append(run + _split_dot(log_keep, later, 2, x_is_lhs=False))
            run = run + jnp.sum(log_keep, axis=0, keepdims=True)
            run_ref[h] = run
            top = jnp.max(run)
            worst = top if worst is None else jnp.maximum(worst, top)
        for h in range(N_HEADS):
            a = jnp.exp2(log_take[h] + between[h])
            if masked:
                a = jnp.where(before, a, 0.0)
            acc_ref[h] += jnp.dot(v_t[h * HEAD_DIM:(h + 1) * HEAD_DIM, :], a.astype(BF16),
                                  preferred_element_type=F32)
        return worst

    worst = tile(qt, True)

    def cond(state):
        kt, worst = state
        return jnp.logical_and(kt >= 0, worst > SB_SKIP_LOG2)

    def body(state):
        kt, _ = state
        return kt - 1, tile(kt, False)

    lax.while_loop(cond, body, (qt - 1, worst))
    out_t = jnp.concatenate([acc_ref[h] for h in range(N_HEADS)], axis=0)
    o_ref[...] = out_t.T.astype(BF16)


def _mix_d(q_t, k, v_t, nb, s):
    nt = s // TQ_D
    tile_t = pl.BlockSpec((1, GROUP, TQ_D), lambda b, t: (b * nt + t, 0, 0))
    full_k = pl.BlockSpec((s, GROUP), lambda b, t: (b, 0), pipeline_mode=pl.Buffered(1))
    full_vt = pl.BlockSpec((nt, GROUP, TK_D), lambda b, t: (b, 0, 0), pipeline_mode=pl.Buffered(1))
    return pl.pallas_call(
        _mix_d_kernel,
        grid=(nb, nt),
        in_specs=[tile_t, full_k, full_vt],
        out_specs=pl.BlockSpec((TQ_D, GROUP), lambda b, t: (b * nt + t, 0)),
        out_shape=jax.ShapeDtypeStruct((nb * s, GROUP), BF16),
        scratch_shapes=[pltpu.VMEM((N_HEADS, GROUP, TQ_D), BF16),
                        pltpu.VMEM((N_HEADS, HEAD_DIM, TQ_D), F32),
                        pltpu.VMEM((N_HEADS, 1, TQ_D), F32)],
        compiler_params=_cparams(("parallel", "parallel")),
        name="mix_d_stick_breaking",
    )(q_t, k, v_t)


def _rel_bias_tile(rel_bias):
    nk = TQ_A + LEFT_CHUNKS * CHUNK
    i = np.arange(TQ_A)[:, None]
    j = np.arange(nk)[None, :]
    dc = j // CHUNK - i // CHUNK
    visible = (dc >= 0) & (dc <= LEFT_CHUNKS)
    p = TQ_A + nk
    d = np.arange(p)
    rel = d - (nk - 1) + LEFT_CHUNKS * CHUNK
    idx = np.clip(rel, -REL_CLIP, REL_CLIP) + REL_CLIP
    diag = rel_bias[:, idx].astype(F32)
    nh = rel_bias.shape[0]
    skew = jnp.tile(diag, (1, TQ_A + 1))[:, :TQ_A * (p + 1)].reshape(nh, TQ_A, p + 1)
    toeplitz = skew[:, :, :nk][:, :, ::-1]
    return jnp.where(jnp.asarray(visible)[None], toeplitz, NEG)


def _permute_w_in(w_in):
    sizes = [GROUP, GROUP, GROUP, 2 * GROUP, GROUP, GROUP, N_HEADS, N_HEADS,
             GROUP, GROUP, GROUP, GROUP, GROUP, GROUP]
    offs = np.concatenate([[0], np.cumsum(sizes)])
    seg = [w_in[:, offs[n]:offs[n + 1]] for n in range(len(sizes))]
    main = jnp.concatenate(seg[0:6] + [seg[9], seg[12]], axis=1).astype(BF16)
    c_t = jnp.concatenate([seg[8].T, seg[10].T, seg[11].T, seg[13].T], axis=0).astype(BF16)
    gates = jnp.concatenate([seg[6], seg[7]], axis=1)
    g_col = jnp.pad(gates, ((0, 0), (0, GATE_LANES - 2 * N_HEADS))).astype(BF16)
    g_row = jnp.pad(gates.T, ((0, GATE_ROWS - 2 * N_HEADS), (0, 0))).astype(BF16)
    return main, c_t, g_col, g_row


def _layer(x, nb, s, lam_init, p):
    x = _ffn(x, p["ffn1_norm"], p["ffn1_wg"].astype(BF16), p["ffn1_wu"].astype(BF16), p["ffn1_wd"].astype(BF16))

    w_main, w_ct, w_gc, w_gr = _permute_w_in(p["w_in"])
    gate_bias = p["b_gate_bias"].reshape(2 * N_HEADS).astype(F32)
    gb_col = jnp.pad(gate_bias, (0, GATE_LANES - 2 * N_HEADS)).reshape(1, GATE_LANES)
    gb_row = jnp.pad(gate_bias, (0, GATE_ROWS - 2 * N_HEADS)).reshape(GATE_ROWS, 1)
    tile4 = lambda g: jnp.tile(g.astype(F32), GROUP // g.shape[0]).reshape(1, GROUP)
    (aq, ak, av, bqk, bv, bo, gcol, grow, cq_t, ck, cv_t, dq_t, dk, dv_t) = _proj(
        x, p["mix_norm"], w_main, w_ct, w_gc, w_gr, gb_col, gb_row,
        tile4(p["a_q_norm"]), tile4(p["a_k_norm"]), tile4(p["c_q_norm"]).reshape(GROUP, 1), tile4(p["c_k_norm"]))

    ya = _mix_a(aq, ak, av, _rel_bias_tile(p["a_rel_bias"]), nb, s)
    yb = _mix_b(bqk, bv, bo, gcol, grow, p["b_conv_w"].astype(F32), p["b_conv_b"].astype(F32).reshape(1, -1),
                p["b_out_norm"].astype(F32).reshape(1, GROUP), nb, s)
    lv = p["c_lambda"].astype(F32)
    lam = jnp.exp(jnp.sum(lv[0] * lv[1])) - jnp.exp(jnp.sum(lv[2] * lv[3])) + lam_init
    yc = _mix_c(lam.reshape(1), cq_t, ck, cv_t, tile4(p["c_out_norm"]).reshape(GROUP, 1), nb, s, lam_init)
    yd = _mix_d(dq_t, dk, dv_t, nb, s)

    return _mix_out_ffn(x, ya, yb, yc, yd, p["w_out"].astype(BF16), p["ffn2_norm"], p["ffn2_wg"].astype(BF16),
                        p["ffn2_wu"].astype(BF16), p["ffn2_wd"].astype(BF16))


_PARAM_NAMES = ("ffn1_norm", "ffn1_wg", "ffn1_wu", "ffn1_wd", "mix_norm", "w_in", "a_q_norm", "a_k_norm",
                "a_rel_bias", "b_conv_w", "b_conv_b", "b_gate_bias", "b_out_norm", "c_q_norm", "c_k_norm",
                "c_lambda", "c_out_norm", "w_out", "ffn2_norm", "ffn2_wg", "ffn2_wu", "ffn2_wd")


def kernel(x, ffn1_norm, ffn1_wg, ffn1_wu, ffn1_wd, mix_norm, w_in, a_q_norm, a_k_norm, a_rel_bias,
           b_conv_w, b_conv_b, b_gate_bias, b_out_norm, c_q_norm, c_k_norm, c_lambda, c_out_norm,
           w_out, ffn2_norm, ffn2_wg, ffn2_wu, ffn2_wd):
    params = dict(zip(_PARAM_NAMES, (ffn1_norm, ffn1_wg, ffn1_wu, ffn1_wd, mix_norm, w_in, a_q_norm, a_k_norm,
                                     a_rel_bias, b_conv_w, b_conv_b, b_gate_bias, b_out_norm, c_q_norm,
                                     c_k_norm, c_lambda, c_out_norm, w_out, ffn2_norm, ffn2_wg, ffn2_wu,
                                     ffn2_wd)))
    nb, s, d = x.shape
    depth = ffn1_norm.shape[0]
    h = x.reshape(nb * s, d)
    for l in range(depth):
        lam_init = 0.8 - 0.6 * math.exp(-0.3 * l)
        h = _layer(h, nb, s, lam_init, {k: v[l] for k, v in params.items()})
    return h.reshape(nb, s, d)
```

```python
import functools
import math

import jax
import jax.numpy as jnp
import numpy as np
from jax import lax
from jax.experimental import pallas as pl
from jax.experimental.pallas import tpu as pltpu

F32 = jnp.float32
BF16 = jnp.bfloat16

D_MODEL = 1024
D_FF = 2816
CHUNK = 64
HEAD_DIM = 64
N_HEADS = 4
GROUP = N_HEADS * HEAD_DIM
LEFT_CHUNKS = 8
REL_CLIP = 128
CONV_WIDTH = 4
DIFF_QK_DIM = HEAD_DIM // 2
EPS = 1e-6
NEG = -1e30
LOG2E = 1.4426950408889634

OFF_AK = 0
OFF_BQK, OFF_BV, OFF_BO = 256, 768, 1024
OFF_CK = 1280
OFF_DK = 1536
MAIN_COLS = 1792
N_T_GROUPS = 6
GATE_LANES = 128
GATE_ROWS = 16
ONES_ROWS = 16
VROWS = HEAD_DIM + ONES_ROWS

VMEM_LIMIT = 56 * 1024 * 1024

TM_FFN = 512
TM_PROJ = 512
TILE_T = 256
TQ_A = 256
LB = 256
TQ_C = 256
TK_C = 256
TQ_D = 256
TK_D = 256
SB_SKIP_LOG2 = -160.0


def _cparams(sem):
    return pltpu.CompilerParams(dimension_semantics=sem, vmem_limit_bytes=VMEM_LIMIT)


def _const_spec(shape):
    nd = len(shape)
    return pl.BlockSpec(shape, lambda *_: (0,) * nd, pipeline_mode=pl.Buffered(1))


def _split_dot(x, mat, terms, x_is_lhs=True):
    acc = None
    rem = x
    for term in range(terms):
        part = rem.astype(BF16)
        if term + 1 < terms:
            rem = rem - part.astype(F32)
        d = (jnp.dot(part, mat, preferred_element_type=F32) if x_is_lhs
             else jnp.dot(mat, part, preferred_element_type=F32))
        acc = d if acc is None else acc + d
    return acc


def _group_ones(width):
    r = lax.broadcasted_iota(jnp.int32, (GROUP, GROUP), 0) // width
    c = lax.broadcasted_iota(jnp.int32, (GROUP, GROUP), 1) // width
    return jnp.where(r == c, 1.0, 0.0).astype(BF16)


def _group_rms(x, gain, width):
    ss = _split_dot(x * x, _group_ones(width), 2)
    return x * lax.rsqrt(ss * (1.0 / width) + EPS) * gain


def _log_sigmoid(x):
    return jnp.minimum(x, 0.0) - jnp.log1p(jnp.exp(-jnp.abs(x)))


def _head_mask(shape, h, axis=1, width=HEAD_DIM):
    lane = lax.broadcasted_iota(jnp.int32, shape, axis)
    return (lane // width) == h


def _ffn_half_step(x, g_ref, wg_ref, wu_ref, wd_ref):
    xn = x * lax.rsqrt(jnp.mean(x * x, axis=-1, keepdims=True) + EPS) * g_ref[...]
    xb = xn.astype(BF16)
    g = jnp.dot(xb, wg_ref[...], preferred_element_type=F32)
    u = jnp.dot(xb, wu_ref[...], preferred_element_type=F32)
    h = (g * jax.nn.sigmoid(g) * u).astype(BF16)
    return x + 0.5 * jnp.dot(h, wd_ref[...], preferred_element_type=F32)


def _ffn_kernel(x_ref, g_ref, wg_ref, wu_ref, wd_ref, o_ref):
    o_ref[...] = _ffn_half_step(x_ref[...], g_ref, wg_ref, wu_ref, wd_ref)


def _mix_out_ffn_kernel(x_ref, ya_ref, yb_ref, yc_ref, yd_ref, wo_ref, g_ref, wg_ref, wu_ref, wd_ref, o_ref):
    x = x_ref[...]
    for grp, y_ref in enumerate((ya_ref, yb_ref, yc_ref, yd_ref)):
        x = x + jnp.dot(y_ref[...], wo_ref[grp * GROUP:(grp + 1) * GROUP, :], preferred_element_type=F32)
    o_ref[...] = _ffn_half_step(x, g_ref, wg_ref, wu_ref, wd_ref)


def _ffn_specs():
    return [_const_spec((1, D_MODEL)), _const_spec((D_MODEL, D_FF)), _const_spec((D_MODEL, D_FF)),
            _const_spec((D_FF, D_MODEL))]


def _ffn(x, gain, wg, wu, wd):
    t = x.shape[0]
    tm = min(TM_FFN, t)
    row = pl.BlockSpec((tm, D_MODEL), lambda i: (i, 0))
    return pl.pallas_call(
        _ffn_kernel,
        grid=(t // tm,),
        in_specs=[row] + _ffn_specs(),
        out_specs=row,
        out_shape=jax.ShapeDtypeStruct((t, D_MODEL), F32),
        compiler_params=_cparams(("parallel",)),
        name="ffn_half_step",
    )(x, gain.reshape(1, D_MODEL), wg, wu, wd)


def _mix_out_ffn(x, ya, yb, yc, yd, w_out, gain, wg, wu, wd):
    t = x.shape[0]
    tm = min(TM_FFN, t)
    row = lambda w: pl.BlockSpec((tm, w), lambda i: (i, 0))
    return pl.pallas_call(
        _mix_out_ffn_kernel,
        grid=(t // tm,),
        in_specs=[row(D_MODEL), row(GROUP), row(GROUP), row(GROUP), row(GROUP),
                  _const_spec((D_MODEL, D_MODEL))] + _ffn_specs(),
        out_specs=row(D_MODEL),
        out_shape=jax.ShapeDtypeStruct((t, D_MODEL), F32),
        compiler_params=_cparams(("parallel",)),
        name="mix_out_proj_ffn",
    )(x, ya, yb, yc, yd, w_out, gain.reshape(1, D_MODEL), wg, wu, wd)


def _proj_kernel(x_ref, g_ref, w_ref, wct_ref, wgc_ref, wgr_ref, gbc_ref, gbr_ref,
                 aqn_ref, akn_ref, cqn_ref, ckn_ref,
                 aq_ref, ak_ref, av_ref, bqk_ref, bv_ref, bo_ref, gc_ref, gr_ref,
                 cq_ref, ck_ref, cv_ref, dq_ref, dk_ref, dv_ref):
    x = x_ref[...]
    hn = x * lax.rsqrt(jnp.mean(x * x, axis=-1, keepdims=True) + EPS) * g_ref[...]
    hb = hn.astype(BF16)

    def cols(off, width=GROUP):
        return jnp.dot(hb, w_ref[:, off:off + width], preferred_element_type=F32)

    nt_dims = (((1,), (1,)), ((), ()))

    def rows_t(idx):
        return lax.dot_general(wct_ref[idx * GROUP:(idx + 1) * GROUP, :], hb, nt_dims, preferred_element_type=F32)

    def group_rms_t(x_t, gain_col, width):
        ss = _split_dot(x_t * x_t, _group_ones(width), 2, x_is_lhs=False)
        return x_t * lax.rsqrt(ss * (1.0 / width) + EPS) * gain_col

    aqt = (group_rms_t(rows_t(4), aqn_ref[...], HEAD_DIM) * (HEAD_DIM ** -0.5 * LOG2E)).astype(BF16)
    ak_ref[...] = _group_rms(cols(OFF_AK), akn_ref[...], HEAD_DIM).astype(BF16)
    avt = rows_t(5).astype(BF16)
    bqk_ref[...] = cols(OFF_BQK, 2 * GROUP)
    bv_ref[...] = cols(OFF_BV).astype(BF16)
    bo_ref[...] = jax.nn.sigmoid(cols(OFF_BO))
    gcol = jnp.dot(hb, wgc_ref[...], preferred_element_type=F32) + gbc_ref[...]
    lane = lax.broadcasted_iota(jnp.int32, gcol.shape, 1)
    gc_ref[...] = jnp.where(lane >= N_HEADS, _log_sigmoid(gcol), gcol)
    grow = lax.dot_general(wgr_ref[...], hb, (((1,), (1,)), ((), ())),
                           preferred_element_type=F32) + gbr_ref[...]
    row = lax.broadcasted_iota(jnp.int32, grow.shape, 0)
    gr_ref[...] = jnp.where(row >= N_HEADS, _log_sigmoid(grow), grow)
    ck_ref[...] = _group_rms(cols(OFF_CK), ckn_ref[...], DIFF_QK_DIM).astype(BF16)
    cqt = (group_rms_t(rows_t(0), cqn_ref[...], DIFF_QK_DIM) * (DIFF_QK_DIM ** -0.5 * LOG2E)).astype(BF16)
    cvt = rows_t(1).astype(BF16)
    dk_ref[...] = cols(OFF_DK).astype(BF16)
    dqt = (rows_t(2) * (HEAD_DIM ** -0.5 * LOG2E)).astype(BF16)
    dvt = rows_t(3).astype(BF16)
    for sub in range(cq_ref.shape[0]):
        tok = slice(sub * TILE_T, (sub + 1) * TILE_T)
        aq_ref[sub] = aqt[:, tok]
        av_ref[sub] = avt[:, tok]
        cq_ref[sub] = cqt[:, tok]
        dq_ref[sub] = dqt[:, tok]
        dv_ref[sub] = dvt[:, tok]
        for h in range(N_HEADS):
            cv_ref[sub, h * VROWS:h * VROWS + HEAD_DIM, :] = cvt[h * HEAD_DIM:(h + 1) * HEAD_DIM, tok]
            cv_ref[sub, h * VROWS + HEAD_DIM:(h + 1) * VROWS, :] = jnp.ones((ONES_ROWS, TILE_T), BF16)


def _proj(x, gain, w_main, w_ct, w_gc, w_gr, gb_col, gb_row, aqn, akn, cqn_col, ckn):
    t = x.shape[0]
    tm = TM_PROJ
    row_spec = lambda w: pl.BlockSpec((tm, w), lambda i: (i, 0))
    bf = lambda w: jax.ShapeDtypeStruct((t, w), BF16)
    f32 = lambda w: jax.ShapeDtypeStruct((t, w), F32)
    n_sub = tm // TILE_T
    tr_shape = jax.ShapeDtypeStruct((t // TILE_T, GROUP, TILE_T), BF16)
    tr_spec = pl.BlockSpec((n_sub, GROUP, TILE_T), lambda i: (i, 0, 0))
    vt_shape = jax.ShapeDtypeStruct((t // TILE_T, N_HEADS * VROWS, TILE_T), BF16)
    vt_spec = pl.BlockSpec((n_sub, N_HEADS * VROWS, TILE_T), lambda i: (i, 0, 0))
    out_shape = [tr_shape, bf(GROUP), tr_shape,
                 f32(2 * GROUP), bf(GROUP), f32(GROUP),
                 f32(GATE_LANES), jax.ShapeDtypeStruct((GATE_ROWS, t), F32),
                 tr_shape, bf(GROUP), vt_shape,
                 tr_shape, bf(GROUP), tr_shape]
    out_specs = ([tr_spec, row_spec(GROUP), tr_spec, row_spec(2 * GROUP), row_spec(GROUP), row_spec(GROUP),
                  row_spec(GATE_LANES), pl.BlockSpec((GATE_ROWS, tm), lambda i: (0, i))]
                 + [tr_spec, row_spec(GROUP), vt_spec, tr_spec, row_spec(GROUP), tr_spec])
    return pl.pallas_call(
        _proj_kernel,
        grid=(t // tm,),
        in_specs=[row_spec(D_MODEL), _const_spec((1, D_MODEL)),
                  _const_spec((D_MODEL, MAIN_COLS)), _const_spec((N_T_GROUPS * GROUP, D_MODEL)),
                  _const_spec((D_MODEL, GATE_LANES)),
                  _const_spec((GATE_ROWS, D_MODEL)), _const_spec((1, GATE_LANES)),
                  _const_spec((GATE_ROWS, 1)),
                  _const_spec((GROUP, 1)), _const_spec((1, GROUP)),
                  _const_spec((GROUP, 1)), _const_spec((1, GROUP))],
        out_specs=out_specs,
        out_shape=out_shape,
        compiler_params=_cparams(("parallel",)),
        name="mix_in_proj",
    )(x, gain.reshape(1, D_MODEL), w_main, w_ct, w_gc, w_gr, gb_col, gb_row, aqn, akn, cqn_col, ckn)


def _mix_a_kernel(qt_ref, k0_ref, k1_ref, k2_ref, v0_ref, v1_ref, v2_ref, bias_ref, o_ref):
    t = pl.program_id(1)
    q_t = qt_ref[0]
    k = jnp.concatenate([k0_ref[...], k1_ref[...], k2_ref[...]], axis=0)
    v_t = jnp.concatenate([v0_ref[0], v1_ref[0], v2_ref[0]], axis=1)
    nk = k.shape[0]
    head_row = lax.broadcasted_iota(jnp.int32, q_t.shape, 0) // HEAD_DIM

    def attend(mask_start):
        scores = [jnp.dot(k, jnp.where(head_row == h, q_t, jnp.zeros_like(q_t)), preferred_element_type=F32)
                  for h in range(N_HEADS)]
        if mask_start:
            key_pos = t * TQ_A - LEFT_CHUNKS * CHUNK + lax.broadcasted_iota(jnp.int32, (nk, TQ_A), 0)
            valid = key_pos >= 0
        probs = []
        for h in range(N_HEADS):
            s = scores[h] + bias_ref[h]
            if mask_start:
                s = jnp.where(valid, s, NEG)
            e = jnp.exp2(s - jnp.max(s, axis=0, keepdims=True))
            probs.append((e * (1.0 / jnp.sum(e, axis=0, keepdims=True))).astype(BF16))
        heads = [jnp.dot(v_t[h * HEAD_DIM:(h + 1) * HEAD_DIM, :], probs[h], preferred_element_type=F32)
                 for h in range(N_HEADS)]
        o_ref[...] = jnp.concatenate(heads, axis=0).T.astype(BF16)

    n_start_tiles = LEFT_CHUNKS * CHUNK // TQ_A

    @pl.when(t < n_start_tiles)
    def _():
        attend(True)

    @pl.when(t >= n_start_tiles)
    def _():
        attend(False)


def _mix_a(q_t, k, v_t, bias_tile_t, nb, s):
    nt = s // TQ_A
    back_idx = lambda b, t, back: b * nt + jnp.maximum(t - back, 0)
    rows = lambda back: pl.BlockSpec((TQ_A, GROUP), lambda b, t: (back_idx(b, t, back), 0))
    tr = lambda back: pl.BlockSpec((1, GROUP, TQ_A), lambda b, t: (back_idx(b, t, back), 0, 0))
    return pl.pallas_call(
        _mix_a_kernel,
        grid=(nb, nt),
        in_specs=[tr(0), rows(2), rows(1), rows(0), tr(2), tr(1), tr(0), _const_spec(bias_tile_t.shape)],
        out_specs=rows(0),
        out_shape=jax.ShapeDtypeStruct((nb * s, GROUP), BF16),
        compiler_params=_cparams(("parallel", "parallel")),
        name="mix_a_chunk_attn",
    )(q_t, k, k, k, v_t, v_t, v_t, bias_tile_t)


def _mix_b_kernel(qk_ref, v_ref, og_ref, gc_ref, gr_ref, cw_ref, cb_ref, on_ref, o_ref,
                  xs_ref, c_ref, n_ref, m_ref):
    c_idx = pl.program_id(1)

    @pl.when(c_idx == 0)
    def _():
        xs_ref[0:8, :] = jnp.zeros((8, 2 * GROUP), F32)
        c_ref[...] = jnp.zeros_like(c_ref)
        n_ref[...] = jnp.zeros_like(n_ref)
        m_ref[...] = jnp.zeros_like(m_ref)

    xs_ref[8:8 + LB, :] = qk_ref[...]
    acc = jnp.broadcast_to(cb_ref[...], (LB, 2 * GROUP))
    for j in range(CONV_WIDTH):
        start = 8 - (CONV_WIDTH - 1) + j
        acc = acc + xs_ref[start:start + LB, :] * cw_ref[j:j + 1, :]
    xs_ref[0:8, :] = xs_ref[LB:LB + 8, :]
    qk = acc * jax.nn.sigmoid(acc)
    q = qk[:, :GROUP]
    k = qk[:, GROUP:] * (HEAD_DIM ** -0.5)
    qb = q.astype(BF16)
    kb = k.astype(BF16)
    v = v_ref[...]

    r = lax.broadcasted_iota(jnp.int32, (LB, LB), 0)
    c = lax.broadcasted_iota(jnp.int32, (LB, LB), 1)
    causal = c <= r
    tri = jnp.where(causal, 1.0, 0.0).astype(BF16)
    tri_t = jnp.where(r <= c, 1.0, 0.0).astype(BF16)
    gcol = gc_ref[...]
    grow = gr_ref[...]
    bcum_col = _split_dot(gcol, tri, 3, x_is_lhs=False)
    bcum_row = _split_dot(grow, tri_t, 3)

    c_state = c_ref[...]
    n_state = n_ref[...]
    q_c = jnp.dot(qb, c_state.astype(BF16), preferred_element_type=F32)
    qn = qb.astype(F32) * n_state.astype(BF16).astype(F32)
    q_n = _split_dot(qn, _group_ones(HEAD_DIM), 2)

    lane = lax.broadcasted_iota(jnp.int32, (LB, GROUP), 1) // HEAD_DIM
    num = jnp.zeros((LB, GROUP), F32)
    den = jnp.zeros((LB, GROUP), F32)
    floor = jnp.zeros((LB, GROUP), F32)
    wg_full = jnp.zeros((LB, GROUP), F32)
    a_full = jnp.zeros((1, GROUP), F32)
    lane1 = lax.broadcasted_iota(jnp.int32, (1, GROUP), 1) // HEAD_DIM
    a_list = []
    for h in range(N_HEADS):
        i_col = gcol[:, h:h + 1]
        i_row = grow[h:h + 1, :]
        b_col = bcum_col[:, N_HEADS + h:N_HEADS + h + 1]
        b_row = bcum_row[N_HEADS + h:N_HEADS + h + 1, :]
        m_prev = m_ref[h:h + 1, 0:1]
        dmat = jnp.where(causal, b_col - b_row + i_row, NEG)
        inter = b_col + m_prev
        m_t = jnp.maximum(inter, jnp.max(dmat, axis=-1, keepdims=True))
        w_intra = jnp.exp(dmat - m_t)
        s_inter = jnp.exp(inter - m_t)
        qh = jnp.where(_head_mask(qb.shape, h), qb, jnp.zeros_like(qb))
        sc = lax.dot_general(qh, kb, (((1,), (1,)), ((), ())), preferred_element_type=F32) * w_intra
        vh = jnp.where(_head_mask(v.shape, h), v, jnp.zeros_like(v))
        num_h = s_inter * q_c + jnp.dot(sc.astype(BF16), vh, preferred_element_type=F32)
        den_h = s_inter * q_n + jnp.sum(sc, axis=-1, keepdims=True)
        sel = lane == h
        num = jnp.where(sel, num_h, num)
        den = jnp.where(sel, den_h, den)
        floor = jnp.where(sel, jnp.exp(-m_t), floor)
        b_tot = b_col[LB - 1:LB, :]
        g = b_tot - b_col + i_col
        m_new = jnp.maximum(b_tot + m_prev, jnp.max(g, axis=0, keepdims=True))
        a_h = jnp.exp(b_tot + m_prev - m_new)
        a_list.append(a_h)
        a_full = jnp.where(lane1 == h, a_h, a_full)
        wg_full = jnp.where(sel, jnp.exp(g - m_new), wg_full)
        m_ref[h:h + 1, :] = jnp.broadcast_to(m_new, (1, m_ref.shape[1]))

    hb = num / jnp.maximum(jnp.abs(den), floor)
    o_ref[...] = (_group_rms(hb, on_ref[...], HEAD_DIM) * og_ref[...]).astype(BF16)

    kw = k * wg_full
    kv = jnp.dot(kw.T.astype(BF16), v, preferred_element_type=F32)
    r2 = lax.broadcasted_iota(jnp.int32, (GROUP, GROUP), 0) // HEAD_DIM
    c2 = lax.broadcasted_iota(jnp.int32, (GROUP, GROUP), 1) // HEAD_DIM
    a_mat = jnp.zeros((GROUP, GROUP), F32)
    for h in range(N_HEADS):
        a_mat = jnp.where(r2 == h, a_list[h], a_mat)
    c_ref[...] = a_mat * c_state + jnp.where(r2 == c2, kv, 0.0)
    n_ref[...] = a_full * n_state + jnp.sum(kw, axis=0, keepdims=True)


def _mix_b(bqk, bv, bo, gcol, grow, conv_w, conv_b, out_norm, nb, s):
    nc = s // LB
    row = lambda w: pl.BlockSpec((LB, w), lambda b, c: (b * nc + c, 0))
    return pl.pallas_call(
        _mix_b_kernel,
        grid=(nb, nc),
        in_specs=[row(2 * GROUP), row(GROUP), row(GROUP), row(GATE_LANES),
                  pl.BlockSpec((GATE_ROWS, LB), lambda b, c: (0, b * nc + c)),
                  _const_spec((CONV_WIDTH, 2 * GROUP)), _const_spec((1, 2 * GROUP)),
                  _const_spec((1, GROUP))],
        out_specs=row(GROUP),
        out_shape=jax.ShapeDtypeStruct((nb * s, GROUP), BF16),
        scratch_shapes=[pltpu.VMEM((LB + 8, 2 * GROUP), F32),
                        pltpu.VMEM((GROUP, GROUP), F32),
                        pltpu.VMEM((1, GROUP), F32),
                        pltpu.VMEM((8, 128), F32)],
        compiler_params=_cparams(("parallel", "arbitrary")),
        name="mix_b_mlstm",
    )(bqk, bv, bo, gcol, grow, conv_w, conv_b, out_norm)


def _mix_c_kernel(lam_ref, qt_ref, k_ref, vt_ref, on_ref, o_ref, qm_ref, p_ref, acc_ref, m_ref, al_ref, *,
                  lam_init):
    qt = pl.program_id(1)
    q_t = qt_ref[0]
    n_maps = 2 * N_HEADS
    comp_row = lax.broadcasted_iota(jnp.int32, q_t.shape, 0) // DIFF_QK_DIM
    for j in range(n_maps):
        qm_ref[j] = jnp.where(comp_row == j, q_t, jnp.zeros_like(q_t))
    acc_ref[...] = jnp.zeros(acc_ref.shape, F32)

    def scores(kt):
        start = pl.multiple_of(kt * TK_C, TK_C)
        k = k_ref[pl.ds(start, TK_C), :]
        return [jnp.dot(k, qm_ref[j], preferred_element_type=F32) for j in range(n_maps)]

    def softmax(sc, slot, masked=False):
        if masked:
            key_chunk = lax.broadcasted_iota(jnp.int32, (TK_C, TQ_C), 0) // CHUNK
            q_chunk = lax.broadcasted_iota(jnp.int32, (TK_C, TQ_C), 1) // CHUNK
            vis = key_chunk <= q_chunk
        for j in range(n_maps):
            s = jnp.where(vis, sc[j], NEG) if masked else sc[j]
            m_old = m_ref[j]
            m_new = jnp.maximum(m_old, jnp.max(s, axis=0, keepdims=True))
            al_ref[j] = jnp.exp2(m_old - m_new)
            m_ref[j] = m_new
            p_ref[slot, j] = jnp.exp2(s - m_new).astype(BF16)

    def pv_update(kt, slot):
        v_ext = vt_ref[kt]
        for j in range(n_maps):
            h = j // 2
            pv = jnp.dot(v_ext[h * VROWS:(h + 1) * VROWS, :], p_ref[slot, j], preferred_element_type=F32)
            acc_ref[j] = al_ref[j] * acc_ref[j] + pv

    def step(kt, prev, slot):
        sc = scores(kt)
        pv_update(prev, slot)
        softmax(sc, 1 - slot)

    m_ref[...] = jnp.full(m_ref.shape, NEG, F32)
    softmax(scores(qt), 0, masked=True)
    n_pairs = qt // 2

    def body(ip, carry):
        a = 2 * ip
        step(a, jnp.where(ip == 0, qt, a - 1), 0)
        step(a + 1, a, 1)
        return carry

    lax.fori_loop(0, n_pairs, body, 0)
    last = jnp.where(n_pairs > 0, 2 * n_pairs - 1, qt)
    odd = qt - 2 * n_pairs

    @pl.when(odd == 1)
    def _():
        step(qt - 1, last, 0)
        pv_update(qt - 1, 1)

    @pl.when(odd == 0)
    def _():
        pv_update(last, 0)

    lam = lam_ref[0]
    heads = []
    for h in range(N_HEADS):
        a0 = acc_ref[2 * h]
        a1 = acc_ref[2 * h + 1]
        o_h = (a0[:HEAD_DIM] * (1.0 / a0[HEAD_DIM:HEAD_DIM + 1])
               - lam * (a1[:HEAD_DIM] * (1.0 / a1[HEAD_DIM:HEAD_DIM + 1])))
        ms = jnp.mean(o_h * o_h, axis=0, keepdims=True)
        heads.append(o_h * lax.rsqrt(ms + EPS))
    out_t = jnp.concatenate(heads, axis=0) * (on_ref[...] * (1.0 - lam_init))
    o_ref[...] = out_t.T.astype(BF16)


def _mix_c(lam, q_t, k, v_t, out_norm_col, nb, s, lam_init):
    nt = s // TQ_C
    tile_t = pl.BlockSpec((1, GROUP, TQ_C), lambda b, t, lam_ref: (b * nt + t, 0, 0))
    full_k = pl.BlockSpec((s, GROUP), lambda b, t, lam_ref: (b, 0), pipeline_mode=pl.Buffered(1))
    full_vt = pl.BlockSpec((nt, N_HEADS * VROWS, TK_C), lambda b, t, lam_ref: (b, 0, 0),
                           pipeline_mode=pl.Buffered(1))
    grid_spec = pltpu.PrefetchScalarGridSpec(
        num_scalar_prefetch=1,
        grid=(nb, nt),
        in_specs=[tile_t, full_k, full_vt,
                  pl.BlockSpec((GROUP, 1), lambda b, t, lam_ref: (0, 0), pipeline_mode=pl.Buffered(1))],
        out_specs=pl.BlockSpec((TQ_C, GROUP), lambda b, t, lam_ref: (b * nt + t, 0)),
        scratch_shapes=[pltpu.VMEM((2 * N_HEADS, GROUP, TQ_C), BF16),
                        pltpu.VMEM((2, 2 * N_HEADS, TK_C, TQ_C), BF16),
                        pltpu.VMEM((2 * N_HEADS, VROWS, TQ_C), F32),
                        pltpu.VMEM((2 * N_HEADS, 1, TQ_C), F32),
                        pltpu.VMEM((2 * N_HEADS, 1, TQ_C), F32)],
    )
    return pl.pallas_call(
        functools.partial(_mix_c_kernel, lam_init=lam_init),
        grid_spec=grid_spec,
        out_shape=jax.ShapeDtypeStruct((nb * s, GROUP), BF16),
        compiler_params=_cparams(("parallel", "parallel")),
        name="mix_c_diff_attn",
    )(lam, q_t, k, v_t, out_norm_col)


def _mix_d_kernel(qt_ref, k_ref, vt_ref, o_ref, qm_ref, acc_ref, run_ref):
    qt = pl.program_id(1)
    q_t = qt_ref[0]
    head_row = lax.broadcasted_iota(jnp.int32, q_t.shape, 0) // HEAD_DIM
    for h in range(N_HEADS):
        qm_ref[h] = jnp.where(head_row == h, q_t, jnp.zeros_like(q_t))
    acc_ref[...] = jnp.zeros(acc_ref.shape, F32)
    run_ref[...] = jnp.zeros(run_ref.shape, F32)
    r = lax.broadcasted_iota(jnp.int32, (TK_D, TK_D), 0)
    c = lax.broadcasted_iota(jnp.int32, (TK_D, TK_D), 1)
    later = jnp.where(c > r, 1.0, 0.0).astype(BF16)

    def tile(kt, masked):
        start = pl.multiple_of(kt * TK_D, TK_D)
        k = k_ref[pl.ds(start, TK_D), :]
        v_t = vt_ref[kt]
        zs = [jnp.dot(k, qm_ref[h], preferred_element_type=F32) for h in range(N_HEADS)]
        if masked:
            before = (lax.broadcasted_iota(jnp.int32, (TK_D, TQ_D), 0)
                      < lax.broadcasted_iota(jnp.int32, (TK_D, TQ_D), 1))
        worst = None
        log_take, between = [], []
        for h in range(N_HEADS):
            z = zs[h]
            log_keep = jnp.minimum(-z, 0.0) - jnp.log2(1.0 + jnp.exp2(-jnp.abs(z)))
            log_take.append(z + log_keep)
            if masked:
                log_keep = jnp.where(before, log_keep, 0.0)
            run = run_ref[h]
            between.append(run + _split_dot(log_keep, later, 2, x_is_lhs=False))
            run = run + jnp.sum(log_keep, axis=0, keepdims=True)
            run_ref[h] = run
            top = jnp.max(run)
            worst = top if worst is None else jnp.maximum(worst, top)
        for h in range(N_HEADS):
            a = jnp.exp2(log_take[h] + between[h])
            if masked:
                a = jnp.where(before, a, 0.0)
            acc_ref[h] += jnp.dot(v_t[h * HEAD_DIM:(h + 1) * HEAD_DIM, :], a.astype(BF16),
                                  preferred_element_type=F32)
        return worst

    worst = tile(qt, True)

    def cond(state):
        kt, worst = state
        return jnp.logical_and(kt >= 0, worst > SB_SKIP_LOG2)

    def body(state):
        kt, _ = state
        return kt - 1, tile(kt, False)

    lax.while_loop(cond, body, (qt - 1, worst))
    out_t = jnp.concatenate([acc_ref[h] for h in range(N_HEADS)], axis=0)
    o_ref[...] = out_t.T.astype(BF16)


def _mix_d(q_t, k, v_t, nb, s):
    nt = s // TQ_D
    tile_t = pl.BlockSpec((1, GROUP, TQ_D), lambda b, t: (b * nt + t, 0, 0))
    full_k = pl.BlockSpec((s, GROUP), lambda b, t: (b, 0), pipeline_mode=pl.Buffered(1))
    full_vt = pl.BlockSpec((nt, GROUP, TK_D), lambda b, t: (b, 0, 0), pipeline_mode=pl.Buffered(1))
    return pl.pallas_call(
        _mix_d_kernel,
        grid=(nb, nt),
        in_specs=[tile_t, full_k, full_vt],
        out_specs=pl.BlockSpec((TQ_D, GROUP), lambda b, t: (b * nt + t, 0)),
        out_shape=jax.ShapeDtypeStruct((nb * s, GROUP), BF16),
        scratch_shapes=[pltpu.VMEM((N_HEADS, GROUP, TQ_D), BF16),
                        pltpu.VMEM((N_HEADS, HEAD_DIM, TQ_D), F32),
                        pltpu.VMEM((N_HEADS, 1, TQ_D), F32)],
        compiler_params=_cparams(("parallel", "parallel")),
        name="mix_d_stick_breaking",
    )(q_t, k, v_t)


def _rel_bias_tile(rel_bias):
    nk = TQ_A + LEFT_CHUNKS * CHUNK
    i = np.arange(TQ_A)[:, None]
    j = np.arange(nk)[None, :]
    dc = j // CHUNK - i // CHUNK
    visible = (dc >= 0) & (dc <= LEFT_CHUNKS)
    p = TQ_A + nk
    d = np.arange(p)
    rel = d - (nk - 1) + LEFT_CHUNKS * CHUNK
    idx = np.clip(rel, -REL_CLIP, REL_CLIP) + REL_CLIP
    diag = rel_bias[:, idx].astype(F32)
    nh = rel_bias.shape[0]
    skew = jnp.tile(diag, (1, TQ_A + 1))[:, :TQ_A * (p + 1)].reshape(nh, TQ_A, p + 1)
    toeplitz = skew[:, :, :nk][:, :, ::-1]
    return jnp.swapaxes(jnp.where(jnp.asarray(visible)[None], toeplitz * LOG2E, NEG), 1, 2)


def _permute_w_in(w_in):
    sizes = [GROUP, GROUP, GROUP, 2 * GROUP, GROUP, GROUP, N_HEADS, N_HEADS,
             GROUP, GROUP, GROUP, GROUP, GROUP, GROUP]
    offs = np.concatenate([[0], np.cumsum(sizes)])
    seg = [w_in[:, offs[n]:offs[n + 1]] for n in range(len(sizes))]
    main = jnp.concatenate([seg[1], seg[3], seg[4], seg[5], seg[9], seg[12]], axis=1).astype(BF16)
    c_t = jnp.concatenate([seg[n].T for n in (8, 10, 11, 13, 0, 2)], axis=0).astype(BF16)
    gates = jnp.concatenate([seg[6], seg[7]], axis=1)
    g_col = jnp.pad(gates, ((0, 0), (0, GATE_LANES - 2 * N_HEADS))).astype(BF16)
    g_row = jnp.pad(gates.T, ((0, GATE_ROWS - 2 * N_HEADS), (0, 0))).astype(BF16)
    return main, c_t, g_col, g_row


def _layer(x, nb, s, lam_init, p):
    x = _ffn(x, p["ffn1_norm"], p["ffn1_wg"].astype(BF16), p["ffn1_wu"].astype(BF16), p["ffn1_wd"].astype(BF16))

    w_main, w_ct, w_gc, w_gr = _permute_w_in(p["w_in"])
    gate_bias = p["b_gate_bias"].reshape(2 * N_HEADS).astype(F32)
    gb_col = jnp.pad(gate_bias, (0, GATE_LANES - 2 * N_HEADS)).reshape(1, GATE_LANES)
    gb_row = jnp.pad(gate_bias, (0, GATE_ROWS - 2 * N_HEADS)).reshape(GATE_ROWS, 1)
    tile4 = lambda g: jnp.tile(g.astype(F32), GROUP // g.shape[0]).reshape(1, GROUP)
    (aq_t, ak, av_t, bqk, bv, bo, gcol, grow, cq_t, ck, cv_t, dq_t, dk, dv_t) = _proj(
        x, p["mix_norm"], w_main, w_ct, w_gc, w_gr, gb_col, gb_row,
        tile4(p["a_q_norm"]).reshape(GROUP, 1), tile4(p["a_k_norm"]),
        tile4(p["c_q_norm"]).reshape(GROUP, 1), tile4(p["c_k_norm"]))

    ya = _mix_a(aq_t, ak, av_t, _rel_bias_tile(p["a_rel_bias"]), nb, s)
    yb = _mix_b(bqk, bv, bo, gcol, grow, p["b_conv_w"].astype(F32), p["b_conv_b"].astype(F32).reshape(1, -1),
                p["b_out_norm"].astype(F32).reshape(1, GROUP), nb, s)
    lv = p["c_lambda"].astype(F32)
    lam = jnp.exp(jnp.sum(lv[0] * lv[1])) - jnp.exp(jnp.sum(lv[2] * lv[3])) + lam_init
    yc = _mix_c(lam.reshape(1), cq_t, ck, cv_t, tile4(p["c_out_norm"]).reshape(GROUP, 1), nb, s, lam_init)
    yd = _mix_d(dq_t, dk, dv_t, nb, s)

    return _mix_out_ffn(x, ya, yb, yc, yd, p["w_out"].astype(BF16), p["ffn2_norm"], p["ffn2_wg"].astype(BF16),
                        p["ffn2_wu"].astype(BF16), p["ffn2_wd"].astype(BF16))


_PARAM_NAMES = ("ffn1_norm", "ffn1_wg", "ffn1_wu", "ffn1_wd", "mix_norm", "w_in", "a_q_norm", "a_k_norm",
                "a_rel_bias", "b_conv_w", "b_conv_b", "b_gate_bias", "b_out_norm", "c_q_norm", "c_k_norm",
                "c_lambda", "c_out_norm", "w_out", "ffn2_norm", "ffn2_wg", "ffn2_wu", "ffn2_wd")


def kernel(x, ffn1_norm, ffn1_wg, ffn1_wu, ffn1_wd, mix_norm, w_in, a_q_norm, a_k_norm, a_rel_bias,
           b_conv_w, b_conv_b, b_gate_bias, b_out_norm, c_q_norm, c_k_norm, c_lambda, c_out_norm,
           w_out, ffn2_norm, ffn2_wg, ffn2_wu, ffn2_wd):
    params = dict(zip(_PARAM_NAMES, (ffn1_norm, ffn1_wg, ffn1_wu, ffn1_wd, mix_norm, w_in, a_q_norm, a_k_norm,
                                     a_rel_bias, b_conv_w, b_conv_b, b_gate_bias, b_out_norm, c_q_norm,
                                     c_k_norm, c_lambda, c_out_norm, w_out, ffn2_norm, ffn2_wg, ffn2_wu,
                                     ffn2_wd)))
    nb, s, d = x.shape
    depth = ffn1_norm.shape[0]
    h = x.reshape(nb * s, d)
    for l in range(depth):
        lam_init = 0.8 - 0.6 * math.exp(-0.3 * l)
        h = _layer(h, nb, s, lam_init, {k: v[l] for k, v in params.items()})
    return h.reshape(nb, s, d)
```

```python
import functools
import math

import jax
import jax.numpy as jnp
import numpy as np
from jax import lax
from jax.experimental import pallas as pl
from jax.experimental.pallas import tpu as pltpu

F32 = jnp.float32
BF16 = jnp.bfloat16

D_MODEL = 1024
D_FF = 2816
CHUNK = 64
HEAD_DIM = 64
N_HEADS = 4
GROUP = N_HEADS * HEAD_DIM
LEFT_CHUNKS = 8
REL_CLIP = 128
CONV_WIDTH = 4
DIFF_QK_DIM = HEAD_DIM // 2
EPS = 1e-6
NEG = -1e30
LOG2E = 1.4426950408889634

OFF_AK = 0
OFF_BQK, OFF_BV, OFF_BO = 256, 768, 1024
OFF_CK = 1280
OFF_DK = 1536
MAIN_COLS = 1792
N_T_GROUPS = 6
GATE_LANES = 128
GATE_ROWS = 16
ONES_ROWS = 16
VROWS = HEAD_DIM + ONES_ROWS

VMEM_LIMIT = 56 * 1024 * 1024

TM_FFN = 512
TM_PROJ = 512
TILE_T = 256
TQ_A = 256
LB = 256
TQ_C = 256
TK_C = 256
TQ_D = 256
TK_D = 256
SB_SKIP_LOG2 = -160.0


def _cparams(sem):
    return pltpu.CompilerParams(dimension_semantics=sem, vmem_limit_bytes=VMEM_LIMIT)


def _const_spec(shape):
    nd = len(shape)
    return pl.BlockSpec(shape, lambda *_: (0,) * nd, pipeline_mode=pl.Buffered(1))


def _split_dot(x, mat, terms, x_is_lhs=True):
    acc = None
    rem = x
    for term in range(terms):
        part = rem.astype(BF16)
        if term + 1 < terms:
            rem = rem - part.astype(F32)
        d = (jnp.dot(part, mat, preferred_element_type=F32) if x_is_lhs
             else jnp.dot(mat, part, preferred_element_type=F32))
        acc = d if acc is None else acc + d
    return acc


def _group_ones(width):
    r = lax.broadcasted_iota(jnp.int32, (GROUP, GROUP), 0) // width
    c = lax.broadcasted_iota(jnp.int32, (GROUP, GROUP), 1) // width
    return jnp.where(r == c, 1.0, 0.0).astype(BF16)


def _group_rms(x, gain, width):
    ss = _split_dot(x * x, _group_ones(width), 2)
    return x * lax.rsqrt(ss * (1.0 / width) + EPS) * gain


def _log_sigmoid(x):
    return jnp.minimum(x, 0.0) - jnp.log1p(jnp.exp(-jnp.abs(x)))


def _head_mask(shape, h, axis=1, width=HEAD_DIM):
    lane = lax.broadcasted_iota(jnp.int32, shape, axis)
    return (lane // width) == h


def _ffn_half_step(x, g_ref, wg_ref, wu_ref, wd_ref):
    xn = x * lax.rsqrt(jnp.mean(x * x, axis=-1, keepdims=True) + EPS) * g_ref[...]
    xb = xn.astype(BF16)
    g = jnp.dot(xb, wg_ref[...], preferred_element_type=F32)
    u = jnp.dot(xb, wu_ref[...], preferred_element_type=F32)
    h = (g * jax.nn.sigmoid(g) * u).astype(BF16)
    return x + 0.5 * jnp.dot(h, wd_ref[...], preferred_element_type=F32)


def _ffn_kernel(x_ref, g_ref, wg_ref, wu_ref, wd_ref, o_ref):
    o_ref[...] = _ffn_half_step(x_ref[...], g_ref, wg_ref, wu_ref, wd_ref)


def _mix_out_ffn_kernel(x_ref, ya_ref, yb_ref, yc_ref, yd_ref, wo_ref, g_ref, wg_ref, wu_ref, wd_ref, o_ref):
    x = x_ref[...]
    for grp, y_ref in enumerate((ya_ref, yb_ref, yc_ref, yd_ref)):
        x = x + jnp.dot(y_ref[...], wo_ref[grp * GROUP:(grp + 1) * GROUP, :], preferred_element_type=F32)
    o_ref[...] = _ffn_half_step(x, g_ref, wg_ref, wu_ref, wd_ref)


def _ffn_specs():
    return [_const_spec((1, D_MODEL)), _const_spec((D_MODEL, D_FF)), _const_spec((D_MODEL, D_FF)),
            _const_spec((D_FF, D_MODEL))]


def _ffn(x, gain, wg, wu, wd):
    t = x.shape[0]
    tm = min(TM_FFN, t)
    row = pl.BlockSpec((tm, D_MODEL), lambda i: (i, 0))
    return pl.pallas_call(
        _ffn_kernel,
        grid=(t // tm,),
        in_specs=[row] + _ffn_specs(),
        out_specs=row,
        out_shape=jax.ShapeDtypeStruct((t, D_MODEL), F32),
        compiler_params=_cparams(("parallel",)),
        name="ffn_half_step",
    )(x, gain.reshape(1, D_MODEL), wg, wu, wd)


def _mix_out_ffn(x, ya, yb, yc, yd, w_out, gain, wg, wu, wd):
    t = x.shape[0]
    tm = min(TM_FFN, t)
    row = lambda w: pl.BlockSpec((tm, w), lambda i: (i, 0))
    return pl.pallas_call(
        _mix_out_ffn_kernel,
        grid=(t // tm,),
        in_specs=[row(D_MODEL), row(GROUP), row(GROUP), row(GROUP), row(GROUP),
                  _const_spec((D_MODEL, D_MODEL))] + _ffn_specs(),
        out_specs=row(D_MODEL),
        out_shape=jax.ShapeDtypeStruct((t, D_MODEL), F32),
        compiler_params=_cparams(("parallel",)),
        name="mix_out_proj_ffn",
    )(x, ya, yb, yc, yd, w_out, gain.reshape(1, D_MODEL), wg, wu, wd)


def _proj_kernel(x_ref, g_ref, w_ref, wct_ref, wgc_ref, wgr_ref, gbc_ref, gbr_ref,
                 aqn_ref, akn_ref, cqn_ref, ckn_ref,
                 aq_ref, ak_ref, av_ref, bqk_ref, bv_ref, bo_ref, gc_ref, gr_ref,
                 cq_ref, ck_ref, cv_ref, dq_ref, dk_ref, dv_ref):
    x = x_ref[...]
    hn = x * lax.rsqrt(jnp.mean(x * x, axis=-1, keepdims=True) + EPS) * g_ref[...]
    hb = hn.astype(BF16)

    def cols(off, width=GROUP):
        return jnp.dot(hb, w_ref[:, off:off + width], preferred_element_type=F32)

    nt_dims = (((1,), (1,)), ((), ()))

    def rows_t(idx):
        return lax.dot_general(wct_ref[idx * GROUP:(idx + 1) * GROUP, :], hb, nt_dims, preferred_element_type=F32)

    def group_rms_t(x_t, gain_col, width):
        ss = _split_dot(x_t * x_t, _group_ones(width), 2, x_is_lhs=False)
        return x_t * lax.rsqrt(ss * (1.0 / width) + EPS) * gain_col

    aqt = (group_rms_t(rows_t(4), aqn_ref[...], HEAD_DIM) * (HEAD_DIM ** -0.5 * LOG2E)).astype(BF16)
    ak_ref[...] = _group_rms(cols(OFF_AK), akn_ref[...], HEAD_DIM).astype(BF16)
    avt = rows_t(5).astype(BF16)
    bqk_ref[...] = cols(OFF_BQK, 2 * GROUP)
    bv_ref[...] = cols(OFF_BV).astype(BF16)
    bo_ref[...] = jax.nn.sigmoid(cols(OFF_BO))
    gcol = jnp.dot(hb, wgc_ref[...], preferred_element_type=F32) + gbc_ref[...]
    lane = lax.broadcasted_iota(jnp.int32, gcol.shape, 1)
    gc_ref[...] = jnp.where(lane >= N_HEADS, _log_sigmoid(gcol), gcol)
    grow = lax.dot_general(wgr_ref[...], hb, (((1,), (1,)), ((), ())),
                           preferred_element_type=F32) + gbr_ref[...]
    row = lax.broadcasted_iota(jnp.int32, grow.shape, 0)
    gr_ref[...] = jnp.where(row >= N_HEADS, _log_sigmoid(grow), grow)
    ck_ref[...] = _group_rms(cols(OFF_CK), ckn_ref[...], DIFF_QK_DIM).astype(BF16)
    cqt = (group_rms_t(rows_t(0), cqn_ref[...], DIFF_QK_DIM) * (DIFF_QK_DIM ** -0.5 * LOG2E)).astype(BF16)
    cvt = rows_t(1).astype(BF16)
    dk_ref[...] = cols(OFF_DK).astype(BF16)
    dqt = (rows_t(2) * (HEAD_DIM ** -0.5 * LOG2E)).astype(BF16)
    dvt = rows_t(3).astype(BF16)
    for sub in range(cq_ref.shape[0]):
        tok = slice(sub * TILE_T, (sub + 1) * TILE_T)
        aq_ref[sub] = aqt[:, tok]
        av_ref[sub] = avt[:, tok]
        cq_ref[sub] = cqt[:, tok]
        dq_ref[sub] = dqt[:, tok]
        dv_ref[sub] = dvt[:, tok]
        for h in range(N_HEADS):
            cv_ref[sub, h * VROWS:h * VROWS + HEAD_DIM, :] = cvt[h * HEAD_DIM:(h + 1) * HEAD_DIM, tok]
            cv_ref[sub, h * VROWS + HEAD_DIM:(h + 1) * VROWS, :] = jnp.ones((ONES_ROWS, TILE_T), BF16)


def _proj(x, gain, w_main, w_ct, w_gc, w_gr, gb_col, gb_row, aqn, akn, cqn_col, ckn):
    t = x.shape[0]
    tm = TM_PROJ
    row_spec = lambda w: pl.BlockSpec((tm, w), lambda i: (i, 0))
    bf = lambda w: jax.ShapeDtypeStruct((t, w), BF16)
    f32 = lambda w: jax.ShapeDtypeStruct((t, w), F32)
    n_sub = tm // TILE_T
    tr_shape = jax.ShapeDtypeStruct((t // TILE_T, GROUP, TILE_T), BF16)
    tr_spec = pl.BlockSpec((n_sub, GROUP, TILE_T), lambda i: (i, 0, 0))
    vt_shape = jax.ShapeDtypeStruct((t // TILE_T, N_HEADS * VROWS, TILE_T), BF16)
    vt_spec = pl.BlockSpec((n_sub, N_HEADS * VROWS, TILE_T), lambda i: (i, 0, 0))
    out_shape = [tr_shape, bf(GROUP), tr_shape,
                 f32(2 * GROUP), bf(GROUP), f32(GROUP),
                 f32(GATE_LANES), jax.ShapeDtypeStruct((GATE_ROWS, t), F32),
                 tr_shape, bf(GROUP), vt_shape,
                 tr_shape, bf(GROUP), tr_shape]
    out_specs = ([tr_spec, row_spec(GROUP), tr_spec, row_spec(2 * GROUP), row_spec(GROUP), row_spec(GROUP),
                  row_spec(GATE_LANES), pl.BlockSpec((GATE_ROWS, tm), lambda i: (0, i))]
                 + [tr_spec, row_spec(GROUP), vt_spec, tr_spec, row_spec(GROUP), tr_spec])
    return pl.pallas_call(
        _proj_kernel,
        grid=(t // tm,),
        in_specs=[row_spec(D_MODEL), _const_spec((1, D_MODEL)),
                  _const_spec((D_MODEL, MAIN_COLS)), _const_spec((N_T_GROUPS * GROUP, D_MODEL)),
                  _const_spec((D_MODEL, GATE_LANES)),
                  _const_spec((GATE_ROWS, D_MODEL)), _const_spec((1, GATE_LANES)),
                  _const_spec((GATE_ROWS, 1)),
                  _const_spec((GROUP, 1)), _const_spec((1, GROUP)),
                  _const_spec((GROUP, 1)), _const_spec((1, GROUP))],
        out_specs=out_specs,
        out_shape=out_shape,
        compiler_params=_cparams(("parallel",)),
        name="mix_in_proj",
    )(x, gain.reshape(1, D_MODEL), w_main, w_ct, w_gc, w_gr, gb_col, gb_row, aqn, akn, cqn_col, ckn)


def _mix_a_kernel(qt_ref, k0_ref, k1_ref, k2_ref, v0_ref, v1_ref, v2_ref, bias_ref, o_ref):
    t = pl.program_id(1)
    q_t = qt_ref[0]
    k = jnp.concatenate([k0_ref[...], k1_ref[...], k2_ref[...]], axis=0)
    v_t = jnp.concatenate([v0_ref[0], v1_ref[0], v2_ref[0]], axis=1)
    nk = k.shape[0]

    def attend(mask_start):
        scores = [jnp.dot(k[:, h * HEAD_DIM:(h + 1) * HEAD_DIM], q_t[h * HEAD_DIM:(h + 1) * HEAD_DIM, :],
                          preferred_element_type=F32) for h in range(N_HEADS)]
        if mask_start:
            key_pos = t * TQ_A - LEFT_CHUNKS * CHUNK + lax.broadcasted_iota(jnp.int32, (nk, TQ_A), 0)
            valid = key_pos >= 0
        probs = []
        for h in range(N_HEADS):
            s = scores[h] + bias_ref[h]
            if mask_start:
                s = jnp.where(valid, s, NEG)
            e = jnp.exp2(s - jnp.max(s, axis=0, keepdims=True))
            probs.append((e * (1.0 / jnp.sum(e, axis=0, keepdims=True))).astype(BF16))
        heads = [jnp.dot(v_t[h * HEAD_DIM:(h + 1) * HEAD_DIM, :], probs[h], preferred_element_type=F32)
                 for h in range(N_HEADS)]
        o_ref[...] = jnp.concatenate(heads, axis=0).T.astype(BF16)

    n_start_tiles = LEFT_CHUNKS * CHUNK // TQ_A

    @pl.when(t < n_start_tiles)
    def _():
        attend(True)

    @pl.when(t >= n_start_tiles)
    def _():
        attend(False)


def _mix_a(q_t, k, v_t, bias_tile_t, nb, s):
    nt = s // TQ_A
    back_idx = lambda b, t, back: b * nt + jnp.maximum(t - back, 0)
    rows = lambda back: pl.BlockSpec((TQ_A, GROUP), lambda b, t: (back_idx(b, t, back), 0))
    tr = lambda back: pl.BlockSpec((1, GROUP, TQ_A), lambda b, t: (back_idx(b, t, back), 0, 0))
    return pl.pallas_call(
        _mix_a_kernel,
        grid=(nb, nt),
        in_specs=[tr(0), rows(2), rows(1), rows(0), tr(2), tr(1), tr(0), _const_spec(bias_tile_t.shape)],
        out_specs=rows(0),
        out_shape=jax.ShapeDtypeStruct((nb * s, GROUP), BF16),
        compiler_params=_cparams(("parallel", "parallel")),
        name="mix_a_chunk_attn",
    )(q_t, k, k, k, v_t, v_t, v_t, bias_tile_t)


def _mix_b_kernel(qk_ref, v_ref, og_ref, gc_ref, gr_ref, cw_ref, cb_ref, on_ref, o_ref,
                  xs_ref, c_ref, n_ref, m_ref):
    c_idx = pl.program_id(1)

    @pl.when(c_idx == 0)
    def _():
        xs_ref[0:8, :] = jnp.zeros((8, 2 * GROUP), F32)
        c_ref[...] = jnp.zeros_like(c_ref)
        n_ref[...] = jnp.zeros_like(n_ref)
        m_ref[...] = jnp.zeros_like(m_ref)

    xs_ref[8:8 + LB, :] = qk_ref[...]
    acc = jnp.broadcast_to(cb_ref[...], (LB, 2 * GROUP))
    for j in range(CONV_WIDTH):
        start = 8 - (CONV_WIDTH - 1) + j
        acc = acc + xs_ref[start:start + LB, :] * cw_ref[j:j + 1, :]
    xs_ref[0:8, :] = xs_ref[LB:LB + 8, :]
    qk = acc * jax.nn.sigmoid(acc)
    q = qk[:, :GROUP]
    k = qk[:, GROUP:] * (HEAD_DIM ** -0.5)
    qb = q.astype(BF16)
    kb = k.astype(BF16)
    v = v_ref[...]

    r = lax.broadcasted_iota(jnp.int32, (LB, LB), 0)
    c = lax.broadcasted_iota(jnp.int32, (LB, LB), 1)
    causal = c <= r
    tri = jnp.where(causal, 1.0, 0.0).astype(BF16)
    tri_t = jnp.where(r <= c, 1.0, 0.0).astype(BF16)
    gcol = gc_ref[...]
    grow = gr_ref[...]
    bcum_col = _split_dot(gcol, tri, 3, x_is_lhs=False)
    bcum_row = _split_dot(grow, tri_t, 3)

    c_state = c_ref[...]
    n_state = n_ref[...]
    q_c = jnp.dot(qb, c_state.astype(BF16), preferred_element_type=F32)
    qn = qb.astype(F32) * n_state.astype(BF16).astype(F32)
    q_n = _split_dot(qn, _group_ones(HEAD_DIM), 2)

    lane = lax.broadcasted_iota(jnp.int32, (LB, GROUP), 1) // HEAD_DIM
    num = jnp.zeros((LB, GROUP), F32)
    den = jnp.zeros((LB, GROUP), F32)
    floor = jnp.zeros((LB, GROUP), F32)
    wg_full = jnp.zeros((LB, GROUP), F32)
    a_full = jnp.zeros((1, GROUP), F32)
    lane1 = lax.broadcasted_iota(jnp.int32, (1, GROUP), 1) // HEAD_DIM
    a_list = []
    for h in range(N_HEADS):
        i_col = gcol[:, h:h + 1]
        i_row = grow[h:h + 1, :]
        b_col = bcum_col[:, N_HEADS + h:N_HEADS + h + 1]
        b_row = bcum_row[N_HEADS + h:N_HEADS + h + 1, :]
        m_prev = m_ref[h:h + 1, 0:1]
        dmat = jnp.where(causal, b_col - b_row + i_row, NEG)
        inter = b_col + m_prev
        m_t = jnp.maximum(inter, jnp.max(dmat, axis=-1, keepdims=True))
        w_intra = jnp.exp(dmat - m_t)
        s_inter = jnp.exp(inter - m_t)
        qh = jnp.where(_head_mask(qb.shape, h), qb, jnp.zeros_like(qb))
        sc = lax.dot_general(qh, kb, (((1,), (1,)), ((), ())), preferred_element_type=F32) * w_intra
        vh = jnp.where(_head_mask(v.shape, h), v, jnp.zeros_like(v))
        num_h = s_inter * q_c + jnp.dot(sc.astype(BF16), vh, preferred_element_type=F32)
        den_h = s_inter * q_n + jnp.sum(sc, axis=-1, keepdims=True)
        sel = lane == h
        num = jnp.where(sel, num_h, num)
        den = jnp.where(sel, den_h, den)
        floor = jnp.where(sel, jnp.exp(-m_t), floor)
        b_tot = b_col[LB - 1:LB, :]
        g = b_tot - b_col + i_col
        m_new = jnp.maximum(b_tot + m_prev, jnp.max(g, axis=0, keepdims=True))
        a_h = jnp.exp(b_tot + m_prev - m_new)
        a_list.append(a_h)
        a_full = jnp.where(lane1 == h, a_h, a_full)
        wg_full = jnp.where(sel, jnp.exp(g - m_new), wg_full)
        m_ref[h:h + 1, :] = jnp.broadcast_to(m_new, (1, m_ref.shape[1]))

    hb = num / jnp.maximum(jnp.abs(den), floor)
    o_ref[...] = (_group_rms(hb, on_ref[...], HEAD_DIM) * og_ref[...]).astype(BF16)

    kw = k * wg_full
    kv = jnp.dot(kw.T.astype(BF16), v, preferred_element_type=F32)
    r2 = lax.broadcasted_iota(jnp.int32, (GROUP, GROUP), 0) // HEAD_DIM
    c2 = lax.broadcasted_iota(jnp.int32, (GROUP, GROUP), 1) // HEAD_DIM
    a_mat = jnp.zeros((GROUP, GROUP), F32)
    for h in range(N_HEADS):
        a_mat = jnp.where(r2 == h, a_list[h], a_mat)
    c_ref[...] = a_mat * c_state + jnp.where(r2 == c2, kv, 0.0)
    n_ref[...] = a_full * n_state + jnp.sum(kw, axis=0, keepdims=True)


def _mix_b(bqk, bv, bo, gcol, grow, conv_w, conv_b, out_norm, nb, s):
    nc = s // LB
    row = lambda w: pl.BlockSpec((LB, w), lambda b, c: (b * nc + c, 0))
    return pl.pallas_call(
        _mix_b_kernel,
        grid=(nb, nc),
        in_specs=[row(2 * GROUP), row(GROUP), row(GROUP), row(GATE_LANES),
                  pl.BlockSpec((GATE_ROWS, LB), lambda b, c: (0, b * nc + c)),
                  _const_spec((CONV_WIDTH, 2 * GROUP)), _const_spec((1, 2 * GROUP)),
                  _const_spec((1, GROUP))],
        out_specs=row(GROUP),
        out_shape=jax.ShapeDtypeStruct((nb * s, GROUP), BF16),
        scratch_shapes=[pltpu.VMEM((LB + 8, 2 * GROUP), F32),
                        pltpu.VMEM((GROUP, GROUP), F32),
                        pltpu.VMEM((1, GROUP), F32),
                        pltpu.VMEM((8, 128), F32)],
        compiler_params=_cparams(("parallel", "arbitrary")),
        name="mix_b_mlstm",
    )(bqk, bv, bo, gcol, grow, conv_w, conv_b, out_norm)


def _mix_c_kernel(lam_ref, qt_ref, k_ref, vt_ref, on_ref, o_ref, p_ref, acc_ref, m_ref, al_ref, *, lam_init):
    qt = pl.program_id(1)
    q_t = qt_ref[0]
    n_maps = 2 * N_HEADS
    acc_ref[...] = jnp.zeros(acc_ref.shape, F32)

    def key_tile(kt):
        return k_ref[pl.ds(pl.multiple_of(kt * TK_C, TK_C), TK_C), :]

    def score_map(k, j):
        ch = slice(j * DIFF_QK_DIM, (j + 1) * DIFF_QK_DIM)
        return jnp.dot(k[:, ch], q_t[ch, :], preferred_element_type=F32)

    def softmax(sc, slot, masked=False):
        if masked:
            key_chunk = lax.broadcasted_iota(jnp.int32, (TK_C, TQ_C), 0) // CHUNK
            q_chunk = lax.broadcasted_iota(jnp.int32, (TK_C, TQ_C), 1) // CHUNK
            vis = key_chunk <= q_chunk
        for j in range(n_maps):
            s = jnp.where(vis, sc[j], NEG) if masked else sc[j]
            m_old = m_ref[j]
            m_new = jnp.maximum(m_old, jnp.max(s, axis=0, keepdims=True))
            al_ref[j] = jnp.exp2(m_old - m_new)
            m_ref[j] = m_new
            p_ref[slot, j] = jnp.exp2(s - m_new).astype(BF16)

    def pv_update(kt, slot):
        v_ext = vt_ref[kt]
        for j in range(n_maps):
            h = j // 2
            pv = jnp.dot(v_ext[h * VROWS:(h + 1) * VROWS, :], p_ref[slot, j], preferred_element_type=F32)
            acc_ref[j] = al_ref[j] * acc_ref[j] + pv

    def step(kt, prev, slot):
        k = key_tile(kt)
        v_ext = vt_ref[prev]
        sc = []
        for j in range(n_maps):
            h = j // 2
            sc.append(score_map(k, j))
            pv = jnp.dot(v_ext[h * VROWS:(h + 1) * VROWS, :], p_ref[slot, j], preferred_element_type=F32)
            acc_ref[j] = al_ref[j] * acc_ref[j] + pv
        softmax(sc, 1 - slot)

    m_ref[...] = jnp.full(m_ref.shape, NEG, F32)
    k_diag = key_tile(qt)
    softmax([score_map(k_diag, j) for j in range(n_maps)], 0, masked=True)
    n_pairs = qt // 2

    def body(ip, carry):
        a = 2 * ip
        step(a, jnp.where(ip == 0, qt, a - 1), 0)
        step(a + 1, a, 1)
        return carry

    lax.fori_loop(0, n_pairs, body, 0)
    last = jnp.where(n_pairs > 0, 2 * n_pairs - 1, qt)
    odd = qt - 2 * n_pairs

    @pl.when(odd == 1)
    def _():
        step(qt - 1, last, 0)
        pv_update(qt - 1, 1)

    @pl.when(odd == 0)
    def _():
        pv_update(last, 0)

    lam = lam_ref[0]
    heads = []
    for h in range(N_HEADS):
        a0 = acc_ref[2 * h]
        a1 = acc_ref[2 * h + 1]
        o_h = (a0[:HEAD_DIM] * (1.0 / a0[HEAD_DIM:HEAD_DIM + 1])
               - lam * (a1[:HEAD_DIM] * (1.0 / a1[HEAD_DIM:HEAD_DIM + 1])))
        ms = jnp.mean(o_h * o_h, axis=0, keepdims=True)
        heads.append(o_h * lax.rsqrt(ms + EPS))
    out_t = jnp.concatenate(heads, axis=0) * (on_ref[...] * (1.0 - lam_init))
    o_ref[...] = out_t.T.astype(BF16)


def _mix_c(lam, q_t, k, v_t, out_norm_col, nb, s, lam_init):
    nt = s // TQ_C
    tile_t = pl.BlockSpec((1, GROUP, TQ_C), lambda b, t, lam_ref: (b * nt + t, 0, 0))
    full_k = pl.BlockSpec((s, GROUP), lambda b, t, lam_ref: (b, 0), pipeline_mode=pl.Buffered(1))
    full_vt = pl.BlockSpec((nt, N_HEADS * VROWS, TK_C), lambda b, t, lam_ref: (b, 0, 0),
                           pipeline_mode=pl.Buffered(1))
    grid_spec = pltpu.PrefetchScalarGridSpec(
        num_scalar_prefetch=1,
        grid=(nb, nt),
        in_specs=[tile_t, full_k, full_vt,
                  pl.BlockSpec((GROUP, 1), lambda b, t, lam_ref: (0, 0), pipeline_mode=pl.Buffered(1))],
        out_specs=pl.BlockSpec((TQ_C, GROUP), lambda b, t, lam_ref: (b * nt + t, 0)),
        scratch_shapes=[pltpu.VMEM((2, 2 * N_HEADS, TK_C, TQ_C), BF16),
                        pltpu.VMEM((2 * N_HEADS, VROWS, TQ_C), F32),
                        pltpu.VMEM((2 * N_HEADS, 1, TQ_C), F32),
                        pltpu.VMEM((2 * N_HEADS, 1, TQ_C), F32)],
    )
    return pl.pallas_call(
        functools.partial(_mix_c_kernel, lam_init=lam_init),
        grid_spec=grid_spec,
        out_shape=jax.ShapeDtypeStruct((nb * s, GROUP), BF16),
        compiler_params=_cparams(("parallel", "parallel")),
        name="mix_c_diff_attn",
    )(lam, q_t, k, v_t, out_norm_col)


def _mix_d_kernel(qt_ref, k_ref, vt_ref, o_ref, acc_ref, run_ref):
    qt = pl.program_id(1)
    q_t = qt_ref[0]
    acc_ref[...] = jnp.zeros(acc_ref.shape, F32)
    run_ref[...] = jnp.zeros(run_ref.shape, F32)
    r = lax.broadcasted_iota(jnp.int32, (TK_D, TK_D), 0)
    c = lax.broadcasted_iota(jnp.int32, (TK_D, TK_D), 1)
    later = jnp.where(c > r, 1.0, 0.0).astype(BF16)

    def tile(kt, masked):
        start = pl.multiple_of(kt * TK_D, TK_D)
        k = k_ref[pl.ds(start, TK_D), :]
        v_t = vt_ref[kt]
        zs = [jnp.dot(k[:, h * HEAD_DIM:(h + 1) * HEAD_DIM], q_t[h * HEAD_DIM:(h + 1) * HEAD_DIM, :],
                      preferred_element_type=F32) for h in range(N_HEADS)]
        if masked:
            before = (lax.broadcasted_iota(jnp.int32, (TK_D, TQ_D), 0)
                      < lax.broadcasted_iota(jnp.int32, (TK_D, TQ_D), 1))
        worst = None
        log_take, between = [], []
        for h in range(N_HEADS):
            z = zs[h]
            log_keep = jnp.minimum(-z, 0.0) - jnp.log2(1.0 + jnp.exp2(-jnp.abs(z)))
            log_take.append(z + log_keep)
            if masked:
                log_keep = jnp.where(before, log_keep, 0.0)
            run = run_ref[h]
            between.append(run + _split_dot(log_keep, later, 2, x_is_lhs=False))
            run = run + jnp.sum(log_keep, axis=0, keepdims=True)
            run_ref[h] = run
            top = jnp.max(run)
            worst = top if worst is None else jnp.maximum(worst, top)
        for h in range(N_HEADS):
            a = jnp.exp2(log_take[h] + between[h])
            if masked:
                a = jnp.where(before, a, 0.0)
            acc_ref[h] += jnp.dot(v_t[h * HEAD_DIM:(h + 1) * HEAD_DIM, :], a.astype(BF16),
                                  preferred_element_type=F32)
        return worst

    worst = tile(qt, True)

    def cond(state):
        kt, worst = state
        return jnp.logical_and(kt >= 0, worst > SB_SKIP_LOG2)

    def body(state):
        kt, _ = state
        return kt - 1, tile(kt, False)

    lax.while_loop(cond, body, (qt - 1, worst))
    out_t = jnp.concatenate([acc_ref[h] for h in range(N_HEADS)], axis=0)
    o_ref[...] = out_t.T.astype(BF16)


def _mix_d(q_t, k, v_t, nb, s):
    nt = s // TQ_D
    tile_t = pl.BlockSpec((1, GROUP, TQ_D), lambda b, t: (b * nt + t, 0, 0))
    full_k = pl.BlockSpec((s, GROUP), lambda b, t: (b, 0), pipeline_mode=pl.Buffered(1))
    full_vt = pl.BlockSpec((nt, GROUP, TK_D), lambda b, t: (b, 0, 0), pipeline_mode=pl.Buffered(1))
    return pl.pallas_call(
        _mix_d_kernel,
        grid=(nb, nt),
        in_specs=[tile_t, full_k, full_vt],
        out_specs=pl.BlockSpec((TQ_D, GROUP), lambda b, t: (b * nt + t, 0)),
        out_shape=jax.ShapeDtypeStruct((nb * s, GROUP), BF16),
        scratch_shapes=[pltpu.VMEM((N_HEADS, HEAD_DIM, TQ_D), F32),
                        pltpu.VMEM((N_HEADS, 1, TQ_D), F32)],
        compiler_params=_cparams(("parallel", "parallel")),
        name="mix_d_stick_breaking",
    )(q_t, k, v_t)


def _rel_bias_tile(rel_bias):
    nk = TQ_A + LEFT_CHUNKS * CHUNK
    i = np.arange(TQ_A)[:, None]
    j = np.arange(nk)[None, :]
    dc = j // CHUNK - i // CHUNK
    visible = (dc >= 0) & (dc <= LEFT_CHUNKS)
    p = TQ_A + nk
    d = np.arange(p)
    rel = d - (nk - 1) + LEFT_CHUNKS * CHUNK
    idx = np.clip(rel, -REL_CLIP, REL_CLIP) + REL_CLIP
    diag = rel_bias[:, idx].astype(F32)
    nh = rel_bias.shape[0]
    skew = jnp.tile(diag, (1, TQ_A + 1))[:, :TQ_A * (p + 1)].reshape(nh, TQ_A, p + 1)
    toeplitz = skew[:, :, :nk][:, :, ::-1]
    return jnp.swapaxes(jnp.where(jnp.asarray(visible)[None], toeplitz * LOG2E, NEG), 1, 2)


def _permute_w_in(w_in):
    sizes = [GROUP, GROUP, GROUP, 2 * GROUP, GROUP, GROUP, N_HEADS, N_HEADS,
             GROUP, GROUP, GROUP, GROUP, GROUP, GROUP]
    offs = np.concatenate([[0], np.cumsum(sizes)])
    seg = [w_in[:, offs[n]:offs[n + 1]] for n in range(len(sizes))]
    main = jnp.concatenate([seg[1], seg[3], seg[4], seg[5], seg[9], seg[12]], axis=1).astype(BF16)
    c_t = jnp.concatenate([seg[n].T for n in (8, 10, 11, 13, 0, 2)], axis=0).astype(BF16)
    gates = jnp.concatenate([seg[6], seg[7]], axis=1)
    g_col = jnp.pad(gates, ((0, 0), (0, GATE_LANES - 2 * N_HEADS))).astype(BF16)
    g_row = jnp.pad(gates.T, ((0, GATE_ROWS - 2 * N_HEADS), (0, 0))).astype(BF16)
    return main, c_t, g_col, g_row


def _layer(x, nb, s, lam_init, p):
    x = _ffn(x, p["ffn1_norm"], p["ffn1_wg"].astype(BF16), p["ffn1_wu"].astype(BF16), p["ffn1_wd"].astype(BF16))

    w_main, w_ct, w_gc, w_gr = _permute_w_in(p["w_in"])
    gate_bias = p["b_gate_bias"].reshape(2 * N_HEADS).astype(F32)
    gb_col = jnp.pad(gate_bias, (0, GATE_LANES - 2 * N_HEADS)).reshape(1, GATE_LANES)
    gb_row = jnp.pad(gate_bias, (0, GATE_ROWS - 2 * N_HEADS)).reshape(GATE_ROWS, 1)
    tile4 = lambda g: jnp.tile(g.astype(F32), GROUP // g.shape[0]).reshape(1, GROUP)
    (aq_t, ak, av_t, bqk, bv, bo, gcol, grow, cq_t, ck, cv_t, dq_t, dk, dv_t) = _proj(
        x, p["mix_norm"], w_main, w_ct, w_gc, w_gr, gb_col, gb_row,
        tile4(p["a_q_norm"]).reshape(GROUP, 1), tile4(p["a_k_norm"]),
        tile4(p["c_q_norm"]).reshape(GROUP, 1), tile4(p["c_k_norm"]))

    ya = _mix_a(aq_t, ak, av_t, _rel_bias_tile(p["a_rel_bias"]), nb, s)
    yb = _mix_b(bqk, bv, bo, gcol, grow, p["b_conv_w"].astype(F32), p["b_conv_b"].astype(F32).reshape(1, -1),
                p["b_out_norm"].astype(F32).reshape(1, GROUP), nb, s)
    lv = p["c_lambda"].astype(F32)
    lam = jnp.exp(jnp.sum(lv[0] * lv[1])) - jnp.exp(jnp.sum(lv[2] * lv[3])) + lam_init
    yc = _mix_c(lam.reshape(1), cq_t, ck, cv_t, tile4(p["c_out_norm"]).reshape(GROUP, 1), nb, s, lam_init)
    yd = _mix_d(dq_t, dk, dv_t, nb, s)

    return _mix_out_ffn(x, ya, yb, yc, yd, p["w_out"].astype(BF16), p["ffn2_norm"], p["ffn2_wg"].astype(BF16),
                        p["ffn2_wu"].astype(BF16), p["ffn2_wd"].astype(BF16))


_PARAM_NAMES = ("ffn1_norm", "ffn1_wg", "ffn1_wu", "ffn1_wd", "mix_norm", "w_in", "a_q_norm", "a_k_norm",
                "a_rel_bias", "b_conv_w", "b_conv_b", "b_gate_bias", "b_out_norm", "c_q_norm", "c_k_norm",
                "c_lambda", "c_out_norm", "w_out", "ffn2_norm", "ffn2_wg", "ffn2_wu", "ffn2_wd")


def kernel(x, ffn1_norm, ffn1_wg, ffn1_wu, ffn1_wd, mix_norm, w_in, a_q_norm, a_k_norm, a_rel_bias,
           b_conv_w, b_conv_b, b_gate_bias, b_out_norm, c_q_norm, c_k_norm, c_lambda, c_out_norm,
           w_out, ffn2_norm, ffn2_wg, ffn2_wu, ffn2_wd):
    params = dict(zip(_PARAM_NAMES, (ffn1_norm, ffn1_wg, ffn1_wu, ffn1_wd, mix_norm, w_in, a_q_norm, a_k_norm,
                                     a_rel_bias, b_conv_w, b_conv_b, b_gate_bias, b_out_norm, c_q_norm,
                                     c_k_norm, c_lambda, c_out_norm, w_out, ffn2_norm, ffn2_wg, ffn2_wu,
                                     ffn2_wd)))
    nb, s, d = x.shape
    depth = ffn1_norm.shape[0]
    h = x.reshape(nb * s, d)
    for l in range(depth):
        lam_init = 0.8 - 0.6 * math.exp(-0.3 * l)
        h = _layer(h, nb, s, lam_init, {k: v[l] for k, v in params.items()})
    return h.reshape(nb, s, d)
```

```python
import functools
import math

import jax
import jax.numpy as jnp
import numpy as np
from jax import lax
from jax.experimental import pallas as pl
from jax.experimental.pallas import tpu as pltpu

F32 = jnp.float32
BF16 = jnp.bfloat16

D_MODEL = 1024
D_FF = 2816
CHUNK = 64
HEAD_DIM = 64
N_HEADS = 4
GROUP = N_HEADS * HEAD_DIM
LEFT_CHUNKS = 8
REL_CLIP = 128
CONV_WIDTH = 4
DIFF_QK_DIM = HEAD_DIM // 2
EPS = 1e-6
NEG = -1e30
LOG2E = 1.4426950408889634

OFF_AK = 0
OFF_BQK, OFF_BV, OFF_BO = 256, 768, 1024
OFF_CK = 1280
OFF_DK = 1536
MAIN_COLS = 1792
N_T_GROUPS = 6
GATE_LANES = 128
GATE_ROWS = 16
ONES_ROWS = 16
VROWS = HEAD_DIM + ONES_ROWS

VMEM_LIMIT = 56 * 1024 * 1024

TM_FFN = 512
TM_PROJ = 512
TILE_T = 256
TQ_A = 256
LB = 256
TQ_C = 256
TK_C = 256
C_STEPS_PER_TRIP = 4
TQ_D = 256
TK_D = 256
SB_SKIP_LOG2 = -160.0


def _cparams(sem):
    return pltpu.CompilerParams(dimension_semantics=sem, vmem_limit_bytes=VMEM_LIMIT)


def _const_spec(shape):
    nd = len(shape)
    return pl.BlockSpec(shape, lambda *_: (0,) * nd, pipeline_mode=pl.Buffered(1))


def _split_dot(x, mat, terms, x_is_lhs=True):
    acc = None
    rem = x
    for term in range(terms):
        part = rem.astype(BF16)
        if term + 1 < terms:
            rem = rem - part.astype(F32)
        d = (jnp.dot(part, mat, preferred_element_type=F32) if x_is_lhs
             else jnp.dot(mat, part, preferred_element_type=F32))
        acc = d if acc is None else acc + d
    return acc


def _group_ones(width):
    r = lax.broadcasted_iota(jnp.int32, (GROUP, GROUP), 0) // width
    c = lax.broadcasted_iota(jnp.int32, (GROUP, GROUP), 1) // width
    return jnp.where(r == c, 1.0, 0.0).astype(BF16)


def _group_rms(x, gain, width):
    ss = _split_dot(x * x, _group_ones(width), 2)
    return x * lax.rsqrt(ss * (1.0 / width) + EPS) * gain


def _log_sigmoid(x):
    return jnp.minimum(x, 0.0) - jnp.log1p(jnp.exp(-jnp.abs(x)))


def _head_mask(shape, h, axis=1, width=HEAD_DIM):
    lane = lax.broadcasted_iota(jnp.int32, shape, axis)
    return (lane // width) == h


def _ffn_half_step(x, g_ref, wg_ref, wu_ref, wd_ref):
    xn = x * lax.rsqrt(jnp.mean(x * x, axis=-1, keepdims=True) + EPS) * g_ref[...]
    xb = xn.astype(BF16)
    g = jnp.dot(xb, wg_ref[...], preferred_element_type=F32)
    u = jnp.dot(xb, wu_ref[...], preferred_element_type=F32)
    h = (g * jax.nn.sigmoid(g) * u).astype(BF16)
    return x + 0.5 * jnp.dot(h, wd_ref[...], preferred_element_type=F32)


def _ffn_kernel(x_ref, g_ref, wg_ref, wu_ref, wd_ref, o_ref):
    o_ref[...] = _ffn_half_step(x_ref[...], g_ref, wg_ref, wu_ref, wd_ref)


def _mix_out_ffn_kernel(x_ref, ya_ref, yb_ref, yc_ref, yd_ref, wo_ref, g_ref, wg_ref, wu_ref, wd_ref, o_ref):
    x = x_ref[...]
    for grp, y_ref in enumerate((ya_ref, yb_ref, yc_ref, yd_ref)):
        x = x + jnp.dot(y_ref[...], wo_ref[grp * GROUP:(grp + 1) * GROUP, :], preferred_element_type=F32)
    o_ref[...] = _ffn_half_step(x, g_ref, wg_ref, wu_ref, wd_ref)


def _ffn_specs():
    return [_const_spec((1, D_MODEL)), _const_spec((D_MODEL, D_FF)), _const_spec((D_MODEL, D_FF)),
            _const_spec((D_FF, D_MODEL))]


def _ffn(x, gain, wg, wu, wd):
    t = x.shape[0]
    tm = min(TM_FFN, t)
    row = pl.BlockSpec((tm, D_MODEL), lambda i: (i, 0))
    return pl.pallas_call(
        _ffn_kernel,
        grid=(t // tm,),
        in_specs=[row] + _ffn_specs(),
        out_specs=row,
        out_shape=jax.ShapeDtypeStruct((t, D_MODEL), F32),
        compiler_params=_cparams(("parallel",)),
        name="ffn_half_step",
    )(x, gain.reshape(1, D_MODEL), wg, wu, wd)


def _mix_out_ffn(x, ya, yb, yc, yd, w_out, gain, wg, wu, wd):
    t = x.shape[0]
    tm = min(TM_FFN, t)
    row = lambda w: pl.BlockSpec((tm, w), lambda i: (i, 0))
    return pl.pallas_call(
        _mix_out_ffn_kernel,
        grid=(t // tm,),
        in_specs=[row(D_MODEL), row(GROUP), row(GROUP), row(GROUP), row(GROUP),
                  _const_spec((D_MODEL, D_MODEL))] + _ffn_specs(),
        out_specs=row(D_MODEL),
        out_shape=jax.ShapeDtypeStruct((t, D_MODEL), F32),
        compiler_params=_cparams(("parallel",)),
        name="mix_out_proj_ffn",
    )(x, ya, yb, yc, yd, w_out, gain.reshape(1, D_MODEL), wg, wu, wd)


def _proj_kernel(x_ref, g_ref, w_ref, wct_ref, wgc_ref, gbc_ref, gbr_ref,
                 aqn_ref, akn_ref, cqn_ref, ckn_ref,
                 aq_ref, ak_ref, av_ref, bqk_ref, bv_ref, bo_ref, gc_ref, gr_ref,
                 cq_ref, ck_ref, cv_ref, dq_ref, dk_ref, dv_ref):
    x = x_ref[...]
    hn = x * lax.rsqrt(jnp.mean(x * x, axis=-1, keepdims=True) + EPS) * g_ref[...]
    hb = hn.astype(BF16)

    def cols(off, width=GROUP):
        return jnp.dot(hb, w_ref[:, off:off + width], preferred_element_type=F32)

    nt_dims = (((1,), (1,)), ((), ()))

    all_t = lax.dot_general(wct_ref[...], hb, nt_dims, preferred_element_type=F32)

    def rows_t(idx):
        return all_t[idx * GROUP:(idx + 1) * GROUP, :]

    def group_rms_t(x_t, gain_col, width):
        ss = _split_dot(x_t * x_t, _group_ones(width), 2, x_is_lhs=False)
        return x_t * lax.rsqrt(ss * (1.0 / width) + EPS) * gain_col

    aqt = (group_rms_t(rows_t(4), aqn_ref[...], HEAD_DIM) * (HEAD_DIM ** -0.5 * LOG2E)).astype(BF16)
    ak_ref[...] = _group_rms(cols(OFF_AK), akn_ref[...], HEAD_DIM).astype(BF16)
    avt = rows_t(5).astype(BF16)
    bqk_ref[...] = cols(OFF_BQK, 2 * GROUP)
    bv_ref[...] = cols(OFF_BV).astype(BF16)
    bo_ref[...] = jax.nn.sigmoid(cols(OFF_BO))
    gcol = jnp.dot(hb, wgc_ref[...], preferred_element_type=F32) + gbc_ref[...]
    lane = lax.broadcasted_iota(jnp.int32, gcol.shape, 1)
    gc_ref[...] = jnp.where(lane >= N_HEADS, _log_sigmoid(gcol), gcol)
    grow = all_t[N_T_GROUPS * GROUP:, :] + gbr_ref[...]
    row = lax.broadcasted_iota(jnp.int32, grow.shape, 0)
    gr_ref[...] = jnp.where(row >= N_HEADS, _log_sigmoid(grow), grow)
    ck_ref[...] = _group_rms(cols(OFF_CK), ckn_ref[...], DIFF_QK_DIM).astype(BF16)
    cqt = (group_rms_t(rows_t(0), cqn_ref[...], DIFF_QK_DIM) * (DIFF_QK_DIM ** -0.5 * LOG2E)).astype(BF16)
    cvt = rows_t(1).astype(BF16)
    dk_ref[...] = cols(OFF_DK).astype(BF16)
    dqt = (rows_t(2) * (HEAD_DIM ** -0.5 * LOG2E)).astype(BF16)
    dvt = rows_t(3).astype(BF16)
    for sub in range(cq_ref.shape[0]):
        tok = slice(sub * TILE_T, (sub + 1) * TILE_T)
        aq_ref[sub] = aqt[:, tok]
        av_ref[sub] = avt[:, tok]
        cq_ref[sub] = cqt[:, tok]
        dq_ref[sub] = dqt[:, tok]
        dv_ref[sub] = dvt[:, tok]
        for h in range(N_HEADS):
            cv_ref[sub, h * VROWS:h * VROWS + HEAD_DIM, :] = cvt[h * HEAD_DIM:(h + 1) * HEAD_DIM, tok]
            cv_ref[sub, h * VROWS + HEAD_DIM:(h + 1) * VROWS, :] = jnp.ones((ONES_ROWS, TILE_T), BF16)


def _proj(x, gain, w_main, w_ct, w_gc, gb_col, gb_row, aqn, akn, cqn_col, ckn):
    t = x.shape[0]
    tm = TM_PROJ
    row_spec = lambda w: pl.BlockSpec((tm, w), lambda i: (i, 0))
    bf = lambda w: jax.ShapeDtypeStruct((t, w), BF16)
    f32 = lambda w: jax.ShapeDtypeStruct((t, w), F32)
    n_sub = tm // TILE_T
    tr_shape = jax.ShapeDtypeStruct((t // TILE_T, GROUP, TILE_T), BF16)
    tr_spec = pl.BlockSpec((n_sub, GROUP, TILE_T), lambda i: (i, 0, 0))
    vt_shape = jax.ShapeDtypeStruct((t // TILE_T, N_HEADS * VROWS, TILE_T), BF16)
    vt_spec = pl.BlockSpec((n_sub, N_HEADS * VROWS, TILE_T), lambda i: (i, 0, 0))
    out_shape = [tr_shape, bf(GROUP), tr_shape,
                 f32(2 * GROUP), bf(GROUP), f32(GROUP),
                 f32(GATE_LANES), jax.ShapeDtypeStruct((GATE_ROWS, t), F32),
                 tr_shape, bf(GROUP), vt_shape,
                 tr_shape, bf(GROUP), tr_shape]
    out_specs = ([tr_spec, row_spec(GROUP), tr_spec, row_spec(2 * GROUP), row_spec(GROUP), row_spec(GROUP),
                  row_spec(GATE_LANES), pl.BlockSpec((GATE_ROWS, tm), lambda i: (0, i))]
                 + [tr_spec, row_spec(GROUP), vt_spec, tr_spec, row_spec(GROUP), tr_spec])
    return pl.pallas_call(
        _proj_kernel,
        grid=(t // tm,),
        in_specs=[row_spec(D_MODEL), _const_spec((1, D_MODEL)),
                  _const_spec((D_MODEL, MAIN_COLS)), _const_spec((N_T_GROUPS * GROUP + GATE_ROWS, D_MODEL)),
                  _const_spec((D_MODEL, GATE_LANES)), _const_spec((1, GATE_LANES)),
                  _const_spec((GATE_ROWS, 1)),
                  _const_spec((GROUP, 1)), _const_spec((1, GROUP)),
                  _const_spec((GROUP, 1)), _const_spec((1, GROUP))],
        out_specs=out_specs,
        out_shape=out_shape,
        compiler_params=_cparams(("parallel",)),
        name="mix_in_proj",
    )(x, gain.reshape(1, D_MODEL), w_main, w_ct, w_gc, gb_col, gb_row, aqn, akn, cqn_col, ckn)


def _mix_a_kernel(qt_ref, k0_ref, k1_ref, k2_ref, v0_ref, v1_ref, v2_ref, bias_ref, o_ref):
    t = pl.program_id(1)
    q_t = qt_ref[0]
    k = jnp.concatenate([k0_ref[...], k1_ref[...], k2_ref[...]], axis=0)
    v_t = jnp.concatenate([v0_ref[0], v1_ref[0], v2_ref[0]], axis=1)
    nk = k.shape[0]

    def attend(mask_start):
        scores = [jnp.dot(k[:, h * HEAD_DIM:(h + 1) * HEAD_DIM], q_t[h * HEAD_DIM:(h + 1) * HEAD_DIM, :],
                          preferred_element_type=F32) for h in range(N_HEADS)]
        if mask_start:
            key_pos = t * TQ_A - LEFT_CHUNKS * CHUNK + lax.broadcasted_iota(jnp.int32, (nk, TQ_A), 0)
            valid = key_pos >= 0
        probs = []
        for h in range(N_HEADS):
            s = scores[h] + bias_ref[h]
            if mask_start:
                s = jnp.where(valid, s, NEG)
            e = jnp.exp2(s - jnp.max(s, axis=0, keepdims=True))
            probs.append((e * (1.0 / jnp.sum(e, axis=0, keepdims=True))).astype(BF16))
        heads = [jnp.dot(v_t[h * HEAD_DIM:(h + 1) * HEAD_DIM, :], probs[h], preferred_element_type=F32)
                 for h in range(N_HEADS)]
        o_ref[...] = jnp.concatenate(heads, axis=0).T.astype(BF16)

    n_start_tiles = LEFT_CHUNKS * CHUNK // TQ_A

    @pl.when(t < n_start_tiles)
    def _():
        attend(True)

    @pl.when(t >= n_start_tiles)
    def _():
        attend(False)


def _mix_a(q_t, k, v_t, bias_tile_t, nb, s):
    nt = s // TQ_A
    back_idx = lambda b, t, back: b * nt + jnp.maximum(t - back, 0)
    rows = lambda back: pl.BlockSpec((TQ_A, GROUP), lambda b, t: (back_idx(b, t, back), 0))
    tr = lambda back: pl.BlockSpec((1, GROUP, TQ_A), lambda b, t: (back_idx(b, t, back), 0, 0))
    return pl.pallas_call(
        _mix_a_kernel,
        grid=(nb, nt),
        in_specs=[tr(0), rows(2), rows(1), rows(0), tr(2), tr(1), tr(0), _const_spec(bias_tile_t.shape)],
        out_specs=rows(0),
        out_shape=jax.ShapeDtypeStruct((nb * s, GROUP), BF16),
        compiler_params=_cparams(("parallel", "parallel")),
        name="mix_a_chunk_attn",
    )(q_t, k, k, k, v_t, v_t, v_t, bias_tile_t)


def _mix_b_kernel(qk_ref, v_ref, og_ref, gc_ref, gr_ref, cw_ref, cb_ref, on_ref, o_ref,
                  xs_ref, c_ref, n_ref, m_ref):
    c_idx = pl.program_id(1)

    @pl.when(c_idx == 0)
    def _():
        xs_ref[0:8, :] = jnp.zeros((8, 2 * GROUP), F32)
        c_ref[...] = jnp.zeros_like(c_ref)
        n_ref[...] = jnp.zeros_like(n_ref)
        m_ref[...] = jnp.zeros_like(m_ref)

    xs_ref[8:8 + LB, :] = qk_ref[...]
    acc = jnp.broadcast_to(cb_ref[...], (LB, 2 * GROUP))
    for j in range(CONV_WIDTH):
        start = 8 - (CONV_WIDTH - 1) + j
        acc = acc + xs_ref[start:start + LB, :] * cw_ref[j:j + 1, :]
    xs_ref[0:8, :] = xs_ref[LB:LB + 8, :]
    qk = acc * jax.nn.sigmoid(acc)
    q = qk[:, :GROUP]
    k = qk[:, GROUP:] * (HEAD_DIM ** -0.5)
    qb = q.astype(BF16)
    kb = k.astype(BF16)
    v = v_ref[...]

    r = lax.broadcasted_iota(jnp.int32, (LB, LB), 0)
    c = lax.broadcasted_iota(jnp.int32, (LB, LB), 1)
    causal = c <= r
    tri = jnp.where(causal, 1.0, 0.0).astype(BF16)
    tri_t = jnp.where(r <= c, 1.0, 0.0).astype(BF16)
    gcol = gc_ref[...]
    grow = gr_ref[...]
    bcum_col = _split_dot(gcol, tri, 3, x_is_lhs=False)
    bcum_row = _split_dot(grow, tri_t, 3)

    c_state = c_ref[...]
    n_state = n_ref[...]
    q_c = jnp.dot(qb, c_state.astype(BF16), preferred_element_type=F32)
    qn = qb.astype(F32) * n_state.astype(BF16).astype(F32)
    q_n = _split_dot(qn, _group_ones(HEAD_DIM), 2)

    lane = lax.broadcasted_iota(jnp.int32, (LB, GROUP), 1) // HEAD_DIM
    num = jnp.zeros((LB, GROUP), F32)
    den = jnp.zeros((LB, GROUP), F32)
    floor = jnp.zeros((LB, GROUP), F32)
    wg_full = jnp.zeros((LB, GROUP), F32)
    a_full = jnp.zeros((1, GROUP), F32)
    lane1 = lax.broadcasted_iota(jnp.int32, (1, GROUP), 1) // HEAD_DIM
    a_list = []
    for h in range(N_HEADS):
        i_col = gcol[:, h:h + 1]
        i_row = grow[h:h + 1, :]
        b_col = bcum_col[:, N_HEADS + h:N_HEADS + h + 1]
        b_row = bcum_row[N_HEADS + h:N_HEADS + h + 1, :]
        m_prev = m_ref[h:h + 1, 0:1]
        dmat = jnp.where(causal, b_col - b_row + i_row, NEG)
        inter = b_col + m_prev
        m_t = jnp.maximum(inter, jnp.max(dmat, axis=-1, keepdims=True))
        w_intra = jnp.exp(dmat - m_t)
        s_inter = jnp.exp(inter - m_t)
        qh = jnp.where(_head_mask(qb.shape, h), qb, jnp.zeros_like(qb))
        sc = lax.dot_general(qh, kb, (((1,), (1,)), ((), ())), preferred_element_type=F32) * w_intra
        vh = jnp.where(_head_mask(v.shape, h), v, jnp.zeros_like(v))
        num_h = s_inter * q_c + jnp.dot(sc.astype(BF16), vh, preferred_element_type=F32)
        den_h = s_inter * q_n + jnp.sum(sc, axis=-1, keepdims=True)
        sel = lane == h
        num = jnp.where(sel, num_h, num)
        den = jnp.where(sel, den_h, den)
        floor = jnp.where(sel, jnp.exp(-m_t), floor)
        b_tot = b_col[LB - 1:LB, :]
        g = b_tot - b_col + i_col
        m_new = jnp.maximum(b_tot + m_prev, jnp.max(g, axis=0, keepdims=True))
        a_h = jnp.exp(b_tot + m_prev - m_new)
        a_list.append(a_h)
        a_full = jnp.where(lane1 == h, a_h, a_full)
        wg_full = jnp.where(sel, jnp.exp(g - m_new), wg_full)
        m_ref[h:h + 1, :] = jnp.broadcast_to(m_new, (1, m_ref.shape[1]))

    hb = num / jnp.maximum(jnp.abs(den), floor)
    o_ref[...] = (_group_rms(hb, on_ref[...], HEAD_DIM) * og_ref[...]).astype(BF16)

    kw = k * wg_full
    kv = jnp.dot(kw.T.astype(BF16), v, preferred_element_type=F32)
    r2 = lax.broadcasted_iota(jnp.int32, (GROUP, GROUP), 0) // HEAD_DIM
    c2 = lax.broadcasted_iota(jnp.int32, (GROUP, GROUP), 1) // HEAD_DIM
    a_mat = jnp.zeros((GROUP, GROUP), F32)
    for h in range(N_HEADS):
        a_mat = jnp.where(r2 == h, a_list[h], a_mat)
    c_ref[...] = a_mat * c_state + jnp.where(r2 == c2, kv, 0.0)
    n_ref[...] = a_full * n_state + jnp.sum(kw, axis=0, keepdims=True)


def _mix_b(bqk, bv, bo, gcol, grow, conv_w, conv_b, out_norm, nb, s):
    nc = s // LB
    row = lambda w: pl.BlockSpec((LB, w), lambda b, c: (b * nc + c, 0))
    return pl.pallas_call(
        _mix_b_kernel,
        grid=(nb, nc),
        in_specs=[row(2 * GROUP), row(GROUP), row(GROUP), row(GATE_LANES),
                  pl.BlockSpec((GATE_ROWS, LB), lambda b, c: (0, b * nc + c)),
                  _const_spec((CONV_WIDTH, 2 * GROUP)), _const_spec((1, 2 * GROUP)),
                  _const_spec((1, GROUP))],
        out_specs=row(GROUP),
        out_shape=jax.ShapeDtypeStruct((nb * s, GROUP), BF16),
        scratch_shapes=[pltpu.VMEM((LB + 8, 2 * GROUP), F32),
                        pltpu.VMEM((GROUP, GROUP), F32),
                        pltpu.VMEM((1, GROUP), F32),
                        pltpu.VMEM((8, 128), F32)],
        compiler_params=_cparams(("parallel", "arbitrary")),
        name="mix_b_mlstm",
    )(bqk, bv, bo, gcol, grow, conv_w, conv_b, out_norm)


def _mix_c_kernel(lam_ref, qt_ref, k_ref, vt_ref, on_ref, o_ref, p_ref, acc_ref, m_ref, al_ref, *, lam_init):
    qt = pl.program_id(1)
    q_t = qt_ref[0]
    n_maps = 2 * N_HEADS
    acc_ref[...] = jnp.zeros(acc_ref.shape, F32)

    def key_tile(kt):
        return k_ref[pl.ds(pl.multiple_of(kt * TK_C, TK_C), TK_C), :]

    def score_map(k, j):
        ch = slice(j * DIFF_QK_DIM, (j + 1) * DIFF_QK_DIM)
        return jnp.dot(k[:, ch], q_t[ch, :], preferred_element_type=F32)

    def softmax(sc, slot, masked=False):
        if masked:
            key_chunk = lax.broadcasted_iota(jnp.int32, (TK_C, TQ_C), 0) // CHUNK
            q_chunk = lax.broadcasted_iota(jnp.int32, (TK_C, TQ_C), 1) // CHUNK
            vis = key_chunk <= q_chunk
        for j in range(n_maps):
            s = jnp.where(vis, sc[j], NEG) if masked else sc[j]
            m_old = m_ref[j]
            m_new = jnp.maximum(m_old, jnp.max(s, axis=0, keepdims=True))
            al_ref[j] = jnp.exp2(m_old - m_new)
            m_ref[j] = m_new
            p_ref[slot, j] = jnp.exp2(s - m_new).astype(BF16)

    def pv_update(kt, slot):
        v_ext = vt_ref[kt]
        for j in range(n_maps):
            h = j // 2
            pv = jnp.dot(v_ext[h * VROWS:(h + 1) * VROWS, :], p_ref[slot, j], preferred_element_type=F32)
            acc_ref[j] = al_ref[j] * acc_ref[j] + pv

    def step(kt, prev, slot):
        k = key_tile(kt)
        v_ext = vt_ref[prev]
        sc = []
        for j in range(n_maps):
            h = j // 2
            sc.append(score_map(k, j))
            pv = jnp.dot(v_ext[h * VROWS:(h + 1) * VROWS, :], p_ref[slot, j], preferred_element_type=F32)
            acc_ref[j] = al_ref[j] * acc_ref[j] + pv
        softmax(sc, 1 - slot)

    m_ref[...] = jnp.full(m_ref.shape, NEG, F32)
    k_diag = key_tile(qt)
    softmax([score_map(k_diag, j) for j in range(n_maps)], 0, masked=True)
    n_trips = qt // C_STEPS_PER_TRIP

    def body(it, carry):
        a = C_STEPS_PER_TRIP * it
        step(a, jnp.where(it == 0, qt, a - 1), 0)
        for n in range(1, C_STEPS_PER_TRIP):
            step(a + n, a + n - 1, n % 2)
        return carry

    lax.fori_loop(0, n_trips, body, 0)
    done = C_STEPS_PER_TRIP * n_trips
    rem = qt - done
    last = jnp.where(n_trips > 0, done - 1, qt)

    for n in range(C_STEPS_PER_TRIP - 1):
        @pl.when(rem > n)
        def _(n=n):
            step(done + n, last if n == 0 else done + n - 1, n % 2)

    final = jnp.where(rem == 0, last, done + rem - 1)
    for slot in range(2):
        @pl.when(rem % 2 == slot)
        def _(slot=slot):
            pv_update(final, slot)

    lam = lam_ref[0]
    heads = []
    for h in range(N_HEADS):
        a0 = acc_ref[2 * h]
        a1 = acc_ref[2 * h + 1]
        o_h = (a0[:HEAD_DIM] * (1.0 / a0[HEAD_DIM:HEAD_DIM + 1])
               - lam * (a1[:HEAD_DIM] * (1.0 / a1[HEAD_DIM:HEAD_DIM + 1])))
        ms = jnp.mean(o_h * o_h, axis=0, keepdims=True)
        heads.append(o_h * lax.rsqrt(ms + EPS))
    out_t = jnp.concatenate(heads, axis=0) * (on_ref[...] * (1.0 - lam_init))
    o_ref[...] = out_t.T.astype(BF16)


def _mix_c(lam, q_t, k, v_t, out_norm_col, nb, s, lam_init):
    nt = s // TQ_C
    tile_t = pl.BlockSpec((1, GROUP, TQ_C), lambda b, t, lam_ref: (b * nt + t, 0, 0))
    full_k = pl.BlockSpec((s, GROUP), lambda b, t, lam_ref: (b, 0), pipeline_mode=pl.Buffered(1))
    full_vt = pl.BlockSpec((nt, N_HEADS * VROWS, TK_C), lambda b, t, lam_ref: (b, 0, 0),
                           pipeline_mode=pl.Buffered(1))
    grid_spec = pltpu.PrefetchScalarGridSpec(
        num_scalar_prefetch=1,
        grid=(nb, nt),
        in_specs=[tile_t, full_k, full_vt,
                  pl.BlockSpec((GROUP, 1), lambda b, t, lam_ref: (0, 0), pipeline_mode=pl.Buffered(1))],
        out_specs=pl.BlockSpec((TQ_C, GROUP), lambda b, t, lam_ref: (b * nt + t, 0)),
        scratch_shapes=[pltpu.VMEM((2, 2 * N_HEADS, TK_C, TQ_C), BF16),
                        pltpu.VMEM((2 * N_HEADS, VROWS, TQ_C), F32),
                        pltpu.VMEM((2 * N_HEADS, 1, TQ_C), F32),
                        pltpu.VMEM((2 * N_HEADS, 1, TQ_C), F32)],
    )
    return pl.pallas_call(
        functools.partial(_mix_c_kernel, lam_init=lam_init),
        grid_spec=grid_spec,
        out_shape=jax.ShapeDtypeStruct((nb * s, GROUP), BF16),
        compiler_params=_cparams(("parallel", "parallel")),
        name="mix_c_diff_attn",
    )(lam, q_t, k, v_t, out_norm_col)


def _mix_d_kernel(qt_ref, k_ref, vt_ref, o_ref, acc_ref, run_ref):
    qt = pl.program_id(1)
    q_t = qt_ref[0]
    acc_ref[...] = jnp.zeros(acc_ref.shape, F32)
    run_ref[...] = jnp.zeros(run_ref.shape, F32)
    r = lax.broadcasted_iota(jnp.int32, (TK_D, TK_D), 0)
    c = lax.broadcasted_iota(jnp.int32, (TK_D, TK_D), 1)
    later = jnp.where(c > r, 1.0, 0.0).astype(BF16)

    def tile(kt, masked):
        start = pl.multiple_of(kt * TK_D, TK_D)
        k = k_ref[pl.ds(start, TK_D), :]
        v_t = vt_ref[kt]
        zs = [jnp.dot(k[:, h * HEAD_DIM:(h + 1) * HEAD_DIM], q_t[h * HEAD_DIM:(h + 1) * HEAD_DIM, :],
                      preferred_element_type=F32) for h in range(N_HEADS)]
        if masked:
            before = (lax.broadcasted_iota(jnp.int32, (TK_D, TQ_D), 0)
                      < lax.broadcasted_iota(jnp.int32, (TK_D, TQ_D), 1))
        worst = None
        log_take, between = [], []
        for h in range(N_HEADS):
            z = zs[h]
            log_keep = jnp.minimum(-z, 0.0) - jnp.log2(1.0 + jnp.exp2(-jnp.abs(z)))
            log_take.append(z + log_keep)
            if masked:
                log_keep = jnp.where(before, log_keep, 0.0)
            run = run_ref[h]
            between.append(run + _split_dot(log_keep, later, 2, x_is_lhs=False))
            run = run + jnp.sum(log_keep, axis=0, keepdims=True)
            run_ref[h] = run
            top = jnp.max(run)
            worst = top if worst is None else jnp.maximum(worst, top)
        for h in range(N_HEADS):
            a = jnp.exp2(log_take[h] + between[h])
            if masked:
                a = jnp.where(before, a, 0.0)
            acc_ref[h] += jnp.dot(v_t[h * HEAD_DIM:(h + 1) * HEAD_DIM, :], a.astype(BF16),
                                  preferred_element_type=F32)
        return worst

    worst = tile(qt, True)

    def cond(state):
        kt, worst = state
        return jnp.logical_and(kt >= 0, worst > SB_SKIP_LOG2)

    def body(state):
        kt, _ = state
        return kt - 1, tile(kt, False)

    lax.while_loop(cond, body, (qt - 1, worst))
    out_t = jnp.concatenate([acc_ref[h] for h in range(N_HEADS)], axis=0)
    o_ref[...] = out_t.T.astype(BF16)


def _mix_d(q_t, k, v_t, nb, s):
    nt = s // TQ_D
    tile_t = pl.BlockSpec((1, GROUP, TQ_D), lambda b, t: (b * nt + t, 0, 0))
    full_k = pl.BlockSpec((s, GROUP), lambda b, t: (b, 0), pipeline_mode=pl.Buffered(1))
    full_vt = pl.BlockSpec((nt, GROUP, TK_D), lambda b, t: (b, 0, 0), pipeline_mode=pl.Buffered(1))
    return pl.pallas_call(
        _mix_d_kernel,
        grid=(nb, nt),
        in_specs=[tile_t, full_k, full_vt],
        out_specs=pl.BlockSpec((TQ_D, GROUP), lambda b, t: (b * nt + t, 0)),
        out_shape=jax.ShapeDtypeStruct((nb * s, GROUP), BF16),
        scratch_shapes=[pltpu.VMEM((N_HEADS, HEAD_DIM, TQ_D), F32),
                        pltpu.VMEM((N_HEADS, 1, TQ_D), F32)],
        compiler_params=_cparams(("parallel", "parallel")),
        name="mix_d_stick_breaking",
    )(q_t, k, v_t)


def _rel_bias_tile(rel_bias):
    nk = TQ_A + LEFT_CHUNKS * CHUNK
    i = np.arange(TQ_A)[:, None]
    j = np.arange(nk)[None, :]
    dc = j // CHUNK - i // CHUNK
    visible = (dc >= 0) & (dc <= LEFT_CHUNKS)
    p = TQ_A + nk
    d = np.arange(p)
    rel = d - (nk - 1) + LEFT_CHUNKS * CHUNK
    idx = np.clip(rel, -REL_CLIP, REL_CLIP) + REL_CLIP
    diag = rel_bias[:, idx].astype(F32)
    nh = rel_bias.shape[0]
    skew = jnp.tile(diag, (1, TQ_A + 1))[:, :TQ_A * (p + 1)].reshape(nh, TQ_A, p + 1)
    toeplitz = skew[:, :, :nk][:, :, ::-1]
    return jnp.swapaxes(jnp.where(jnp.asarray(visible)[None], toeplitz * LOG2E, NEG), 1, 2)


def _permute_w_in(w_in):
    sizes = [GROUP, GROUP, GROUP, 2 * GROUP, GROUP, GROUP, N_HEADS, N_HEADS,
             GROUP, GROUP, GROUP, GROUP, GROUP, GROUP]
    offs = np.concatenate([[0], np.cumsum(sizes)])
    seg = [w_in[:, offs[n]:offs[n + 1]] for n in range(len(sizes))]
    main = jnp.concatenate([seg[1], seg[3], seg[4], seg[5], seg[9], seg[12]], axis=1).astype(BF16)
    gates = jnp.concatenate([seg[6], seg[7]], axis=1)
    g_col = jnp.pad(gates, ((0, 0), (0, GATE_LANES - 2 * N_HEADS))).astype(BF16)
    g_row = jnp.pad(gates.T, ((0, GATE_ROWS - 2 * N_HEADS), (0, 0)))
    c_t = jnp.concatenate([seg[n].T for n in (8, 10, 11, 13, 0, 2)] + [g_row], axis=0).astype(BF16)
    return main, c_t, g_col


def _layer(x, nb, s, lam_init, p):
    x = _ffn(x, p["ffn1_norm"], p["ffn1_wg"].astype(BF16), p["ffn1_wu"].astype(BF16), p["ffn1_wd"].astype(BF16))

    w_main, w_ct, w_gc = _permute_w_in(p["w_in"])
    gate_bias = p["b_gate_bias"].reshape(2 * N_HEADS).astype(F32)
    gb_col = jnp.pad(gate_bias, (0, GATE_LANES - 2 * N_HEADS)).reshape(1, GATE_LANES)
    gb_row = jnp.pad(gate_bias, (0, GATE_ROWS - 2 * N_HEADS)).reshape(GATE_ROWS, 1)
    tile4 = lambda g: jnp.tile(g.astype(F32), GROUP // g.shape[0]).reshape(1, GROUP)
    (aq_t, ak, av_t, bqk, bv, bo, gcol, grow, cq_t, ck, cv_t, dq_t, dk, dv_t) = _proj(
        x, p["mix_norm"], w_main, w_ct, w_gc, gb_col, gb_row,
        tile4(p["a_q_norm"]).reshape(GROUP, 1), tile4(p["a_k_norm"]),
        tile4(p["c_q_norm"]).reshape(GROUP, 1), tile4(p["c_k_norm"]))

    ya = _mix_a(aq_t, ak, av_t, _rel_bias_tile(p["a_rel_bias"]), nb, s)
    yb = _mix_b(bqk, bv, bo, gcol, grow, p["b_conv_w"].astype(F32), p["b_conv_b"].astype(F32).reshape(1, -1),
                p["b_out_norm"].astype(F32).reshape(1, GROUP), nb, s)
    lv = p["c_lambda"].astype(F32)
    lam = jnp.exp(jnp.sum(lv[0] * lv[1])) - jnp.exp(jnp.sum(lv[2] * lv[3])) + lam_init
    yc = _mix_c(lam.reshape(1), cq_t, ck, cv_t, tile4(p["c_out_norm"]).reshape(GROUP, 1), nb, s, lam_init)
    yd = _mix_d(dq_t, dk, dv_t, nb, s)

    return _mix_out_ffn(x, ya, yb, yc, yd, p["w_out"].astype(BF16), p["ffn2_norm"], p["ffn2_wg"].astype(BF16),
                        p["ffn2_wu"].astype(BF16), p["ffn2_wd"].astype(BF16))


_PARAM_NAMES = ("ffn1_norm", "ffn1_wg", "ffn1_wu", "ffn1_wd", "mix_norm", "w_in", "a_q_norm", "a_k_norm",
                "a_rel_bias", "b_conv_w", "b_conv_b", "b_gate_bias", "b_out_norm", "c_q_norm", "c_k_norm",
                "c_lambda", "c_out_norm", "w_out", "ffn2_norm", "ffn2_wg", "ffn2_wu", "ffn2_wd")


def kernel(x, ffn1_norm, ffn1_wg, ffn1_wu, ffn1_wd, mix_norm, w_in, a_q_norm, a_k_norm, a_rel_bias,
           b_conv_w, b_conv_b, b_gate_bias, b_out_norm, c_q_norm, c_k_norm, c_lambda, c_out_norm,
           w_out, ffn2_norm, ffn2_wg, ffn2_wu, ffn2_wd):
    params = dict(zip(_PARAM_NAMES, (ffn1_norm, ffn1_wg, ffn1_wu, ffn1_wd, mix_norm, w_in, a_q_norm, a_k_norm,
                                     a_rel_bias, b_conv_w, b_conv_b, b_gate_bias, b_out_norm, c_q_norm,
                                     c_k_norm, c_lambda, c_out_norm, w_out, ffn2_norm, ffn2_wg, ffn2_wu,
                                     ffn2_wd)))
    nb, s, d = x.shape
    depth = ffn1_norm.shape[0]
    h = x.reshape(nb * s, d)
    for l in range(depth):
        lam_init = 0.8 - 0.6 * math.exp(-0.3 * l)
        h = _layer(h, nb, s, lam_init, {k: v[l] for k, v in params.items()})
    return h.reshape(nb, s, d)
```

```python
import functools
import math

import jax
import jax.numpy as jnp
import numpy as np
from jax import lax
from jax.experimental import pallas as pl
from jax.experimental.pallas import tpu as pltpu

F32 = jnp.float32
BF16 = jnp.bfloat16

D_MODEL = 1024
D_FF = 2816
CHUNK = 64
HEAD_DIM = 64
N_HEADS = 4
GROUP = N_HEADS * HEAD_DIM
LEFT_CHUNKS = 8
REL_CLIP = 128
CONV_WIDTH = 4
DIFF_QK_DIM = HEAD_DIM // 2
EPS = 1e-6
NEG = -1e30
LOG2E = 1.4426950408889634

OFF_AK = 0
OFF_BQK, OFF_BV, OFF_BO = 256, 768, 1024
OFF_CK = 1280
OFF_DK = 1536
MAIN_COLS = 1792
N_T_GROUPS = 6
GATE_LANES = 128
GATE_ROWS = 16
ONES_ROWS = 16
VROWS = HEAD_DIM + ONES_ROWS

VMEM_LIMIT = 56 * 1024 * 1024

TM_FFN = 512
TM_PROJ = 512
TILE_T = 256
TQ_A = 256
LB = 256
TQ_C = 256
TK_C = 256
C_STEPS_PER_TRIP = 8
TQ_D = 256
TK_D = 256
SB_SKIP_LOG2 = -160.0


def _cparams(sem):
    return pltpu.CompilerParams(dimension_semantics=sem, vmem_limit_bytes=VMEM_LIMIT)


def _const_spec(shape):
    nd = len(shape)
    return pl.BlockSpec(shape, lambda *_: (0,) * nd, pipeline_mode=pl.Buffered(1))


def _split_dot(x, mat, terms, x_is_lhs=True):
    acc = None
    rem = x
    for term in range(terms):
        part = rem.astype(BF16)
        if term + 1 < terms:
            rem = rem - part.astype(F32)
        d = (jnp.dot(part, mat, preferred_element_type=F32) if x_is_lhs
             else jnp.dot(mat, part, preferred_element_type=F32))
        acc = d if acc is None else acc + d
    return acc


def _group_ones(width):
    r = lax.broadcasted_iota(jnp.int32, (GROUP, GROUP), 0) // width
    c = lax.broadcasted_iota(jnp.int32, (GROUP, GROUP), 1) // width
    return jnp.where(r == c, 1.0, 0.0).astype(BF16)


def _group_rms(x, gain, width):
    ss = _split_dot(x * x, _group_ones(width), 2)
    return x * lax.rsqrt(ss * (1.0 / width) + EPS) * gain


def _log_sigmoid(x):
    return jnp.minimum(x, 0.0) - jnp.log1p(jnp.exp(-jnp.abs(x)))


def _head_mask(shape, h, axis=1, width=HEAD_DIM):
    lane = lax.broadcasted_iota(jnp.int32, shape, axis)
    return (lane // width) == h


def _ffn_half_step(x, g_ref, wg_ref, wu_ref, wd_ref):
    xn = x * lax.rsqrt(jnp.mean(x * x, axis=-1, keepdims=True) + EPS) * g_ref[...]
    xb = xn.astype(BF16)
    g = jnp.dot(xb, wg_ref[...], preferred_element_type=F32)
    u = jnp.dot(xb, wu_ref[...], preferred_element_type=F32)
    h = (g * jax.nn.sigmoid(g) * u).astype(BF16)
    return x + 0.5 * jnp.dot(h, wd_ref[...], preferred_element_type=F32)


def _ffn_kernel(x_ref, g_ref, wg_ref, wu_ref, wd_ref, o_ref):
    o_ref[...] = _ffn_half_step(x_ref[...], g_ref, wg_ref, wu_ref, wd_ref)


def _mix_out_ffn_kernel(x_ref, ya_ref, yb_ref, yc_ref, yd_ref, wo_ref, g_ref, wg_ref, wu_ref, wd_ref, o_ref):
    x = x_ref[...]
    for grp, y_ref in enumerate((ya_ref, yb_ref, yc_ref, yd_ref)):
        x = x + jnp.dot(y_ref[...], wo_ref[grp * GROUP:(grp + 1) * GROUP, :], preferred_element_type=F32)
    o_ref[...] = _ffn_half_step(x, g_ref, wg_ref, wu_ref, wd_ref)


def _ffn_specs():
    return [_const_spec((1, D_MODEL)), _const_spec((D_MODEL, D_FF)), _const_spec((D_MODEL, D_FF)),
            _const_spec((D_FF, D_MODEL))]


def _ffn(x, gain, wg, wu, wd):
    t = x.shape[0]
    tm = min(TM_FFN, t)
    row = pl.BlockSpec((tm, D_MODEL), lambda i: (i, 0))
    return pl.pallas_call(
        _ffn_kernel,
        grid=(t // tm,),
        in_specs=[row] + _ffn_specs(),
        out_specs=row,
        out_shape=jax.ShapeDtypeStruct((t, D_MODEL), F32),
        compiler_params=_cparams(("parallel",)),
        name="ffn_half_step",
    )(x, gain.reshape(1, D_MODEL), wg, wu, wd)


def _mix_out_ffn(x, ya, yb, yc, yd, w_out, gain, wg, wu, wd):
    t = x.shape[0]
    tm = min(TM_FFN, t)
    row = lambda w: pl.BlockSpec((tm, w), lambda i: (i, 0))
    return pl.pallas_call(
        _mix_out_ffn_kernel,
        grid=(t // tm,),
        in_specs=[row(D_MODEL), row(GROUP), row(GROUP), row(GROUP), row(GROUP),
                  _const_spec((D_MODEL, D_MODEL))] + _ffn_specs(),
        out_specs=row(D_MODEL),
        out_shape=jax.ShapeDtypeStruct((t, D_MODEL), F32),
        compiler_params=_cparams(("parallel",)),
        name="mix_out_proj_ffn",
    )(x, ya, yb, yc, yd, w_out, gain.reshape(1, D_MODEL), wg, wu, wd)


def _proj_kernel(x_ref, g_ref, w_ref, wct_ref, wgc_ref, gbc_ref, gbr_ref,
                 aqn_ref, akn_ref, cqn_ref, ckn_ref,
                 aq_ref, ak_ref, av_ref, bqk_ref, bv_ref, bo_ref, gc_ref, gr_ref,
                 cq_ref, ck_ref, cv_ref, dq_ref, dk_ref, dv_ref):
    x = x_ref[...]
    hn = x * lax.rsqrt(jnp.mean(x * x, axis=-1, keepdims=True) + EPS) * g_ref[...]
    hb = hn.astype(BF16)

    def cols(off, width=GROUP):
        return jnp.dot(hb, w_ref[:, off:off + width], preferred_element_type=F32)

    nt_dims = (((1,), (1,)), ((), ()))

    all_t = lax.dot_general(wct_ref[...], hb, nt_dims, preferred_element_type=F32)

    def rows_t(idx):
        return all_t[idx * GROUP:(idx + 1) * GROUP, :]

    def group_rms_t(x_t, gain_col, width):
        ss = _split_dot(x_t * x_t, _group_ones(width), 2, x_is_lhs=False)
        return x_t * lax.rsqrt(ss * (1.0 / width) + EPS) * gain_col

    aqt = (group_rms_t(rows_t(4), aqn_ref[...], HEAD_DIM) * (HEAD_DIM ** -0.5 * LOG2E)).astype(BF16)
    ak_ref[...] = _group_rms(cols(OFF_AK), akn_ref[...], HEAD_DIM).astype(BF16)
    avt = rows_t(5).astype(BF16)
    bqk_ref[...] = cols(OFF_BQK, 2 * GROUP)
    bv_ref[...] = cols(OFF_BV).astype(BF16)
    bo_ref[...] = jax.nn.sigmoid(cols(OFF_BO))
    gcol = jnp.dot(hb, wgc_ref[...], preferred_element_type=F32) + gbc_ref[...]
    lane = lax.broadcasted_iota(jnp.int32, gcol.shape, 1)
    gc_ref[...] = jnp.where(lane >= N_HEADS, _log_sigmoid(gcol), gcol)
    grow = all_t[N_T_GROUPS * GROUP:, :] + gbr_ref[...]
    row = lax.broadcasted_iota(jnp.int32, grow.shape, 0)
    gr_ref[...] = jnp.where(row >= N_HEADS, _log_sigmoid(grow), grow)
    ck_ref[...] = _group_rms(cols(OFF_CK), ckn_ref[...], DIFF_QK_DIM).astype(BF16)
    cqt = (group_rms_t(rows_t(0), cqn_ref[...], DIFF_QK_DIM) * (DIFF_QK_DIM ** -0.5 * LOG2E)).astype(BF16)
    cvt = rows_t(1).astype(BF16)
    dk_ref[...] = cols(OFF_DK).astype(BF16)
    dqt = (rows_t(2) * (HEAD_DIM ** -0.5 * LOG2E)).astype(BF16)
    dvt = rows_t(3).astype(BF16)
    for sub in range(cq_ref.shape[0]):
        tok = slice(sub * TILE_T, (sub + 1) * TILE_T)
        aq_ref[sub] = aqt[:, tok]
        av_ref[sub] = avt[:, tok]
        cq_ref[sub] = cqt[:, tok]
        dq_ref[sub] = dqt[:, tok]
        dv_ref[sub] = dvt[:, tok]
        for h in range(N_HEADS):
            cv_ref[sub, h * VROWS:h * VROWS + HEAD_DIM, :] = cvt[h * HEAD_DIM:(h + 1) * HEAD_DIM, tok]
            cv_ref[sub, h * VROWS + HEAD_DIM:(h + 1) * VROWS, :] = jnp.ones((ONES_ROWS, TILE_T), BF16)


def _proj(x, gain, w_main, w_ct, w_gc, gb_col, gb_row, aqn, akn, cqn_col, ckn):
    t = x.shape[0]
    tm = TM_PROJ
    row_spec = lambda w: pl.BlockSpec((tm, w), lambda i: (i, 0))
    bf = lambda w: jax.ShapeDtypeStruct((t, w), BF16)
    f32 = lambda w: jax.ShapeDtypeStruct((t, w), F32)
    n_sub = tm // TILE_T
    tr_shape = jax.ShapeDtypeStruct((t // TILE_T, GROUP, TILE_T), BF16)
    tr_spec = pl.BlockSpec((n_sub, GROUP, TILE_T), lambda i: (i, 0, 0))
    vt_shape = jax.ShapeDtypeStruct((t // TILE_T, N_HEADS * VROWS, TILE_T), BF16)
    vt_spec = pl.BlockSpec((n_sub, N_HEADS * VROWS, TILE_T), lambda i: (i, 0, 0))
    out_shape = [tr_shape, bf(GROUP), tr_shape,
                 f32(2 * GROUP), bf(GROUP), f32(GROUP),
                 f32(GATE_LANES), jax.ShapeDtypeStruct((GATE_ROWS, t), F32),
                 tr_shape, bf(GROUP), vt_shape,
                 tr_shape, bf(GROUP), tr_shape]
    out_specs = ([tr_spec, row_spec(GROUP), tr_spec, row_spec(2 * GROUP), row_spec(GROUP), row_spec(GROUP),
                  row_spec(GATE_LANES), pl.BlockSpec((GATE_ROWS, tm), lambda i: (0, i))]
                 + [tr_spec, row_spec(GROUP), vt_spec, tr_spec, row_spec(GROUP), tr_spec])
    return pl.pallas_call(
        _proj_kernel,
        grid=(t // tm,),
        in_specs=[row_spec(D_MODEL), _const_spec((1, D_MODEL)),
                  _const_spec((D_MODEL, MAIN_COLS)), _const_spec((N_T_GROUPS * GROUP + GATE_ROWS, D_MODEL)),
                  _const_spec((D_MODEL, GATE_LANES)), _const_spec((1, GATE_LANES)),
                  _const_spec((GATE_ROWS, 1)),
                  _const_spec((GROUP, 1)), _const_spec((1, GROUP)),
                  _const_spec((GROUP, 1)), _const_spec((1, GROUP))],
        out_specs=out_specs,
        out_shape=out_shape,
        compiler_params=_cparams(("parallel",)),
        name="mix_in_proj",
    )(x, gain.reshape(1, D_MODEL), w_main, w_ct, w_gc, gb_col, gb_row, aqn, akn, cqn_col, ckn)


def _mix_a_kernel(qt_ref, k0_ref, k1_ref, k2_ref, v0_ref, v1_ref, v2_ref, bias_ref, o_ref):
    t = pl.program_id(1)
    q_t = qt_ref[0]
    k = jnp.concatenate([k0_ref[...], k1_ref[...], k2_ref[...]], axis=0)
    v_t = jnp.concatenate([v0_ref[0], v1_ref[0], v2_ref[0]], axis=1)
    nk = k.shape[0]

    def attend(mask_start):
        scores = [jnp.dot(k[:, h * HEAD_DIM:(h + 1) * HEAD_DIM], q_t[h * HEAD_DIM:(h + 1) * HEAD_DIM, :],
                          preferred_element_type=F32) for h in range(N_HEADS)]
        if mask_start:
            key_pos = t * TQ_A - LEFT_CHUNKS * CHUNK + lax.broadcasted_iota(jnp.int32, (nk, TQ_A), 0)
            valid = key_pos >= 0
        probs = []
        for h in range(N_HEADS):
            s = scores[h] + bias_ref[h]
            if mask_start:
                s = jnp.where(valid, s, NEG)
            e = jnp.exp2(s - jnp.max(s, axis=0, keepdims=True))
            probs.append((e * (1.0 / jnp.sum(e, axis=0, keepdims=True))).astype(BF16))
        heads = [jnp.dot(v_t[h * HEAD_DIM:(h + 1) * HEAD_DIM, :], probs[h], preferred_element_type=F32)
                 for h in range(N_HEADS)]
        o_ref[...] = jnp.concatenate(heads, axis=0).T.astype(BF16)

    n_start_tiles = LEFT_CHUNKS * CHUNK // TQ_A

    @pl.when(t < n_start_tiles)
    def _():
        attend(True)

    @pl.when(t >= n_start_tiles)
    def _():
        attend(False)


def _mix_a(q_t, k, v_t, bias_tile_t, nb, s):
    nt = s // TQ_A
    back_idx = lambda b, t, back: b * nt + jnp.maximum(t - back, 0)
    rows = lambda back: pl.BlockSpec((TQ_A, GROUP), lambda b, t: (back_idx(b, t, back), 0))
    tr = lambda back: pl.BlockSpec((1, GROUP, TQ_A), lambda b, t: (back_idx(b, t, back), 0, 0))
    return pl.pallas_call(
        _mix_a_kernel,
        grid=(nb, nt),
        in_specs=[tr(0), rows(2), rows(1), rows(0), tr(2), tr(1), tr(0), _const_spec(bias_tile_t.shape)],
        out_specs=rows(0),
        out_shape=jax.ShapeDtypeStruct((nb * s, GROUP), BF16),
        compiler_params=_cparams(("parallel", "parallel")),
        name="mix_a_chunk_attn",
    )(q_t, k, k, k, v_t, v_t, v_t, bias_tile_t)


def _mix_b_kernel(qk_ref, v_ref, og_ref, gc_ref, gr_ref, cw_ref, cb_ref, on_ref, o_ref,
                  xs_ref, c_ref, n_ref, m_ref):
    c_idx = pl.program_id(1)

    @pl.when(c_idx == 0)
    def _():
        xs_ref[0:8, :] = jnp.zeros((8, 2 * GROUP), F32)
        c_ref[...] = jnp.zeros_like(c_ref)
        n_ref[...] = jnp.zeros_like(n_ref)
        m_ref[...] = jnp.zeros_like(m_ref)

    xs_ref[8:8 + LB, :] = qk_ref[...]
    acc = jnp.broadcast_to(cb_ref[...], (LB, 2 * GROUP))
    for j in range(CONV_WIDTH):
        start = 8 - (CONV_WIDTH - 1) + j
        acc = acc + xs_ref[start:start + LB, :] * cw_ref[j:j + 1, :]
    xs_ref[0:8, :] = xs_ref[LB:LB + 8, :]
    qk = acc * jax.nn.sigmoid(acc)
    q = qk[:, :GROUP]
    k = qk[:, GROUP:] * (HEAD_DIM ** -0.5)
    qb = q.astype(BF16)
    kb = k.astype(BF16)
    v = v_ref[...]

    r = lax.broadcasted_iota(jnp.int32, (LB, LB), 0)
    c = lax.broadcasted_iota(jnp.int32, (LB, LB), 1)
    causal = c <= r
    tri = jnp.where(causal, 1.0, 0.0).astype(BF16)
    tri_t = jnp.where(r <= c, 1.0, 0.0).astype(BF16)
    gcol = gc_ref[...]
    grow = gr_ref[...]
    bcum_col = _split_dot(gcol, tri, 3, x_is_lhs=False)
    bcum_row = _split_dot(grow, tri_t, 3)

    c_state = c_ref[...]
    n_state = n_ref[...]
    q_c = jnp.dot(qb, c_state.astype(BF16), preferred_element_type=F32)
    qn = qb.astype(F32) * n_state.astype(BF16).astype(F32)
    q_n = _split_dot(qn, _group_ones(HEAD_DIM), 2)

    lane = lax.broadcasted_iota(jnp.int32, (LB, GROUP), 1) // HEAD_DIM
    num = jnp.zeros((LB, GROUP), F32)
    den = jnp.zeros((LB, GROUP), F32)
    floor = jnp.zeros((LB, GROUP), F32)
    wg_full = jnp.zeros((LB, GROUP), F32)
    a_full = jnp.zeros((1, GROUP), F32)
    lane1 = lax.broadcasted_iota(jnp.int32, (1, GROUP), 1) // HEAD_DIM
    a_list = []
    for h in range(N_HEADS):
        i_col = gcol[:, h:h + 1]
        i_row = grow[h:h + 1, :]
        b_col = bcum_col[:, N_HEADS + h:N_HEADS + h + 1]
        b_row = bcum_row[N_HEADS + h:N_HEADS + h + 1, :]
        m_prev = m_ref[h:h + 1, 0:1]
        dmat = jnp.where(causal, b_col - b_row + i_row, NEG)
        inter = b_col + m_prev
        m_t = jnp.maximum(inter, jnp.max(dmat, axis=-1, keepdims=True))
        w_intra = jnp.exp(dmat - m_t)
        s_inter = jnp.exp(inter - m_t)
        qh = jnp.where(_head_mask(qb.shape, h), qb, jnp.zeros_like(qb))
        sc = lax.dot_general(qh, kb, (((1,), (1,)), ((), ())), preferred_element_type=F32) * w_intra
        vh = jnp.where(_head_mask(v.shape, h), v, jnp.zeros_like(v))
        num_h = s_inter * q_c + jnp.dot(sc.astype(BF16), vh, preferred_element_type=F32)
        den_h = s_inter * q_n + jnp.sum(sc, axis=-1, keepdims=True)
        sel = lane == h
        num = jnp.where(sel, num_h, num)
        den = jnp.where(sel, den_h, den)
        floor = jnp.where(sel, jnp.exp(-m_t), floor)
        b_tot = b_col[LB - 1:LB, :]
        g = b_tot - b_col + i_col
        m_new = jnp.maximum(b_tot + m_prev, jnp.max(g, axis=0, keepdims=True))
        a_h = jnp.exp(b_tot + m_prev - m_new)
        a_list.append(a_h)
        a_full = jnp.where(lane1 == h, a_h, a_full)
        wg_full = jnp.where(sel, jnp.exp(g - m_new), wg_full)
        m_ref[h:h + 1, :] = jnp.broadcast_to(m_new, (1, m_ref.shape[1]))

    hb = num / jnp.maximum(jnp.abs(den), floor)
    o_ref[...] = (_group_rms(hb, on_ref[...], HEAD_DIM) * og_ref[...]).astype(BF16)

    kw = k * wg_full
    kv = jnp.dot(kw.T.astype(BF16), v, preferred_element_type=F32)
    r2 = lax.broadcasted_iota(jnp.int32, (GROUP, GROUP), 0) // HEAD_DIM
    c2 = lax.broadcasted_iota(jnp.int32, (GROUP, GROUP), 1) // HEAD_DIM
    a_mat = jnp.zeros((GROUP, GROUP), F32)
    for h in range(N_HEADS):
        a_mat = jnp.where(r2 == h, a_list[h], a_mat)
    c_ref[...] = a_mat * c_state + jnp.where(r2 == c2, kv, 0.0)
    n_ref[...] = a_full * n_state + jnp.sum(kw, axis=0, keepdims=True)


def _mix_b(bqk, bv, bo, gcol, grow, conv_w, conv_b, out_norm, nb, s):
    nc = s // LB
    row = lambda w: pl.BlockSpec((LB, w), lambda b, c: (b * nc + c, 0))
    return pl.pallas_call(
        _mix_b_kernel,
        grid=(nb, nc),
        in_specs=[row(2 * GROUP), row(GROUP), row(GROUP), row(GATE_LANES),
                  pl.BlockSpec((GATE_ROWS, LB), lambda b, c: (0, b * nc + c)),
                  _const_spec((CONV_WIDTH, 2 * GROUP)), _const_spec((1, 2 * GROUP)),
                  _const_spec((1, GROUP))],
        out_specs=row(GROUP),
        out_shape=jax.ShapeDtypeStruct((nb * s, GROUP), BF16),
        scratch_shapes=[pltpu.VMEM((LB + 8, 2 * GROUP), F32),
                        pltpu.VMEM((GROUP, GROUP), F32),
                        pltpu.VMEM((1, GROUP), F32),
                        pltpu.VMEM((8, 128), F32)],
        compiler_params=_cparams(("parallel", "arbitrary")),
        name="mix_b_mlstm",
    )(bqk, bv, bo, gcol, grow, conv_w, conv_b, out_norm)


def _mix_c_kernel(lam_ref, qt_ref, k_ref, vt_ref, on_ref, o_ref, p_ref, acc_ref, m_ref, al_ref, *, lam_init):
    qt = pl.program_id(1)
    q_t = qt_ref[0]
    n_maps = 2 * N_HEADS
    acc_ref[...] = jnp.zeros(acc_ref.shape, F32)

    def key_tile(kt):
        return k_ref[pl.ds(pl.multiple_of(kt * TK_C, TK_C), TK_C), :]

    def score_map(k, j):
        ch = slice(j * DIFF_QK_DIM, (j + 1) * DIFF_QK_DIM)
        return jnp.dot(k[:, ch], q_t[ch, :], preferred_element_type=F32)

    def softmax(sc, slot, masked=False):
        if masked:
            key_chunk = lax.broadcasted_iota(jnp.int32, (TK_C, TQ_C), 0) // CHUNK
            q_chunk = lax.broadcasted_iota(jnp.int32, (TK_C, TQ_C), 1) // CHUNK
            vis = key_chunk <= q_chunk
        for j in range(n_maps):
            s = jnp.where(vis, sc[j], NEG) if masked else sc[j]
            m_old = m_ref[j]
            m_new = jnp.maximum(m_old, jnp.max(s, axis=0, keepdims=True))
            al_ref[j] = jnp.exp2(m_old - m_new)
            m_ref[j] = m_new
            p_ref[slot, j] = jnp.exp2(s - m_new).astype(BF16)

    def pv_update(kt, slot):
        v_ext = vt_ref[kt]
        for j in range(n_maps):
            h = j // 2
            pv = jnp.dot(v_ext[h * VROWS:(h + 1) * VROWS, :], p_ref[slot, j], preferred_element_type=F32)
            acc_ref[j] = al_ref[j] * acc_ref[j] + pv

    def step(kt, prev, slot):
        k = key_tile(kt)
        v_ext = vt_ref[prev]
        sc = []
        for j in range(n_maps):
            h = j // 2
            sc.append(score_map(k, j))
            pv = jnp.dot(v_ext[h * VROWS:(h + 1) * VROWS, :], p_ref[slot, j], preferred_element_type=F32)
            acc_ref[j] = al_ref[j] * acc_ref[j] + pv
        softmax(sc, 1 - slot)

    m_ref[...] = jnp.full(m_ref.shape, NEG, F32)
    k_diag = key_tile(qt)
    softmax([score_map(k_diag, j) for j in range(n_maps)], 0, masked=True)
    n_trips = qt // C_STEPS_PER_TRIP

    def body(it, carry):
        a = C_STEPS_PER_TRIP * it
        step(a, jnp.where(it == 0, qt, a - 1), 0)
        for n in range(1, C_STEPS_PER_TRIP):
            step(a + n, a + n - 1, n % 2)
        return carry

    lax.fori_loop(0, n_trips, body, 0)
    done = C_STEPS_PER_TRIP * n_trips
    rem = qt - done
    last = jnp.where(n_trips > 0, done - 1, qt)

    for n in range(C_STEPS_PER_TRIP - 1):
        @pl.when(rem > n)
        def _(n=n):
            step(done + n, last if n == 0 else done + n - 1, n % 2)

    final = jnp.where(rem == 0, last, done + rem - 1)
    for slot in range(2):
        @pl.when(rem % 2 == slot)
        def _(slot=slot):
            pv_update(final, slot)

    lam = lam_ref[0]
    heads = []
    for h in range(N_HEADS):
        a0 = acc_ref[2 * h]
        a1 = acc_ref[2 * h + 1]
        o_h = (a0[:HEAD_DIM] * (1.0 / a0[HEAD_DIM:HEAD_DIM + 1])
               - lam * (a1[:HEAD_DIM] * (1.0 / a1[HEAD_DIM:HEAD_DIM + 1])))
        ms = jnp.mean(o_h * o_h, axis=0, keepdims=True)
        heads.append(o_h * lax.rsqrt(ms + EPS))
    out_t = jnp.concatenate(heads, axis=0) * (on_ref[...] * (1.0 - lam_init))
    o_ref[...] = out_t.T.astype(BF16)


def _mix_c(lam, q_t, k, v_t, out_norm_col, nb, s, lam_init):
    nt = s // TQ_C
    tile_t = pl.BlockSpec((1, GROUP, TQ_C), lambda b, t, lam_ref: (b * nt + t, 0, 0))
    full_k = pl.BlockSpec((s, GROUP), lambda b, t, lam_ref: (b, 0), pipeline_mode=pl.Buffered(1))
    full_vt = pl.BlockSpec((nt, N_HEADS * VROWS, TK_C), lambda b, t, lam_ref: (b, 0, 0),
                           pipeline_mode=pl.Buffered(1))
    grid_spec = pltpu.PrefetchScalarGridSpec(
        num_scalar_prefetch=1,
        grid=(nb, nt),
        in_specs=[tile_t, full_k, full_vt,
                  pl.BlockSpec((GROUP, 1), lambda b, t, lam_ref: (0, 0), pipeline_mode=pl.Buffered(1))],
        out_specs=pl.BlockSpec((TQ_C, GROUP), lambda b, t, lam_ref: (b * nt + t, 0)),
        scratch_shapes=[pltpu.VMEM((2, 2 * N_HEADS, TK_C, TQ_C), BF16),
                        pltpu.VMEM((2 * N_HEADS, VROWS, TQ_C), F32),
                        pltpu.VMEM((2 * N_HEADS, 1, TQ_C), F32),
                        pltpu.VMEM((2 * N_HEADS, 1, TQ_C), F32)],
    )
    return pl.pallas_call(
        functools.partial(_mix_c_kernel, lam_init=lam_init),
        grid_spec=grid_spec,
        out_shape=jax.ShapeDtypeStruct((nb * s, GROUP), BF16),
        compiler_params=_cparams(("parallel", "parallel")),
        name="mix_c_diff_attn",
    )(lam, q_t, k, v_t, out_norm_col)


def _mix_d_kernel(qt_ref, k_ref, vt_ref, o_ref, acc_ref, run_ref):
    qt = pl.program_id(1)
    q_t = qt_ref[0]
    acc_ref[...] = jnp.zeros(acc_ref.shape, F32)
    run_ref[...] = jnp.zeros(run_ref.shape, F32)
    r = lax.broadcasted_iota(jnp.int32, (TK_D, TK_D), 0)
    c = lax.broadcasted_iota(jnp.int32, (TK_D, TK_D), 1)
    later = jnp.where(c > r, 1.0, 0.0).astype(BF16)

    def tile(kt, masked):
        start = pl.multiple_of(kt * TK_D, TK_D)
        k = k_ref[pl.ds(start, TK_D), :]
        v_t = vt_ref[kt]
        zs = [jnp.dot(k[:, h * HEAD_DIM:(h + 1) * HEAD_DIM], q_t[h * HEAD_DIM:(h + 1) * HEAD_DIM, :],
                      preferred_element_type=F32) for h in range(N_HEADS)]
        if masked:
            before = (lax.broadcasted_iota(jnp.int32, (TK_D, TQ_D), 0)
                      < lax.broadcasted_iota(jnp.int32, (TK_D, TQ_D), 1))
        worst = None
        log_take, between = [], []
        for h in range(N_HEADS):
            z = zs[h]
            log_keep = jnp.minimum(-z, 0.0) - jnp.log2(1.0 + jnp.exp2(-jnp.abs(z)))
            log_take.append(z + log_keep)
            if masked:
                log_keep = jnp.where(before, log_keep, 0.0)
            run = run_ref[h]
            between.append(run + _split_dot(log_keep, later, 2, x_is_lhs=False))
            run = run + jnp.sum(log_keep, axis=0, keepdims=True)
            run_ref[h] = run
            top = jnp.max(run)
            worst = top if worst is None else jnp.maximum(worst, top)
        for h in range(N_HEADS):
            a = jnp.exp2(log_take[h] + between[h])
            if masked:
                a = jnp.where(before, a, 0.0)
            acc_ref[h] += jnp.dot(v_t[h * HEAD_DIM:(h + 1) * HEAD_DIM, :], a.astype(BF16),
                                  preferred_element_type=F32)
        return worst

    worst = tile(qt, True)

    def cond(state):
        kt, worst = state
        return jnp.logical_and(kt >= 0, worst > SB_SKIP_LOG2)

    def body(state):
        kt, _ = state
        return kt - 1, tile(kt, False)

    lax.while_loop(cond, body, (qt - 1, worst))
    out_t = jnp.concatenate([acc_ref[h] for h in range(N_HEADS)], axis=0)
    o_ref[...] = out_t.T.astype(BF16)


def _mix_d(q_t, k, v_t, nb, s):
    nt = s // TQ_D
    tile_t = pl.BlockSpec((1, GROUP, TQ_D), lambda b, t: (b * nt + t, 0, 0))
    full_k = pl.BlockSpec((s, GROUP), lambda b, t: (b, 0), pipeline_mode=pl.Buffered(1))
    full_vt = pl.BlockSpec((nt, GROUP, TK_D), lambda b, t: (b, 0, 0), pipeline_mode=pl.Buffered(1))
    return pl.pallas_call(
        _mix_d_kernel,
        grid=(nb, nt),
        in_specs=[tile_t, full_k, full_vt],
        out_specs=pl.BlockSpec((TQ_D, GROUP), lambda b, t: (b * nt + t, 0)),
        out_shape=jax.ShapeDtypeStruct((nb * s, GROUP), BF16),
        scratch_shapes=[pltpu.VMEM((N_HEADS, HEAD_DIM, TQ_D), F32),
                        pltpu.VMEM((N_HEADS, 1, TQ_D), F32)],
        compiler_params=_cparams(("parallel", "parallel")),
        name="mix_d_stick_breaking",
    )(q_t, k, v_t)


def _rel_bias_tile(rel_bias):
    nk = TQ_A + LEFT_CHUNKS * CHUNK
    i = np.arange(TQ_A)[:, None]
    j = np.arange(nk)[None, :]
    dc = j // CHUNK - i // CHUNK
    visible = (dc >= 0) & (dc <= LEFT_CHUNKS)
    p = TQ_A + nk
    e = np.arange(p)
    d = np.where(e < nk, e, e - p)
    rel = LEFT_CHUNKS * CHUNK - d
    idx = np.clip(rel, -REL_CLIP, REL_CLIP) + REL_CLIP
    diag = rel_bias[:, idx].astype(F32)
    nh = rel_bias.shape[0]
    skew = jnp.tile(diag, (1, TQ_A))[:, :TQ_A * (p - 1)].reshape(nh, TQ_A, p - 1)
    toeplitz = skew[:, :, :nk]
    return jnp.swapaxes(jnp.where(jnp.asarray(visible)[None], toeplitz * LOG2E, NEG), 1, 2)


def _permute_w_in(w_in):
    sizes = [GROUP, GROUP, GROUP, 2 * GROUP, GROUP, GROUP, N_HEADS, N_HEADS,
             GROUP, GROUP, GROUP, GROUP, GROUP, GROUP]
    offs = np.concatenate([[0], np.cumsum(sizes)])
    seg = [w_in[:, offs[n]:offs[n + 1]] for n in range(len(sizes))]
    main = jnp.concatenate([seg[1], seg[3], seg[4], seg[5], seg[9], seg[12]], axis=1).astype(BF16)
    gates = jnp.concatenate([seg[6], seg[7]], axis=1)
    g_col = jnp.pad(gates, ((0, 0), (0, GATE_LANES - 2 * N_HEADS))).astype(BF16)
    g_row = jnp.pad(gates.T, ((0, GATE_ROWS - 2 * N_HEADS), (0, 0)))
    c_t = jnp.concatenate([seg[n].T for n in (8, 10, 11, 13, 0, 2)] + [g_row], axis=0).astype(BF16)
    return main, c_t, g_col


def _layer(x, nb, s, lam_init, p):
    x = _ffn(x, p["ffn1_norm"], p["ffn1_wg"].astype(BF16), p["ffn1_wu"].astype(BF16), p["ffn1_wd"].astype(BF16))

    w_main, w_ct, w_gc = _permute_w_in(p["w_in"])
    gate_bias = p["b_gate_bias"].reshape(2 * N_HEADS).astype(F32)
    gb_col = jnp.pad(gate_bias, (0, GATE_LANES - 2 * N_HEADS)).reshape(1, GATE_LANES)
    gb_row = jnp.pad(gate_bias, (0, GATE_ROWS - 2 * N_HEADS)).reshape(GATE_ROWS, 1)
    tile4 = lambda g: jnp.tile(g.astype(F32), GROUP // g.shape[0]).reshape(1, GROUP)
    (aq_t, ak, av_t, bqk, bv, bo, gcol, grow, cq_t, ck, cv_t, dq_t, dk, dv_t) = _proj(
        x, p["mix_norm"], w_main, w_ct, w_gc, gb_col, gb_row,
        tile4(p["a_q_norm"]).reshape(GROUP, 1), tile4(p["a_k_norm"]),
        tile4(p["c_q_norm"]).reshape(GROUP, 1), tile4(p["c_k_norm"]))

    ya = _mix_a(aq_t, ak, av_t, _rel_bias_tile(p["a_rel_bias"]), nb, s)
    yb = _mix_b(bqk, bv, bo, gcol, grow, p["b_conv_w"].astype(F32), p["b_conv_b"].astype(F32).reshape(1, -1),
                p["b_out_norm"].astype(F32).reshape(1, GROUP), nb, s)
    lv = p["c_lambda"].astype(F32)
    lam = jnp.exp(jnp.sum(lv[0] * lv[1])) - jnp.exp(jnp.sum(lv[2] * lv[3])) + lam_init
    yc = _mix_c(lam.reshape(1), cq_t, ck, cv_t, tile4(p["c_out_norm"]).reshape(GROUP, 1), nb, s, lam_init)
    yd = _mix_d(dq_t, dk, dv_t, nb, s)

    return _mix_out_ffn(x, ya, yb, yc, yd, p["w_out"].astype(BF16), p["ffn2_norm"], p["ffn2_wg"].astype(BF16),
                        p["ffn2_wu"].astype(BF16), p["ffn2_wd"].astype(BF16))


_PARAM_NAMES = ("ffn1_norm", "ffn1_wg", "ffn1_wu", "ffn1_wd", "mix_norm", "w_in", "a_q_norm", "a_k_norm",
                "a_rel_bias", "b_conv_w", "b_conv_b", "b_gate_bias", "b_out_norm", "c_q_norm", "c_k_norm",
                "c_lambda", "c_out_norm", "w_out", "ffn2_norm", "ffn2_wg", "ffn2_wu", "ffn2_wd")


def kernel(x, ffn1_norm, ffn1_wg, ffn1_wu, ffn1_wd, mix_norm, w_in, a_q_norm, a_k_norm, a_rel_bias,
           b_conv_w, b_conv_b, b_gate_bias, b_out_norm, c_q_norm, c_k_norm, c_lambda, c_out_norm,
           w_out, ffn2_norm, ffn2_wg, ffn2_wu, ffn2_wd):
    params = dict(zip(_PARAM_NAMES, (ffn1_norm, ffn1_wg, ffn1_wu, ffn1_wd, mix_norm, w_in, a_q_norm, a_k_norm,
                                     a_rel_bias, b_conv_w, b_conv_b, b_gate_bias, b_out_norm, c_q_norm,
                                     c_k_norm, c_lambda, c_out_norm, w_out, ffn2_norm, ffn2_wg, ffn2_wu,
                                     ffn2_wd)))
    nb, s, d = x.shape
    depth = ffn1_norm.shape[0]
    h = x.reshape(nb * s, d)
    for l in range(depth):
        lam_init = 0.8 - 0.6 * math.exp(-0.3 * l)
        h = _layer(h, nb, s, lam_init, {k: v[l] for k, v in params.items()})
    return h.reshape(nb, s, d)
```

```python
import functools
import math

import jax
import jax.numpy as jnp
import numpy as np
from jax import lax
from jax.experimental import pallas as pl
from jax.experimental.pallas import tpu as pltpu

F32 = jnp.float32
BF16 = jnp.bfloat16

D_MODEL = 1024
D_FF = 2816
CHUNK = 64
HEAD_DIM = 64
N_HEADS = 4
GROUP = N_HEADS * HEAD_DIM
LEFT_CHUNKS = 8
REL_CLIP = 128
CONV_WIDTH = 4
DIFF_QK_DIM = HEAD_DIM // 2
EPS = 1e-6
NEG = -1e30
LOG2E = 1.4426950408889634

OFF_AK = 0
OFF_BQK, OFF_BO = 256, 768
OFF_CK = 1024
OFF_DK = 1280
MAIN_COLS = 1536
N_T_GROUPS = 7
GATE_LANES = 128
GATE_ROWS = 16
ONES_ROWS = 16
VROWS = HEAD_DIM + ONES_ROWS

VMEM_LIMIT = 56 * 1024 * 1024

TM_FFN = 512
TM_PROJ = 512
TILE_T = 256
TQ_A = 256
LB = 256
TQ_C = 256
TK_C = 256
C_STEPS_PER_TRIP = 8
TQ_D = 256
TK_D = 256
SB_SKIP_LOG2 = -160.0


def _cparams(sem):
    return pltpu.CompilerParams(dimension_semantics=sem, vmem_limit_bytes=VMEM_LIMIT)


def _const_spec(shape):
    nd = len(shape)
    return pl.BlockSpec(shape, lambda *_: (0,) * nd, pipeline_mode=pl.Buffered(1))


def _split_dot(x, mat, terms, x_is_lhs=True):
    acc = None
    rem = x
    for term in range(terms):
        part = rem.astype(BF16)
        if term + 1 < terms:
            rem = rem - part.astype(F32)
        d = (jnp.dot(part, mat, preferred_element_type=F32) if x_is_lhs
             else jnp.dot(mat, part, preferred_element_type=F32))
        acc = d if acc is None else acc + d
    return acc


def _group_ones(width):
    r = lax.broadcasted_iota(jnp.int32, (GROUP, GROUP), 0) // width
    c = lax.broadcasted_iota(jnp.int32, (GROUP, GROUP), 1) // width
    return jnp.where(r == c, 1.0, 0.0).astype(BF16)


def _group_rms(x, gain, width):
    ss = _split_dot(x * x, _group_ones(width), 2)
    return x * lax.rsqrt(ss * (1.0 / width) + EPS) * gain


def _log_sigmoid(x):
    return jnp.minimum(x, 0.0) - jnp.log1p(jnp.exp(-jnp.abs(x)))


def _head_mask(shape, h, axis=1, width=HEAD_DIM):
    lane = lax.broadcasted_iota(jnp.int32, shape, axis)
    return (lane // width) == h


def _ffn_half_step(x, g_ref, wg_ref, wu_ref, wd_ref):
    xn = x * lax.rsqrt(jnp.mean(x * x, axis=-1, keepdims=True) + EPS) * g_ref[...]
    xb = xn.astype(BF16)
    g = jnp.dot(xb, wg_ref[...], preferred_element_type=F32)
    u = jnp.dot(xb, wu_ref[...], preferred_element_type=F32)
    h = (g * jax.nn.sigmoid(g) * u).astype(BF16)
    return x + 0.5 * jnp.dot(h, wd_ref[...], preferred_element_type=F32)


def _ffn_kernel(x_ref, g_ref, wg_ref, wu_ref, wd_ref, o_ref):
    o_ref[...] = _ffn_half_step(x_ref[...], g_ref, wg_ref, wu_ref, wd_ref)


def _mix_out_ffn_kernel(x_ref, ya_ref, yb_ref, yc_ref, yd_ref, wo_ref, g_ref, wg_ref, wu_ref, wd_ref, o_ref):
    x = x_ref[...]
    for grp, y_ref in enumerate((ya_ref, yb_ref, yc_ref, yd_ref)):
        x = x + jnp.dot(y_ref[...], wo_ref[grp * GROUP:(grp + 1) * GROUP, :], preferred_element_type=F32)
    o_ref[...] = _ffn_half_step(x, g_ref, wg_ref, wu_ref, wd_ref)


def _ffn_specs():
    return [_const_spec((1, D_MODEL)), _const_spec((D_MODEL, D_FF)), _const_spec((D_MODEL, D_FF)),
            _const_spec((D_FF, D_MODEL))]


def _ffn(x, gain, wg, wu, wd):
    t = x.shape[0]
    tm = min(TM_FFN, t)
    row = pl.BlockSpec((tm, D_MODEL), lambda i: (i, 0))
    return pl.pallas_call(
        _ffn_kernel,
        grid=(t // tm,),
        in_specs=[row] + _ffn_specs(),
        out_specs=row,
        out_shape=jax.ShapeDtypeStruct((t, D_MODEL), F32),
        compiler_params=_cparams(("parallel",)),
        name="ffn_half_step",
    )(x, gain.reshape(1, D_MODEL), wg, wu, wd)


def _mix_out_ffn(x, ya, yb, yc, yd, w_out, gain, wg, wu, wd):
    t = x.shape[0]
    tm = min(TM_FFN, t)
    row = lambda w: pl.BlockSpec((tm, w), lambda i: (i, 0))
    return pl.pallas_call(
        _mix_out_ffn_kernel,
        grid=(t // tm,),
        in_specs=[row(D_MODEL), row(GROUP), row(GROUP), row(GROUP), row(GROUP),
                  _const_spec((D_MODEL, D_MODEL))] + _ffn_specs(),
        out_specs=row(D_MODEL),
        out_shape=jax.ShapeDtypeStruct((t, D_MODEL), F32),
        compiler_params=_cparams(("parallel",)),
        name="mix_out_proj_ffn",
    )(x, ya, yb, yc, yd, w_out, gain.reshape(1, D_MODEL), wg, wu, wd)


def _proj_kernel(x_ref, g_ref, w_ref, wct_ref, wgc_ref, gbc_ref, gbr_ref,
                 aqn_ref, akn_ref, cqn_ref, ckn_ref,
                 aq_ref, ak_ref, av_ref, bqk_ref, bv_ref, bo_ref, gc_ref, gr_ref,
                 cq_ref, ck_ref, cv_ref, dq_ref, dk_ref, dv_ref):
    x = x_ref[...]
    hn = x * lax.rsqrt(jnp.mean(x * x, axis=-1, keepdims=True) + EPS) * g_ref[...]
    hb = hn.astype(BF16)

    def cols(off, width=GROUP):
        return jnp.dot(hb, w_ref[:, off:off + width], preferred_element_type=F32)

    nt_dims = (((1,), (1,)), ((), ()))

    all_t = lax.dot_general(wct_ref[...], hb, nt_dims, preferred_element_type=F32)

    def rows_t(idx):
        return all_t[idx * GROUP:(idx + 1) * GROUP, :]

    def group_rms_t(x_t, gain_col, width):
        ss = _split_dot(x_t * x_t, _group_ones(width), 2, x_is_lhs=False)
        return x_t * lax.rsqrt(ss * (1.0 / width) + EPS) * gain_col

    aqt = (group_rms_t(rows_t(4), aqn_ref[...], HEAD_DIM) * (HEAD_DIM ** -0.5 * LOG2E)).astype(BF16)
    ak_ref[...] = _group_rms(cols(OFF_AK), akn_ref[...], HEAD_DIM).astype(BF16)
    avt = rows_t(5).astype(BF16)
    bqk_ref[...] = cols(OFF_BQK, 2 * GROUP)
    bvt = rows_t(6).astype(BF16)
    bo_ref[...] = jax.nn.sigmoid(cols(OFF_BO))
    gcol = jnp.dot(hb, wgc_ref[...], preferred_element_type=F32) + gbc_ref[...]
    lane = lax.broadcasted_iota(jnp.int32, gcol.shape, 1)
    gc_ref[...] = jnp.where(lane >= N_HEADS, _log_sigmoid(gcol), gcol)
    grow = all_t[N_T_GROUPS * GROUP:, :] + gbr_ref[...]
    row = lax.broadcasted_iota(jnp.int32, grow.shape, 0)
    gr_ref[...] = jnp.where(row >= N_HEADS, _log_sigmoid(grow), grow)
    ck_ref[...] = _group_rms(cols(OFF_CK), ckn_ref[...], DIFF_QK_DIM).astype(BF16)
    cqt = (group_rms_t(rows_t(0), cqn_ref[...], DIFF_QK_DIM) * (DIFF_QK_DIM ** -0.5 * LOG2E)).astype(BF16)
    cvt = rows_t(1).astype(BF16)
    dk_ref[...] = cols(OFF_DK).astype(BF16)
    dqt = (rows_t(2) * (HEAD_DIM ** -0.5 * LOG2E)).astype(BF16)
    dvt = rows_t(3).astype(BF16)
    for sub in range(cq_ref.shape[0]):
        tok = slice(sub * TILE_T, (sub + 1) * TILE_T)
        aq_ref[sub] = aqt[:, tok]
        av_ref[sub] = avt[:, tok]
        bv_ref[sub] = bvt[:, tok]
        cq_ref[sub] = cqt[:, tok]
        dq_ref[sub] = dqt[:, tok]
        dv_ref[sub] = dvt[:, tok]
        for h in range(N_HEADS):
            cv_ref[sub, h * VROWS:h * VROWS + HEAD_DIM, :] = cvt[h * HEAD_DIM:(h + 1) * HEAD_DIM, tok]
            cv_ref[sub, h * VROWS + HEAD_DIM:(h + 1) * VROWS, :] = jnp.ones((ONES_ROWS, TILE_T), BF16)


def _proj(x, gain, w_main, w_ct, w_gc, gb_col, gb_row, aqn, akn, cqn_col, ckn):
    t = x.shape[0]
    tm = TM_PROJ
    row_spec = lambda w: pl.BlockSpec((tm, w), lambda i: (i, 0))
    bf = lambda w: jax.ShapeDtypeStruct((t, w), BF16)
    f32 = lambda w: jax.ShapeDtypeStruct((t, w), F32)
    n_sub = tm // TILE_T
    tr_shape = jax.ShapeDtypeStruct((t // TILE_T, GROUP, TILE_T), BF16)
    tr_spec = pl.BlockSpec((n_sub, GROUP, TILE_T), lambda i: (i, 0, 0))
    vt_shape = jax.ShapeDtypeStruct((t // TILE_T, N_HEADS * VROWS, TILE_T), BF16)
    vt_spec = pl.BlockSpec((n_sub, N_HEADS * VROWS, TILE_T), lambda i: (i, 0, 0))
    out_shape = [tr_shape, bf(GROUP), tr_shape,
                 f32(2 * GROUP), tr_shape, f32(GROUP),
                 f32(GATE_LANES), jax.ShapeDtypeStruct((GATE_ROWS, t), F32),
                 tr_shape, bf(GROUP), vt_shape,
                 tr_shape, bf(GROUP), tr_shape]
    out_specs = ([tr_spec, row_spec(GROUP), tr_spec, row_spec(2 * GROUP), tr_spec, row_spec(GROUP),
                  row_spec(GATE_LANES), pl.BlockSpec((GATE_ROWS, tm), lambda i: (0, i))]
                 + [tr_spec, row_spec(GROUP), vt_spec, tr_spec, row_spec(GROUP), tr_spec])
    return pl.pallas_call(
        _proj_kernel,
        grid=(t // tm,),
        in_specs=[row_spec(D_MODEL), _const_spec((1, D_MODEL)),
                  _const_spec((D_MODEL, MAIN_COLS)), _const_spec((N_T_GROUPS * GROUP + GATE_ROWS, D_MODEL)),
                  _const_spec((D_MODEL, GATE_LANES)), _const_spec((1, GATE_LANES)),
                  _const_spec((GATE_ROWS, 1)),
                  _const_spec((GROUP, 1)), _const_spec((1, GROUP)),
                  _const_spec((GROUP, 1)), _const_spec((1, GROUP))],
        out_specs=out_specs,
        out_shape=out_shape,
        compiler_params=_cparams(("parallel",)),
        name="mix_in_proj",
    )(x, gain.reshape(1, D_MODEL), w_main, w_ct, w_gc, gb_col, gb_row, aqn, akn, cqn_col, ckn)


def _mix_a_kernel(qt_ref, k0_ref, k1_ref, k2_ref, v0_ref, v1_ref, v2_ref, bias_ref, o_ref):
    t = pl.program_id(1)
    q_t = qt_ref[0]
    k = jnp.concatenate([k0_ref[...], k1_ref[...], k2_ref[...]], axis=0)
    v_t = jnp.concatenate([v0_ref[0], v1_ref[0], v2_ref[0]], axis=1)
    nk = k.shape[0]

    def attend(mask_start):
        scores = [jnp.dot(k[:, h * HEAD_DIM:(h + 1) * HEAD_DIM], q_t[h * HEAD_DIM:(h + 1) * HEAD_DIM, :],
                          preferred_element_type=F32) for h in range(N_HEADS)]
        if mask_start:
            key_pos = t * TQ_A - LEFT_CHUNKS * CHUNK + lax.broadcasted_iota(jnp.int32, (nk, TQ_A), 0)
            valid = key_pos >= 0
        probs = []
        for h in range(N_HEADS):
            s = scores[h] + bias_ref[h]
            if mask_start:
                s = jnp.where(valid, s, NEG)
            e = jnp.exp2(s - jnp.max(s, axis=0, keepdims=True))
            probs.append((e * (1.0 / jnp.sum(e, axis=0, keepdims=True))).astype(BF16))
        heads = [jnp.dot(v_t[h * HEAD_DIM:(h + 1) * HEAD_DIM, :], probs[h], preferred_element_type=F32)
                 for h in range(N_HEADS)]
        o_ref[...] = jnp.concatenate(heads, axis=0).T.astype(BF16)

    n_start_tiles = LEFT_CHUNKS * CHUNK // TQ_A

    @pl.when(t < n_start_tiles)
    def _():
        attend(True)

    @pl.when(t >= n_start_tiles)
    def _():
        attend(False)


def _mix_a(q_t, k, v_t, bias_tile_t, nb, s):
    nt = s // TQ_A
    back_idx = lambda b, t, back: b * nt + jnp.maximum(t - back, 0)
    rows = lambda back: pl.BlockSpec((TQ_A, GROUP), lambda b, t: (back_idx(b, t, back), 0))
    tr = lambda back: pl.BlockSpec((1, GROUP, TQ_A), lambda b, t: (back_idx(b, t, back), 0, 0))
    return pl.pallas_call(
        _mix_a_kernel,
        grid=(nb, nt),
        in_specs=[tr(0), rows(2), rows(1), rows(0), tr(2), tr(1), tr(0), _const_spec(bias_tile_t.shape)],
        out_specs=rows(0),
        out_shape=jax.ShapeDtypeStruct((nb * s, GROUP), BF16),
        compiler_params=_cparams(("parallel", "parallel")),
        name="mix_a_chunk_attn",
    )(q_t, k, k, k, v_t, v_t, v_t, bias_tile_t)


def _mix_b_kernel(qk_ref, vt_ref, og_ref, gc_ref, gr_ref, cw_ref, cb_ref, on_ref, o_ref,
                  xs_ref, ct_ref, n_ref, m_ref):
    c_idx = pl.program_id(1)

    @pl.when(c_idx == 0)
    def _():
        xs_ref[0:8, :] = jnp.zeros((8, 2 * GROUP), F32)
        ct_ref[...] = jnp.zeros_like(ct_ref)
        n_ref[...] = jnp.zeros_like(n_ref)
        m_ref[...] = jnp.zeros_like(m_ref)

    xs_ref[8:8 + LB, :] = qk_ref[...]
    acc = jnp.broadcast_to(cb_ref[...], (LB, 2 * GROUP))
    for j in range(CONV_WIDTH):
        start = 8 - (CONV_WIDTH - 1) + j
        acc = acc + xs_ref[start:start + LB, :] * cw_ref[j:j + 1, :]
    xs_ref[0:8, :] = xs_ref[LB:LB + 8, :]
    qk = acc * jax.nn.sigmoid(acc)
    q_t = qk[:, :GROUP].T.astype(BF16)
    kb = (qk[:, GROUP:] * (HEAD_DIM ** -0.5)).astype(BF16)
    v_t = vt_ref[0]

    r = lax.broadcasted_iota(jnp.int32, (LB, LB), 0)
    c = lax.broadcasted_iota(jnp.int32, (LB, LB), 1)
    tri = jnp.where(c <= r, 1.0, 0.0).astype(BF16)
    tri_t = jnp.where(r <= c, 1.0, 0.0).astype(BF16)
    visible = r <= c
    gcol = gc_ref[...]
    grow = gr_ref[...]
    bcum_col = _split_dot(gcol, tri, 3, x_is_lhs=False)
    bcum_row = _split_dot(grow, tri_t, 3)

    ct_state = ct_ref[...]
    n_state = n_ref[...]
    inter_num = jnp.dot(ct_state.astype(BF16), q_t, preferred_element_type=F32)
    n16 = jnp.concatenate([n_state, jnp.zeros_like(n_state)], axis=0).astype(BF16)
    inter_den = jnp.dot(n16, q_t, preferred_element_type=F32)

    row8 = lax.broadcasted_iota(jnp.int32, (8, GROUP), 0)
    col8 = lax.broadcasted_iota(jnp.int32, (8, GROUP), 1) // HEAD_DIM
    lane1 = lax.broadcasted_iota(jnp.int32, (1, GROUP), 1) // HEAD_DIM
    a_full = jnp.zeros((1, GROUP), F32)
    a8 = jnp.zeros((8, GROUP), F32)
    heads, vw_rows, wg_rows = [], [], []
    for h in range(N_HEADS):
        ch = slice(h * HEAD_DIM, (h + 1) * HEAD_DIM)
        i_row = grow[h:h + 1, :]
        b_row = bcum_row[N_HEADS + h:N_HEADS + h + 1, :]
        u_col = bcum_col[:, N_HEADS + h:N_HEADS + h + 1] - gcol[:, h:h + 1]
        m_prev = m_ref[h:h + 1, 0:1]
        dmat = jnp.where(visible, b_row - u_col, NEG)
        inter = b_row + m_prev
        m_t = jnp.maximum(inter, jnp.max(dmat, axis=0, keepdims=True))
        w_intra = jnp.exp(dmat - m_t)
        s_inter = jnp.exp(inter - m_t)
        sc = jnp.dot(kb[:, ch], q_t[ch, :], preferred_element_type=F32) * w_intra
        num = s_inter * inter_num[ch, :] + jnp.dot(v_t[ch, :], sc.astype(BF16), preferred_element_type=F32)
        den = s_inter * inter_den[h:h + 1, :] + jnp.sum(sc, axis=0, keepdims=True)
        h_t = num * (1.0 / jnp.maximum(jnp.abs(den), jnp.exp(-m_t)))
        heads.append(h_t * lax.rsqrt(jnp.mean(h_t * h_t, axis=0, keepdims=True) + EPS))
        b_tot = b_row[:, LB - 1:LB]
        g = b_tot - b_row + i_row
        m_new = jnp.maximum(b_tot + m_prev, jnp.max(g, axis=1, keepdims=True))
        a_h = jnp.exp(b_tot + m_prev - m_new)
        wg = jnp.exp(g - m_new)
        a_full = jnp.where(lane1 == h, a_h, a_full)
        a8 = jnp.where(row8 == h, a_h, a8)
        vw_rows.append((v_t[ch, :].astype(F32) * wg).astype(BF16))
        wg_rows.append(wg)
        m_ref[h:h + 1, :] = jnp.broadcast_to(m_new, (1, m_ref.shape[1]))

    out_t = jnp.concatenate(heads, axis=0) * on_ref[...]
    o_ref[...] = (out_t.T * og_ref[...]).astype(BF16)

    vw = jnp.concatenate(vw_rows, axis=0)
    kv_t = jnp.dot(vw, kb, preferred_element_type=F32)
    r2 = lax.broadcasted_iota(jnp.int32, (GROUP, GROUP), 0) // HEAD_DIM
    c2 = lax.broadcasted_iota(jnp.int32, (GROUP, GROUP), 1) // HEAD_DIM
    ct_ref[...] = a_full * ct_state + jnp.where(r2 == c2, kv_t, 0.0)
    wg16 = jnp.concatenate(wg_rows + [jnp.zeros((GATE_ROWS - N_HEADS, LB), F32)], axis=0).astype(BF16)
    n_add = jnp.dot(wg16, kb, preferred_element_type=F32)[0:8, :]
    n_ref[...] = a8 * n_state + jnp.where(row8 == col8, n_add, 0.0)


def _mix_b(bqk, bv_t, bo, gcol, grow, conv_w, conv_b, out_norm_col, nb, s):
    nc = s // LB
    row = lambda w: pl.BlockSpec((LB, w), lambda b, c: (b * nc + c, 0))
    return pl.pallas_call(
        _mix_b_kernel,
        grid=(nb, nc),
        in_specs=[row(2 * GROUP), pl.BlockSpec((1, GROUP, LB), lambda b, c: (b * nc + c, 0, 0)),
                  row(GROUP), row(GATE_LANES),
                  pl.BlockSpec((GATE_ROWS, LB), lambda b, c: (0, b * nc + c)),
                  _const_spec((CONV_WIDTH, 2 * GROUP)), _const_spec((1, 2 * GROUP)),
                  _const_spec((GROUP, 1))],
        out_specs=row(GROUP),
        out_shape=jax.ShapeDtypeStruct((nb * s, GROUP), BF16),
        scratch_shapes=[pltpu.VMEM((LB + 8, 2 * GROUP), F32),
                        pltpu.VMEM((GROUP, GROUP), F32),
                        pltpu.VMEM((8, GROUP), F32),
                        pltpu.VMEM((8, 128), F32)],
        compiler_params=_cparams(("parallel", "arbitrary")),
        name="mix_b_mlstm",
    )(bqk, bv_t, bo, gcol, grow, conv_w, conv_b, out_norm_col)


def _mix_c_kernel(lam_ref, qt_ref, k_ref, vt_ref, on_ref, o_ref, p_ref, acc_ref, m_ref, al_ref, *, lam_init):
    qt = pl.program_id(1)
    q_t = qt_ref[0]
    n_maps = 2 * N_HEADS
    acc_ref[...] = jnp.zeros(acc_ref.shape, F32)

    def key_tile(kt):
        return k_ref[pl.ds(pl.multiple_of(kt * TK_C, TK_C), TK_C), :]

    def score_map(k, j):
        ch = slice(j * DIFF_QK_DIM, (j + 1) * DIFF_QK_DIM)
        return jnp.dot(k[:, ch], q_t[ch, :], preferred_element_type=F32)

    def softmax(sc, slot, masked=False):
        if masked:
            key_chunk = lax.broadcasted_iota(jnp.int32, (TK_C, TQ_C), 0) // CHUNK
            q_chunk = lax.broadcasted_iota(jnp.int32, (TK_C, TQ_C), 1) // CHUNK
            vis = key_chunk <= q_chunk
        for j in range(n_maps):
            s = jnp.where(vis, sc[j], NEG) if masked else sc[j]
            m_old = m_ref[j]
            m_new = jnp.maximum(m_old, jnp.max(s, axis=0, keepdims=True))
            al_ref[j] = jnp.exp2(m_old - m_new)
            m_ref[j] = m_new
            p_ref[slot, j] = jnp.exp2(s - m_new).astype(BF16)

    def pv_update(kt, slot):
        v_ext = vt_ref[kt]
        for j in range(n_maps):
            h = j // 2
            pv = jnp.dot(v_ext[h * VROWS:(h + 1) * VROWS, :], p_ref[slot, j], preferred_element_type=F32)
            acc_ref[j] = al_ref[j] * acc_ref[j] + pv

    def step(kt, prev, slot):
        k = key_tile(kt)
        v_ext = vt_ref[prev]
        sc = []
        for j in range(n_maps):
            h = j // 2
            sc.append(score_map(k, j))
            pv = jnp.dot(v_ext[h * VROWS:(h + 1) * VROWS, :], p_ref[slot, j], preferred_element_type=F32)
            acc_ref[j] = al_ref[j] * acc_ref[j] + pv
        softmax(sc, 1 - slot)

    m_ref[...] = jnp.full(m_ref.shape, NEG, F32)
    k_diag = key_tile(qt)
    softmax([score_map(k_diag, j) for j in range(n_maps)], 0, masked=True)
    n_trips = qt // C_STEPS_PER_TRIP

    def body(it, carry):
        a = C_STEPS_PER_TRIP * it
        step(a, jnp.where(it == 0, qt, a - 1), 0)
        for n in range(1, C_STEPS_PER_TRIP):
            step(a + n, a + n - 1, n % 2)
        return carry

    lax.fori_loop(0, n_trips, body, 0)
    done = C_STEPS_PER_TRIP * n_trips
    rem = qt - done
    last = jnp.where(n_trips > 0, done - 1, qt)

    for n in range(C_STEPS_PER_TRIP - 1):
        @pl.when(rem > n)
        def _(n=n):
            step(done + n, last if n == 0 else done + n - 1, n % 2)

    final = jnp.where(rem == 0, last, done + rem - 1)
    for slot in range(2):
        @pl.when(rem % 2 == slot)
        def _(slot=slot):
            pv_update(final, slot)

    lam = lam_ref[0]
    heads = []
    for h in range(N_HEADS):
        a0 = acc_ref[2 * h]
        a1 = acc_ref[2 * h + 1]
        o_h = (a0[:HEAD_DIM] * (1.0 / a0[HEAD_DIM:HEAD_DIM + 1])
               - lam * (a1[:HEAD_DIM] * (1.0 / a1[HEAD_DIM:HEAD_DIM + 1])))
        ms = jnp.mean(o_h * o_h, axis=0, keepdims=True)
        heads.append(o_h * lax.rsqrt(ms + EPS))
    out_t = jnp.concatenate(heads, axis=0) * (on_ref[...] * (1.0 - lam_init))
    o_ref[...] = out_t.T.astype(BF16)


def _mix_c(lam, q_t, k, v_t, out_norm_col, nb, s, lam_init):
    nt = s // TQ_C
    tile_t = pl.BlockSpec((1, GROUP, TQ_C), lambda b, t, lam_ref: (b * nt + t, 0, 0))
    full_k = pl.BlockSpec((s, GROUP), lambda b, t, lam_ref: (b, 0), pipeline_mode=pl.Buffered(1))
    full_vt = pl.BlockSpec((nt, N_HEADS * VROWS, TK_C), lambda b, t, lam_ref: (b, 0, 0),
                           pipeline_mode=pl.Buffered(1))
    grid_spec = pltpu.PrefetchScalarGridSpec(
        num_scalar_prefetch=1,
        grid=(nb, nt),
        in_specs=[tile_t, full_k, full_vt,
                  pl.BlockSpec((GROUP, 1), lambda b, t, lam_ref: (0, 0), pipeline_mode=pl.Buffered(1))],
        out_specs=pl.BlockSpec((TQ_C, GROUP), lambda b, t, lam_ref: (b * nt + t, 0)),
        scratch_shapes=[pltpu.VMEM((2, 2 * N_HEADS, TK_C, TQ_C), BF16),
                        pltpu.VMEM((2 * N_HEADS, VROWS, TQ_C), F32),
                        pltpu.VMEM((2 * N_HEADS, 1, TQ_C), F32),
                        pltpu.VMEM((2 * N_HEADS, 1, TQ_C), F32)],
    )
    return pl.pallas_call(
        functools.partial(_mix_c_kernel, lam_init=lam_init),
        grid_spec=grid_spec,
        out_shape=jax.ShapeDtypeStruct((nb * s, GROUP), BF16),
        compiler_params=_cparams(("parallel", "parallel")),
        name="mix_c_diff_attn",
    )(lam, q_t, k, v_t, out_norm_col)


def _mix_d_kernel(qt_ref, k_ref, vt_ref, o_ref, acc_ref, run_ref):
    qt = pl.program_id(1)
    q_t = qt_ref[0]
    acc_ref[...] = jnp.zeros(acc_ref.shape, F32)
    run_ref[...] = jnp.zeros(run_ref.shape, F32)
    r = lax.broadcasted_iota(jnp.int32, (TK_D, TK_D), 0)
    c = lax.broadcasted_iota(jnp.int32, (TK_D, TK_D), 1)
    later = jnp.where(c > r, 1.0, 0.0).astype(BF16)

    def tile(kt, masked):
        start = pl.multiple_of(kt * TK_D, TK_D)
        k = k_ref[pl.ds(start, TK_D), :]
        v_t = vt_ref[kt]
        zs = [jnp.dot(k[:, h * HEAD_DIM:(h + 1) * HEAD_DIM], q_t[h * HEAD_DIM:(h + 1) * HEAD_DIM, :],
                      preferred_element_type=F32) for h in range(N_HEADS)]
        if masked:
            before = (lax.broadcasted_iota(jnp.int32, (TK_D, TQ_D), 0)
                      < lax.broadcasted_iota(jnp.int32, (TK_D, TQ_D), 1))
        worst = None
        log_take, between = [], []
        for h in range(N_HEADS):
            z = zs[h]
            log_keep = jnp.minimum(-z, 0.0) - jnp.log2(1.0 + jnp.exp2(-jnp.abs(z)))
            log_take.append(z + log_keep)
            if masked:
                log_keep = jnp.where(before, log_keep, 0.0)
            run = run_ref[h]
            between.append(run + _split_dot(log_keep, later, 2, x_is_lhs=False))
            run = run + jnp.sum(log_keep, axis=0, keepdims=True)
            run_ref[h] = run
            top = jnp.max(run)
            worst = top if worst is None else jnp.maximum(worst, top)
        for h in range(N_HEADS):
            a = jnp.exp2(log_take[h] + between[h])
            if masked:
                a = jnp.where(before, a, 0.0)
            acc_ref[h] += jnp.dot(v_t[h * HEAD_DIM:(h + 1) * HEAD_DIM, :], a.astype(BF16),
                                  preferred_element_type=F32)
        return worst

    worst = tile(qt, True)

    def cond(state):
        kt, worst = state
        return jnp.logical_and(kt >= 0, worst > SB_SKIP_LOG2)

    def body(state):
        kt, _ = state
        return kt - 1, tile(kt, False)

    lax.while_loop(cond, body, (qt - 1, worst))
    out_t = jnp.concatenate([acc_ref[h] for h in range(N_HEADS)], axis=0)
    o_ref[...] = out_t.T.astype(BF16)


def _mix_d(q_t, k, v_t, nb, s):
    nt = s // TQ_D
    tile_t = pl.BlockSpec((1, GROUP, TQ_D), lambda b, t: (b * nt + t, 0, 0))
    full_k = pl.BlockSpec((s, GROUP), lambda b, t: (b, 0), pipeline_mode=pl.Buffered(1))
    full_vt = pl.BlockSpec((nt, GROUP, TK_D), lambda b, t: (b, 0, 0), pipeline_mode=pl.Buffered(1))
    return pl.pallas_call(
        _mix_d_kernel,
        grid=(nb, nt),
        in_specs=[tile_t, full_k, full_vt],
        out_specs=pl.BlockSpec((TQ_D, GROUP), lambda b, t: (b * nt + t, 0)),
        out_shape=jax.ShapeDtypeStruct((nb * s, GROUP), BF16),
        scratch_shapes=[pltpu.VMEM((N_HEADS, HEAD_DIM, TQ_D), F32),
                        pltpu.VMEM((N_HEADS, 1, TQ_D), F32)],
        compiler_params=_cparams(("parallel", "parallel")),
        name="mix_d_stick_breaking",
    )(q_t, k, v_t)


def _rel_bias_tile(rel_bias):
    nk = TQ_A + LEFT_CHUNKS * CHUNK
    i = np.arange(TQ_A)[:, None]
    j = np.arange(nk)[None, :]
    dc = j // CHUNK - i // CHUNK
    visible = (dc >= 0) & (dc <= LEFT_CHUNKS)
    p = TQ_A + nk
    e = np.arange(p)
    d = np.where(e < nk, e, e - p)
    rel = LEFT_CHUNKS * CHUNK - d
    idx = np.clip(rel, -REL_CLIP, REL_CLIP) + REL_CLIP
    diag = rel_bias[:, idx].astype(F32)
    nh = rel_bias.shape[0]
    skew = jnp.tile(diag, (1, TQ_A))[:, :TQ_A * (p - 1)].reshape(nh, TQ_A, p - 1)
    toeplitz = skew[:, :, :nk]
    return jnp.swapaxes(jnp.where(jnp.asarray(visible)[None], toeplitz * LOG2E, NEG), 1, 2)


def _permute_w_in(w_in):
    sizes = [GROUP, GROUP, GROUP, 2 * GROUP, GROUP, GROUP, N_HEADS, N_HEADS,
             GROUP, GROUP, GROUP, GROUP, GROUP, GROUP]
    offs = np.concatenate([[0], np.cumsum(sizes)])
    seg = [w_in[:, offs[n]:offs[n + 1]] for n in range(len(sizes))]
    main = jnp.concatenate([seg[1], seg[3], seg[5], seg[9], seg[12]], axis=1).astype(BF16)
    gates = jnp.concatenate([seg[6], seg[7]], axis=1)
    g_col = jnp.pad(gates, ((0, 0), (0, GATE_LANES - 2 * N_HEADS))).astype(BF16)
    g_row = jnp.pad(gates.T, ((0, GATE_ROWS - 2 * N_HEADS), (0, 0)))
    c_t = jnp.concatenate([seg[n].T for n in (8, 10, 11, 13, 0, 2, 4)] + [g_row], axis=0).astype(BF16)
    return main, c_t, g_col


def _layer(x, nb, s, lam_init, p):
    x = _ffn(x, p["ffn1_norm"], p["ffn1_wg"].astype(BF16), p["ffn1_wu"].astype(BF16), p["ffn1_wd"].astype(BF16))

    w_main, w_ct, w_gc = _permute_w_in(p["w_in"])
    gate_bias = p["b_gate_bias"].reshape(2 * N_HEADS).astype(F32)
    gb_col = jnp.pad(gate_bias, (0, GATE_LANES - 2 * N_HEADS)).reshape(1, GATE_LANES)
    gb_row = jnp.pad(gate_bias, (0, GATE_ROWS - 2 * N_HEADS)).reshape(GATE_ROWS, 1)
    tile4 = lambda g: jnp.tile(g.astype(F32), GROUP // g.shape[0]).reshape(1, GROUP)
    (aq_t, ak, av_t, bqk, bv_t, bo, gcol, grow, cq_t, ck, cv_t, dq_t, dk, dv_t) = _proj(
        x, p["mix_norm"], w_main, w_ct, w_gc, gb_col, gb_row,
        tile4(p["a_q_norm"]).reshape(GROUP, 1), tile4(p["a_k_norm"]),
        tile4(p["c_q_norm"]).reshape(GROUP, 1), tile4(p["c_k_norm"]))

    ya = _mix_a(aq_t, ak, av_t, _rel_bias_tile(p["a_rel_bias"]), nb, s)
    yb = _mix_b(bqk, bv_t, bo, gcol, grow, p["b_conv_w"].astype(F32), p["b_conv_b"].astype(F32).reshape(1, -1),
                p["b_out_norm"].astype(F32).reshape(GROUP, 1), nb, s)
    lv = p["c_lambda"].astype(F32)
    lam = jnp.exp(jnp.sum(lv[0] * lv[1])) - jnp.exp(jnp.sum(lv[2] * lv[3])) + lam_init
    yc = _mix_c(lam.reshape(1), cq_t, ck, cv_t, tile4(p["c_out_norm"]).reshape(GROUP, 1), nb, s, lam_init)
    yd = _mix_d(dq_t, dk, dv_t, nb, s)

    return _mix_out_ffn(x, ya, yb, yc, yd, p["w_out"].astype(BF16), p["ffn2_norm"], p["ffn2_wg"].astype(BF16),
                        p["ffn2_wu"].astype(BF16), p["ffn2_wd"].astype(BF16))


_PARAM_NAMES = ("ffn1_norm", "ffn1_wg", "ffn1_wu", "ffn1_wd", "mix_norm", "w_in", "a_q_norm", "a_k_norm",
                "a_rel_bias", "b_conv_w", "b_conv_b", "b_gate_bias", "b_out_norm", "c_q_norm", "c_k_norm",
                "c_lambda", "c_out_norm", "w_out", "ffn2_norm", "ffn2_wg", "ffn2_wu", "ffn2_wd")


def kernel(x, ffn1_norm, ffn1_wg, ffn1_wu, ffn1_wd, mix_norm, w_in, a_q_norm, a_k_norm, a_rel_bias,
           b_conv_w, b_conv_b, b_gate_bias, b_out_norm, c_q_norm, c_k_norm, c_lambda, c_out_norm,
           w_out, ffn2_norm, ffn2_wg, ffn2_wu, ffn2_wd):
    params = dict(zip(_PARAM_NAMES, (ffn1_norm, ffn1_wg, ffn1_wu, ffn1_wd, mix_norm, w_in, a_q_norm, a_k_norm,
                                     a_rel_bias, b_conv_w, b_conv_b, b_gate_bias, b_out_norm, c_q_norm,
                                     c_k_norm, c_lambda, c_out_norm, w_out, ffn2_norm, ffn2_wg, ffn2_wu,
                                     ffn2_wd)))
    nb, s, d = x.shape
    depth = ffn1_norm.shape[0]
    h = x.reshape(nb * s, d)
    for l in range(depth):
        lam_init = 0.8 - 0.6 * math.exp(-0.3 * l)
        h = _layer(h, nb, s, lam_init, {k: v[l] for k, v in params.items()})
    return h.reshape(nb, s, d)
```

```python
import functools
import math

import jax
import jax.numpy as jnp
import numpy as np
from jax import lax
from jax.experimental import pallas as pl
from jax.experimental.pallas import tpu as pltpu

F32 = jnp.float32
BF16 = jnp.bfloat16

D_MODEL = 1024
D_FF = 2816
CHUNK = 64
HEAD_DIM = 64
N_HEADS = 4
GROUP = N_HEADS * HEAD_DIM
LEFT_CHUNKS = 8
REL_CLIP = 128
CONV_WIDTH = 4
DIFF_QK_DIM = HEAD_DIM // 2
EPS = 1e-6
NEG = -1e30
LOG2E = 1.4426950408889634

OFF_AK = 0
OFF_BQK, OFF_BV, OFF_BO = 256, 768, 1024
OFF_CK = 1280
OFF_DK = 1536
MAIN_COLS = 1792
N_T_GROUPS = 6
GATE_LANES = 128
GATE_ROWS = 16
ONES_ROWS = 16
VROWS = HEAD_DIM + ONES_ROWS

VMEM_LIMIT = 56 * 1024 * 1024

TM_FFN = 512
TM_PROJ = 512
TILE_T = 256
TQ_A = 256
LB = 256
TQ_C = 256
TK_C = 256
C_STEPS_PER_TRIP = 8
TQ_D = 256
TK_D = 256
SB_SKIP_LOG2 = -160.0


def _cparams(sem):
    return pltpu.CompilerParams(dimension_semantics=sem, vmem_limit_bytes=VMEM_LIMIT)


def _const_spec(shape):
    nd = len(shape)
    return pl.BlockSpec(shape, lambda *_: (0,) * nd, pipeline_mode=pl.Buffered(1))


def _split_dot(x, mat, terms, x_is_lhs=True):
    acc = None
    rem = x
    for term in range(terms):
        part = rem.astype(BF16)
        if term + 1 < terms:
            rem = rem - part.astype(F32)
        d = (jnp.dot(part, mat, preferred_element_type=F32) if x_is_lhs
             else jnp.dot(mat, part, preferred_element_type=F32))
        acc = d if acc is None else acc + d
    return acc


def _group_ones(width):
    r = lax.broadcasted_iota(jnp.int32, (GROUP, GROUP), 0) // width
    c = lax.broadcasted_iota(jnp.int32, (GROUP, GROUP), 1) // width
    return jnp.where(r == c, 1.0, 0.0).astype(BF16)


def _group_rms(x, gain, width):
    ss = _split_dot(x * x, _group_ones(width), 2)
    return x * lax.rsqrt(ss * (1.0 / width) + EPS) * gain


def _log_sigmoid(x):
    return jnp.minimum(x, 0.0) - jnp.log1p(jnp.exp(-jnp.abs(x)))


def _head_mask(shape, h, axis=1, width=HEAD_DIM):
    lane = lax.broadcasted_iota(jnp.int32, shape, axis)
    return (lane // width) == h


def _ffn_half_step(x, g_ref, wg_ref, wu_ref, wd_ref):
    xn = x * lax.rsqrt(jnp.mean(x * x, axis=-1, keepdims=True) + EPS) * g_ref[...]
    xb = xn.astype(BF16)
    g = jnp.dot(xb, wg_ref[...], preferred_element_type=F32)
    u = jnp.dot(xb, wu_ref[...], preferred_element_type=F32)
    h = (g * jax.nn.sigmoid(g) * u).astype(BF16)
    return x + 0.5 * jnp.dot(h, wd_ref[...], preferred_element_type=F32)


def _ffn_kernel(x_ref, g_ref, wg_ref, wu_ref, wd_ref, o_ref):
    o_ref[...] = _ffn_half_step(x_ref[...], g_ref, wg_ref, wu_ref, wd_ref)


def _mix_out_ffn_kernel(x_ref, ya_ref, yb_ref, yc_ref, yd_ref, wo_ref, g_ref, wg_ref, wu_ref, wd_ref, o_ref):
    x = x_ref[...]
    for grp, y_ref in enumerate((ya_ref, yb_ref, yc_ref, yd_ref)):
        x = x + jnp.dot(y_ref[...], wo_ref[grp * GROUP:(grp + 1) * GROUP, :], preferred_element_type=F32)
    o_ref[...] = _ffn_half_step(x, g_ref, wg_ref, wu_ref, wd_ref)


def _ffn_specs():
    return [_const_spec((1, D_MODEL)), _const_spec((D_MODEL, D_FF)), _const_spec((D_MODEL, D_FF)),
            _const_spec((D_FF, D_MODEL))]


def _ffn(x, gain, wg, wu, wd):
    t = x.shape[0]
    tm = min(TM_FFN, t)
    row = pl.BlockSpec((tm, D_MODEL), lambda i: (i, 0))
    return pl.pallas_call(
        _ffn_kernel,
        grid=(t // tm,),
        in_specs=[row] + _ffn_specs(),
        out_specs=row,
        out_shape=jax.ShapeDtypeStruct((t, D_MODEL), F32),
        compiler_params=_cparams(("parallel",)),
        name="ffn_half_step",
    )(x, gain.reshape(1, D_MODEL), wg, wu, wd)


def _mix_out_ffn(x, ya, yb, yc, yd, w_out, gain, wg, wu, wd):
    t = x.shape[0]
    tm = min(TM_FFN, t)
    row = lambda w: pl.BlockSpec((tm, w), lambda i: (i, 0))
    return pl.pallas_call(
        _mix_out_ffn_kernel,
        grid=(t // tm,),
        in_specs=[row(D_MODEL), row(GROUP), row(GROUP), row(GROUP), row(GROUP),
                  _const_spec((D_MODEL, D_MODEL))] + _ffn_specs(),
        out_specs=row(D_MODEL),
        out_shape=jax.ShapeDtypeStruct((t, D_MODEL), F32),
        compiler_params=_cparams(("parallel",)),
        name="mix_out_proj_ffn",
    )(x, ya, yb, yc, yd, w_out, gain.reshape(1, D_MODEL), wg, wu, wd)


def _proj_kernel(x_ref, g_ref, w_ref, wct_ref, wgc_ref, gbc_ref, gbr_ref,
                 aqn_ref, akn_ref, cqn_ref, ckn_ref,
                 aq_ref, ak_ref, av_ref, bqk_ref, bv_ref, bo_ref, gc_ref, gr_ref,
                 cq_ref, ck_ref, cv_ref, dq_ref, dk_ref, dv_ref):
    x = x_ref[...]
    hn = x * lax.rsqrt(jnp.mean(x * x, axis=-1, keepdims=True) + EPS) * g_ref[...]
    hb = hn.astype(BF16)

    def cols(off, width=GROUP):
        return jnp.dot(hb, w_ref[:, off:off + width], preferred_element_type=F32)

    nt_dims = (((1,), (1,)), ((), ()))

    all_t = lax.dot_general(wct_ref[...], hb, nt_dims, preferred_element_type=F32)

    def rows_t(idx):
        return all_t[idx * GROUP:(idx + 1) * GROUP, :]

    def group_rms_t(x_t, gain_col, width):
        ss = _split_dot(x_t * x_t, _group_ones(width), 2, x_is_lhs=False)
        return x_t * lax.rsqrt(ss * (1.0 / width) + EPS) * gain_col

    aqt = (group_rms_t(rows_t(4), aqn_ref[...], HEAD_DIM) * (HEAD_DIM ** -0.5 * LOG2E)).astype(BF16)
    ak_ref[...] = _group_rms(cols(OFF_AK), akn_ref[...], HEAD_DIM).astype(BF16)
    avt = rows_t(5).astype(BF16)
    bqk_ref[...] = cols(OFF_BQK, 2 * GROUP)
    bv_ref[...] = cols(OFF_BV).astype(BF16)
    bo_ref[...] = jax.nn.sigmoid(cols(OFF_BO))
    gcol = jnp.dot(hb, wgc_ref[...], preferred_element_type=F32) + gbc_ref[...]
    lane = lax.broadcasted_iota(jnp.int32, gcol.shape, 1)
    gc_ref[...] = jnp.where(lane >= N_HEADS, _log_sigmoid(gcol), gcol)
    grow = all_t[N_T_GROUPS * GROUP:, :] + gbr_ref[...]
    row = lax.broadcasted_iota(jnp.int32, grow.shape, 0)
    gr_ref[...] = jnp.where(row >= N_HEADS, _log_sigmoid(grow), grow)
    ck_ref[...] = _group_rms(cols(OFF_CK), ckn_ref[...], DIFF_QK_DIM).astype(BF16)
    cqt = (group_rms_t(rows_t(0), cqn_ref[...], DIFF_QK_DIM) * (DIFF_QK_DIM ** -0.5 * LOG2E)).astype(BF16)
    cvt = rows_t(1).astype(BF16)
    dk_ref[...] = cols(OFF_DK).astype(BF16)
    dqt = (rows_t(2) * (HEAD_DIM ** -0.5 * LOG2E)).astype(BF16)
    dvt = rows_t(3).astype(BF16)
    for sub in range(cq_ref.shape[0]):
        tok = slice(sub * TILE_T, (sub + 1) * TILE_T)
        aq_ref[sub] = aqt[:, tok]
        av_ref[sub] = avt[:, tok]
        cq_ref[sub] = cqt[:, tok]
        dq_ref[sub] = dqt[:, tok]
        dv_ref[sub] = dvt[:, tok]
        for h in range(N_HEADS):
            cv_ref[sub, h * VROWS:h * VROWS + HEAD_DIM, :] = cvt[h * HEAD_DIM:(h + 1) * HEAD_DIM, tok]
            cv_ref[sub, h * VROWS + HEAD_DIM:(h + 1) * VROWS, :] = jnp.ones((ONES_ROWS, TILE_T), BF16)


def _proj(x, gain, w_main, w_ct, w_gc, gb_col, gb_row, aqn, akn, cqn_col, ckn):
    t = x.shape[0]
    tm = TM_PROJ
    row_spec = lambda w: pl.BlockSpec((tm, w), lambda i: (i, 0))
    bf = lambda w: jax.ShapeDtypeStruct((t, w), BF16)
    f32 = lambda w: jax.ShapeDtypeStruct((t, w), F32)
    n_sub = tm // TILE_T
    tr_shape = jax.ShapeDtypeStruct((t // TILE_T, GROUP, TILE_T), BF16)
    tr_spec = pl.BlockSpec((n_sub, GROUP, TILE_T), lambda i: (i, 0, 0))
    vt_shape = jax.ShapeDtypeStruct((t // TILE_T, N_HEADS * VROWS, TILE_T), BF16)
    vt_spec = pl.BlockSpec((n_sub, N_HEADS * VROWS, TILE_T), lambda i: (i, 0, 0))
    out_shape = [tr_shape, bf(GROUP), tr_shape,
                 f32(2 * GROUP), bf(GROUP), f32(GROUP),
                 f32(GATE_LANES), jax.ShapeDtypeStruct((GATE_ROWS, t), F32),
                 tr_shape, bf(GROUP), vt_shape,
                 tr_shape, bf(GROUP), tr_shape]
    out_specs = ([tr_spec, row_spec(GROUP), tr_spec, row_spec(2 * GROUP), row_spec(GROUP), row_spec(GROUP),
                  row_spec(GATE_LANES), pl.BlockSpec((GATE_ROWS, tm), lambda i: (0, i))]
                 + [tr_spec, row_spec(GROUP), vt_spec, tr_spec, row_spec(GROUP), tr_spec])
    return pl.pallas_call(
        _proj_kernel,
        grid=(t // tm,),
        in_specs=[row_spec(D_MODEL), _const_spec((1, D_MODEL)),
                  _const_spec((D_MODEL, MAIN_COLS)), _const_spec((N_T_GROUPS * GROUP + GATE_ROWS, D_MODEL)),
                  _const_spec((D_MODEL, GATE_LANES)), _const_spec((1, GATE_LANES)),
                  _const_spec((GATE_ROWS, 1)),
                  _const_spec((GROUP, 1)), _const_spec((1, GROUP)),
                  _const_spec((GROUP, 1)), _const_spec((1, GROUP))],
        out_specs=out_specs,
        out_shape=out_shape,
        compiler_params=_cparams(("parallel",)),
        name="mix_in_proj",
    )(x, gain.reshape(1, D_MODEL), w_main, w_ct, w_gc, gb_col, gb_row, aqn, akn, cqn_col, ckn)


def _mix_a_kernel(qt_ref, k0_ref, k1_ref, k2_ref, v0_ref, v1_ref, v2_ref, bias_ref, o_ref):
    t = pl.program_id(1)
    q_t = qt_ref[0]
    k = jnp.concatenate([k0_ref[...], k1_ref[...], k2_ref[...]], axis=0)
    v_t = jnp.concatenate([v0_ref[0], v1_ref[0], v2_ref[0]], axis=1)
    nk = k.shape[0]

    def attend(mask_start):
        scores = [jnp.dot(k[:, h * HEAD_DIM:(h + 1) * HEAD_DIM], q_t[h * HEAD_DIM:(h + 1) * HEAD_DIM, :],
                          preferred_element_type=F32) for h in range(N_HEADS)]
        if mask_start:
            key_pos = t * TQ_A - LEFT_CHUNKS * CHUNK + lax.broadcasted_iota(jnp.int32, (nk, TQ_A), 0)
            valid = key_pos >= 0
        probs = []
        for h in range(N_HEADS):
            s = scores[h] + bias_ref[h]
            if mask_start:
                s = jnp.where(valid, s, NEG)
            e = jnp.exp2(s - jnp.max(s, axis=0, keepdims=True))
            probs.append((e * (1.0 / jnp.sum(e, axis=0, keepdims=True))).astype(BF16))
        heads = [jnp.dot(v_t[h * HEAD_DIM:(h + 1) * HEAD_DIM, :], probs[h], preferred_element_type=F32)
                 for h in range(N_HEADS)]
        o_ref[...] = jnp.concatenate(heads, axis=0).T.astype(BF16)

    n_start_tiles = LEFT_CHUNKS * CHUNK // TQ_A

    @pl.when(t < n_start_tiles)
    def _():
        attend(True)

    @pl.when(t >= n_start_tiles)
    def _():
        attend(False)


def _mix_a(q_t, k, v_t, bias_tile_t, nb, s):
    nt = s // TQ_A
    back_idx = lambda b, t, back: b * nt + jnp.maximum(t - back, 0)
    rows = lambda back: pl.BlockSpec((TQ_A, GROUP), lambda b, t: (back_idx(b, t, back), 0))
    tr = lambda back: pl.BlockSpec((1, GROUP, TQ_A), lambda b, t: (back_idx(b, t, back), 0, 0))
    return pl.pallas_call(
        _mix_a_kernel,
        grid=(nb, nt),
        in_specs=[tr(0), rows(2), rows(1), rows(0), tr(2), tr(1), tr(0), _const_spec(bias_tile_t.shape)],
        out_specs=rows(0),
        out_shape=jax.ShapeDtypeStruct((nb * s, GROUP), BF16),
        compiler_params=_cparams(("parallel", "parallel")),
        name="mix_a_chunk_attn",
    )(q_t, k, k, k, v_t, v_t, v_t, bias_tile_t)


def _mix_b_kernel(qk_ref, v_ref, og_ref, gc_ref, gr_ref, cw_ref, cb_ref, on_ref, o_ref,
                  xs_ref, ct_ref, n_ref, m_ref):
    c_idx = pl.program_id(1)

    @pl.when(c_idx == 0)
    def _():
        xs_ref[0:8, :] = jnp.zeros((8, 2 * GROUP), F32)
        ct_ref[...] = jnp.zeros_like(ct_ref)
        n_ref[...] = jnp.zeros_like(n_ref)
        m_ref[...] = jnp.zeros_like(m_ref)

    xs_ref[8:8 + LB, :] = qk_ref[...]
    acc = jnp.broadcast_to(cb_ref[...], (LB, 2 * GROUP))
    for j in range(CONV_WIDTH):
        start = 8 - (CONV_WIDTH - 1) + j
        acc = acc + xs_ref[start:start + LB, :] * cw_ref[j:j + 1, :]
    xs_ref[0:8, :] = xs_ref[LB:LB + 8, :]
    qk = acc * jax.nn.sigmoid(acc)
    q_t = qk[:, :GROUP].T.astype(BF16)
    kb = (qk[:, GROUP:] * (HEAD_DIM ** -0.5)).astype(BF16)
    v_t = v_ref[...].astype(F32).T.astype(BF16)

    r = lax.broadcasted_iota(jnp.int32, (LB, LB), 0)
    c = lax.broadcasted_iota(jnp.int32, (LB, LB), 1)
    tri = jnp.where(c <= r, 1.0, 0.0).astype(BF16)
    tri_t = jnp.where(r <= c, 1.0, 0.0).astype(BF16)
    visible = r <= c
    gcol = gc_ref[...]
    grow = gr_ref[...]
    bcum_col = _split_dot(gcol, tri, 3, x_is_lhs=False)
    bcum_row = _split_dot(grow, tri_t, 3)

    ct_state = ct_ref[...]
    n_state = n_ref[...]
    inter_num = jnp.dot(ct_state.astype(BF16), q_t, preferred_element_type=F32)
    n16 = jnp.concatenate([n_state, jnp.zeros_like(n_state)], axis=0).astype(BF16)
    inter_den = jnp.dot(n16, q_t, preferred_element_type=F32)

    row8 = lax.broadcasted_iota(jnp.int32, (8, GROUP), 0)
    col8 = lax.broadcasted_iota(jnp.int32, (8, GROUP), 1) // HEAD_DIM
    lane1 = lax.broadcasted_iota(jnp.int32, (1, GROUP), 1) // HEAD_DIM
    a_full = jnp.zeros((1, GROUP), F32)
    a8 = jnp.zeros((8, GROUP), F32)
    heads, vw_rows, wg_rows = [], [], []
    for h in range(N_HEADS):
        ch = slice(h * HEAD_DIM, (h + 1) * HEAD_DIM)
        i_row = grow[h:h + 1, :]
        b_row = bcum_row[N_HEADS + h:N_HEADS + h + 1, :]
        u_col = bcum_col[:, N_HEADS + h:N_HEADS + h + 1] - gcol[:, h:h + 1]
        m_prev = m_ref[h:h + 1, 0:1]
        dmat = jnp.where(visible, b_row - u_col, NEG)
        inter = b_row + m_prev
        m_t = jnp.maximum(inter, jnp.max(dmat, axis=0, keepdims=True))
        w_intra = jnp.exp(dmat - m_t)
        s_inter = jnp.exp(inter - m_t)
        sc = jnp.dot(kb[:, ch], q_t[ch, :], preferred_element_type=F32) * w_intra
        num = s_inter * inter_num[ch, :] + jnp.dot(v_t[ch, :], sc.astype(BF16), preferred_element_type=F32)
        den = s_inter * inter_den[h:h + 1, :] + jnp.sum(sc, axis=0, keepdims=True)
        h_t = num * (1.0 / jnp.maximum(jnp.abs(den), jnp.exp(-m_t)))
        heads.append(h_t * lax.rsqrt(jnp.mean(h_t * h_t, axis=0, keepdims=True) + EPS))
        b_tot = b_row[:, LB - 1:LB]
        g = b_tot - b_row + i_row
        m_new = jnp.maximum(b_tot + m_prev, jnp.max(g, axis=1, keepdims=True))
        a_h = jnp.exp(b_tot + m_prev - m_new)
        wg = jnp.exp(g - m_new)
        a_full = jnp.where(lane1 == h, a_h, a_full)
        a8 = jnp.where(row8 == h, a_h, a8)
        vw_rows.append((v_t[ch, :].astype(F32) * wg).astype(BF16))
        wg_rows.append(wg)
        m_ref[h:h + 1, :] = jnp.broadcast_to(m_new, (1, m_ref.shape[1]))

    out_t = jnp.concatenate(heads, axis=0) * on_ref[...]
    o_ref[...] = (out_t.T * og_ref[...]).astype(BF16)

    vw = jnp.concatenate(vw_rows, axis=0)
    kv_t = jnp.dot(vw, kb, preferred_element_type=F32)
    r2 = lax.broadcasted_iota(jnp.int32, (GROUP, GROUP), 0) // HEAD_DIM
    c2 = lax.broadcasted_iota(jnp.int32, (GROUP, GROUP), 1) // HEAD_DIM
    ct_ref[...] = a_full * ct_state + jnp.where(r2 == c2, kv_t, 0.0)
    wg16 = jnp.concatenate(wg_rows + [jnp.zeros((GATE_ROWS - N_HEADS, LB), F32)], axis=0).astype(BF16)
    n_add = jnp.dot(wg16, kb, preferred_element_type=F32)[0:8, :]
    n_ref[...] = a8 * n_state + jnp.where(row8 == col8, n_add, 0.0)


def _mix_b(bqk, bv, bo, gcol, grow, conv_w, conv_b, out_norm_col, nb, s):
    nc = s // LB
    row = lambda w: pl.BlockSpec((LB, w), lambda b, c: (b * nc + c, 0))
    return pl.pallas_call(
        _mix_b_kernel,
        grid=(nb, nc),
        in_specs=[row(2 * GROUP), row(GROUP), row(GROUP), row(GATE_LANES),
                  pl.BlockSpec((GATE_ROWS, LB), lambda b, c: (0, b * nc + c)),
                  _const_spec((CONV_WIDTH, 2 * GROUP)), _const_spec((1, 2 * GROUP)),
                  _const_spec((GROUP, 1))],
        out_specs=row(GROUP),
        out_shape=jax.ShapeDtypeStruct((nb * s, GROUP), BF16),
        scratch_shapes=[pltpu.VMEM((LB + 8, 2 * GROUP), F32),
                        pltpu.VMEM((GROUP, GROUP), F32),
                        pltpu.VMEM((8, GROUP), F32),
                        pltpu.VMEM((8, 128), F32)],
        compiler_params=_cparams(("parallel", "arbitrary")),
        name="mix_b_mlstm",
    )(bqk, bv, bo, gcol, grow, conv_w, conv_b, out_norm_col)


def _mix_c_kernel(lam_ref, qt_ref, k_ref, vt_ref, on_ref, o_ref, p_ref, acc_ref, m_ref, al_ref, *, lam_init):
    qt = pl.program_id(1)
    q_t = qt_ref[0]
    n_maps = 2 * N_HEADS
    acc_ref[...] = jnp.zeros(acc_ref.shape, F32)

    def key_tile(kt):
        return k_ref[pl.ds(pl.multiple_of(kt * TK_C, TK_C), TK_C), :]

    def score_map(k, j):
        ch = slice(j * DIFF_QK_DIM, (j + 1) * DIFF_QK_DIM)
        return jnp.dot(k[:, ch], q_t[ch, :], preferred_element_type=F32)

    def softmax(sc, slot, masked=False):
        if masked:
            key_chunk = lax.broadcasted_iota(jnp.int32, (TK_C, TQ_C), 0) // CHUNK
            q_chunk = lax.broadcasted_iota(jnp.int32, (TK_C, TQ_C), 1) // CHUNK
            vis = key_chunk <= q_chunk
        for j in range(n_maps):
            s = jnp.where(vis, sc[j], NEG) if masked else sc[j]
            m_old = m_ref[j]
            m_new = jnp.maximum(m_old, jnp.max(s, axis=0, keepdims=True))
            al_ref[j] = jnp.exp2(m_old - m_new)
            m_ref[j] = m_new
            p_ref[slot, j] = jnp.exp2(s - m_new).astype(BF16)

    def pv_update(kt, slot):
        v_ext = vt_ref[kt]
        for j in range(n_maps):
            h = j // 2
            pv = jnp.dot(v_ext[h * VROWS:(h + 1) * VROWS, :], p_ref[slot, j], preferred_element_type=F32)
            acc_ref[j] = al_ref[j] * acc_ref[j] + pv

    def step(kt, prev, slot):
        k = key_tile(kt)
        v_ext = vt_ref[prev]
        sc = []
        for j in range(n_maps):
            h = j // 2
            sc.append(score_map(k, j))
            pv = jnp.dot(v_ext[h * VROWS:(h + 1) * VROWS, :], p_ref[slot, j], preferred_element_type=F32)
            acc_ref[j] = al_ref[j] * acc_ref[j] + pv
        softmax(sc, 1 - slot)

    m_ref[...] = jnp.full(m_ref.shape, NEG, F32)
    k_diag = key_tile(qt)
    softmax([score_map(k_diag, j) for j in range(n_maps)], 0, masked=True)
    n_trips = qt // C_STEPS_PER_TRIP

    def body(it, carry):
        a = C_STEPS_PER_TRIP * it
        step(a, jnp.where(it == 0, qt, a - 1), 0)
        for n in range(1, C_STEPS_PER_TRIP):
            step(a + n, a + n - 1, n % 2)
        return carry

    lax.fori_loop(0, n_trips, body, 0)
    done = C_STEPS_PER_TRIP * n_trips
    rem = qt - done
    last = jnp.where(n_trips > 0, done - 1, qt)

    for n in range(C_STEPS_PER_TRIP - 1):
        @pl.when(rem > n)
        def _(n=n):
            step(done + n, last if n == 0 else done + n - 1, n % 2)

    final = jnp.where(rem == 0, last, done + rem - 1)
    for slot in range(2):
        @pl.when(rem % 2 == slot)
        def _(slot=slot):
            pv_update(final, slot)

    lam = lam_ref[0]
    heads = []
    for h in range(N_HEADS):
        a0 = acc_ref[2 * h]
        a1 = acc_ref[2 * h + 1]
        o_h = (a0[:HEAD_DIM] * (1.0 / a0[HEAD_DIM:HEAD_DIM + 1])
               - lam * (a1[:HEAD_DIM] * (1.0 / a1[HEAD_DIM:HEAD_DIM + 1])))
        ms = jnp.mean(o_h * o_h, axis=0, keepdims=True)
        heads.append(o_h * lax.rsqrt(ms + EPS))
    out_t = jnp.concatenate(heads, axis=0) * (on_ref[...] * (1.0 - lam_init))
    o_ref[...] = out_t.T.astype(BF16)


def _mix_c(lam, q_t, k, v_t, out_norm_col, nb, s, lam_init):
    nt = s // TQ_C
    tile_t = pl.BlockSpec((1, GROUP, TQ_C), lambda b, t, lam_ref: (b * nt + t, 0, 0))
    full_k = pl.BlockSpec((s, GROUP), lambda b, t, lam_ref: (b, 0), pipeline_mode=pl.Buffered(1))
    full_vt = pl.BlockSpec((nt, N_HEADS * VROWS, TK_C), lambda b, t, lam_ref: (b, 0, 0),
                           pipeline_mode=pl.Buffered(1))
    grid_spec = pltpu.PrefetchScalarGridSpec(
        num_scalar_prefetch=1,
        grid=(nb, nt),
        in_specs=[tile_t, full_k, full_vt,
                  pl.BlockSpec((GROUP, 1), lambda b, t, lam_ref: (0, 0), pipeline_mode=pl.Buffered(1))],
        out_specs=pl.BlockSpec((TQ_C, GROUP), lambda b, t, lam_ref: (b * nt + t, 0)),
        scratch_shapes=[pltpu.VMEM((2, 2 * N_HEADS, TK_C, TQ_C), BF16),
                        pltpu.VMEM((2 * N_HEADS, VROWS, TQ_C), F32),
                        pltpu.VMEM((2 * N_HEADS, 1, TQ_C), F32),
                        pltpu.VMEM((2 * N_HEADS, 1, TQ_C), F32)],
    )
    return pl.pallas_call(
        functools.partial(_mix_c_kernel, lam_init=lam_init),
        grid_spec=grid_spec,
        out_shape=jax.ShapeDtypeStruct((nb * s, GROUP), BF16),
        compiler_params=_cparams(("parallel", "parallel")),
        name="mix_c_diff_attn",
    )(lam, q_t, k, v_t, out_norm_col)


def _mix_d_kernel(qt_ref, k_ref, vt_ref, o_ref, acc_ref, run_ref):
    qt = pl.program_id(1)
    q_t = qt_ref[0]
    acc_ref[...] = jnp.zeros(acc_ref.shape, F32)
    run_ref[...] = jnp.zeros(run_ref.shape, F32)
    r = lax.broadcasted_iota(jnp.int32, (TK_D, TK_D), 0)
    c = lax.broadcasted_iota(jnp.int32, (TK_D, TK_D), 1)
    later = jnp.where(c > r, 1.0, 0.0).astype(BF16)

    def tile(kt, masked):
        start = pl.multiple_of(kt * TK_D, TK_D)
        k = k_ref[pl.ds(start, TK_D), :]
        v_t = vt_ref[kt]
        zs = [jnp.dot(k[:, h * HEAD_DIM:(h + 1) * HEAD_DIM], q_t[h * HEAD_DIM:(h + 1) * HEAD_DIM, :],
                      preferred_element_type=F32) for h in range(N_HEADS)]
        if masked:
            before = (lax.broadcasted_iota(jnp.int32, (TK_D, TQ_D), 0)
                      < lax.broadcasted_iota(jnp.int32, (TK_D, TQ_D), 1))
        worst = None
        log_take, between = [], []
        for h in range(N_HEADS):
            z = zs[h]
            log_keep = jnp.minimum(-z, 0.0) - jnp.log2(1.0 + jnp.exp2(-jnp.abs(z)))
            log_take.append(z + log_keep)
            if masked:
                log_keep = jnp.where(before, log_keep, 0.0)
            run = run_ref[h]
            between.append(run + _split_dot(log_keep, later, 2, x_is_lhs=False))
            run = run + jnp.sum(log_keep, axis=0, keepdims=True)
            run_ref[h] = run
            top = jnp.max(run)
            worst = top if worst is None else jnp.maximum(worst, top)
        for h in range(N_HEADS):
            a = jnp.exp2(log_take[h] + between[h])
            if masked:
                a = jnp.where(before, a, 0.0)
            acc_ref[h] += jnp.dot(v_t[h * HEAD_DIM:(h + 1) * HEAD_DIM, :], a.astype(BF16),
                                  preferred_element_type=F32)
        return worst

    worst = tile(qt, True)

    def cond(state):
        kt, worst = state
        return jnp.logical_and(kt >= 0, worst > SB_SKIP_LOG2)

    def body(state):
        kt, _ = state
        return kt - 1, tile(kt, False)

    lax.while_loop(cond, body, (qt - 1, worst))
    out_t = jnp.concatenate([acc_ref[h] for h in range(N_HEADS)], axis=0)
    o_ref[...] = out_t.T.astype(BF16)


def _mix_d(q_t, k, v_t, nb, s):
    nt = s // TQ_D
    tile_t = pl.BlockSpec((1, GROUP, TQ_D), lambda b, t: (b * nt + t, 0, 0))
    full_k = pl.BlockSpec((s, GROUP), lambda b, t: (b, 0), pipeline_mode=pl.Buffered(1))
    full_vt = pl.BlockSpec((nt, GROUP, TK_D), lambda b, t: (b, 0, 0), pipeline_mode=pl.Buffered(1))
    return pl.pallas_call(
        _mix_d_kernel,
        grid=(nb, nt),
        in_specs=[tile_t, full_k, full_vt],
        out_specs=pl.BlockSpec((TQ_D, GROUP), lambda b, t: (b * nt + t, 0)),
        out_shape=jax.ShapeDtypeStruct((nb * s, GROUP), BF16),
        scratch_shapes=[pltpu.VMEM((N_HEADS, HEAD_DIM, TQ_D), F32),
                        pltpu.VMEM((N_HEADS, 1, TQ_D), F32)],
        compiler_params=_cparams(("parallel", "parallel")),
        name="mix_d_stick_breaking",
    )(q_t, k, v_t)


def _rel_bias_tile(rel_bias):
    nk = TQ_A + LEFT_CHUNKS * CHUNK
    i = np.arange(TQ_A)[:, None]
    j = np.arange(nk)[None, :]
    dc = j // CHUNK - i // CHUNK
    visible = (dc >= 0) & (dc <= LEFT_CHUNKS)
    p = TQ_A + nk
    e = np.arange(p)
    d = np.where(e < nk, e, e - p)
    rel = LEFT_CHUNKS * CHUNK - d
    idx = np.clip(rel, -REL_CLIP, REL_CLIP) + REL_CLIP
    diag = rel_bias[:, idx].astype(F32)
    nh = rel_bias.shape[0]
    skew = jnp.tile(diag, (1, TQ_A))[:, :TQ_A * (p - 1)].reshape(nh, TQ_A, p - 1)
    toeplitz = skew[:, :, :nk]
    return jnp.swapaxes(jnp.where(jnp.asarray(visible)[None], toeplitz * LOG2E, NEG), 1, 2)


def _permute_w_in(w_in):
    sizes = [GROUP, GROUP, GROUP, 2 * GROUP, GROUP, GROUP, N_HEADS, N_HEADS,
             GROUP, GROUP, GROUP, GROUP, GROUP, GROUP]
    offs = np.concatenate([[0], np.cumsum(sizes)])
    seg = [w_in[:, offs[n]:offs[n + 1]] for n in range(len(sizes))]
    main = jnp.concatenate([seg[1], seg[3], seg[4], seg[5], seg[9], seg[12]], axis=1).astype(BF16)
    gates = jnp.concatenate([seg[6], seg[7]], axis=1)
    g_col = jnp.pad(gates, ((0, 0), (0, GATE_LANES - 2 * N_HEADS))).astype(BF16)
    g_row = jnp.pad(gates.T, ((0, GATE_ROWS - 2 * N_HEADS), (0, 0)))
    c_t = jnp.concatenate([seg[n].T for n in (8, 10, 11, 13, 0, 2)] + [g_row], axis=0).astype(BF16)
    return main, c_t, g_col


def _layer(x, nb, s, lam_init, p):
    x = _ffn(x, p["ffn1_norm"], p["ffn1_wg"].astype(BF16), p["ffn1_wu"].astype(BF16), p["ffn1_wd"].astype(BF16))

    w_main, w_ct, w_gc = _permute_w_in(p["w_in"])
    gate_bias = p["b_gate_bias"].reshape(2 * N_HEADS).astype(F32)
    gb_col = jnp.pad(gate_bias, (0, GATE_LANES - 2 * N_HEADS)).reshape(1, GATE_LANES)
    gb_row = jnp.pad(gate_bias, (0, GATE_ROWS - 2 * N_HEADS)).reshape(GATE_ROWS, 1)
    tile4 = lambda g: jnp.tile(g.astype(F32), GROUP // g.shape[0]).reshape(1, GROUP)
    (aq_t, ak, av_t, bqk, bv, bo, gcol, grow, cq_t, ck, cv_t, dq_t, dk, dv_t) = _proj(
        x, p["mix_norm"], w_main, w_ct, w_gc, gb_col, gb_row,
        tile4(p["a_q_norm"]).reshape(GROUP, 1), tile4(p["a_k_norm"]),
        tile4(p["c_q_norm"]).reshape(GROUP, 1), tile4(p["c_k_norm"]))

    ya = _mix_a(aq_t, ak, av_t, _rel_bias_tile(p["a_rel_bias"]), nb, s)
    yb = _mix_b(bqk, bv, bo, gcol, grow, p["b_conv_w"].astype(F32), p["b_conv_b"].astype(F32).reshape(1, -1),
                p["b_out_norm"].astype(F32).reshape(GROUP, 1), nb, s)
    lv = p["c_lambda"].astype(F32)
    lam = jnp.exp(jnp.sum(lv[0] * lv[1])) - jnp.exp(jnp.sum(lv[2] * lv[3])) + lam_init
    yc = _mix_c(lam.reshape(1), cq_t, ck, cv_t, tile4(p["c_out_norm"]).reshape(GROUP, 1), nb, s, lam_init)
    yd = _mix_d(dq_t, dk, dv_t, nb, s)

    return _mix_out_ffn(x, ya, yb, yc, yd, p["w_out"].astype(BF16), p["ffn2_norm"], p["ffn2_wg"].astype(BF16),
                        p["ffn2_wu"].astype(BF16), p["ffn2_wd"].astype(BF16))


_PARAM_NAMES = ("ffn1_norm", "ffn1_wg", "ffn1_wu", "ffn1_wd", "mix_norm", "w_in", "a_q_norm", "a_k_norm",
                "a_rel_bias", "b_conv_w", "b_conv_b", "b_gate_bias", "b_out_norm", "c_q_norm", "c_k_norm",
                "c_lambda", "c_out_norm", "w_out", "ffn2_norm", "ffn2_wg", "ffn2_wu", "ffn2_wd")


def kernel(x, ffn1_norm, ffn1_wg, ffn1_wu, ffn1_wd, mix_norm, w_in, a_q_norm, a_k_norm, a_rel_bias,
           b_conv_w, b_conv_b, b_gate_bias, b_out_norm, c_q_norm, c_k_norm, c_lambda, c_out_norm,
           w_out, ffn2_norm, ffn2_wg, ffn2_wu, ffn2_wd):
    params = dict(zip(_PARAM_NAMES, (ffn1_norm, ffn1_wg, ffn1_wu, ffn1_wd, mix_norm, w_in, a_q_norm, a_k_norm,
                                     a_rel_bias, b_conv_w, b_conv_b, b_gate_bias, b_out_norm, c_q_norm,
                                     c_k_norm, c_lambda, c_out_norm, w_out, ffn2_norm, ffn2_wg, ffn2_wu,
                                     ffn2_wd)))
    nb, s, d = x.shape
    depth = ffn1_norm.shape[0]
    h = x.reshape(nb * s, d)
    for l in range(depth):
        lam_init = 0.8 - 0.6 * math.exp(-0.3 * l)
        h = _layer(h, nb, s, lam_init, {k: v[l] for k, v in params.items()})
    return h.reshape(nb, s, d)
```

```python
import functools
import math

import jax
import jax.numpy as jnp
import numpy as np
from jax import lax
from jax.experimental import pallas as pl
from jax.experimental.pallas import tpu as pltpu

F32 = jnp.float32
BF16 = jnp.bfloat16

D_MODEL = 1024
D_FF = 2816
CHUNK = 64
HEAD_DIM = 64
N_HEADS = 4
GROUP = N_HEADS * HEAD_DIM
LEFT_CHUNKS = 8
REL_CLIP = 128
CONV_WIDTH = 4
DIFF_QK_DIM = HEAD_DIM // 2
EPS = 1e-6
NEG = -1e30
LOG2E = 1.4426950408889634

OFF_AK = 0
OFF_BQK, OFF_BV, OFF_BO = 256, 768, 1024
OFF_CK = 1280
OFF_DK = 1536
MAIN_COLS = 1792
N_T_GROUPS = 6
GATE_LANES = 128
GATE_ROWS = 16
ONES_ROWS = 16
VROWS = HEAD_DIM + ONES_ROWS

VMEM_LIMIT = 56 * 1024 * 1024

TM_FFN = 512
TM_PROJ = 512
TILE_T = 256
TQ_A = 256
LB = 256
TQ_C = 256
TK_C = 256
C_STEPS_PER_TRIP = 8
assert C_STEPS_PER_TRIP % 4 == 0
TQ_D = 256
TK_D = 256
SB_SKIP_LOG2 = -160.0


def _cparams(sem):
    return pltpu.CompilerParams(dimension_semantics=sem, vmem_limit_bytes=VMEM_LIMIT)


def _const_spec(shape):
    nd = len(shape)
    return pl.BlockSpec(shape, lambda *_: (0,) * nd, pipeline_mode=pl.Buffered(1))


def _split_dot(x, mat, terms, x_is_lhs=True):
    acc = None
    rem = x
    for term in range(terms):
        part = rem.astype(BF16)
        if term + 1 < terms:
            rem = rem - part.astype(F32)
        d = (jnp.dot(part, mat, preferred_element_type=F32) if x_is_lhs
             else jnp.dot(mat, part, preferred_element_type=F32))
        acc = d if acc is None else acc + d
    return acc


def _group_ones(width):
    r = lax.broadcasted_iota(jnp.int32, (GROUP, GROUP), 0) // width
    c = lax.broadcasted_iota(jnp.int32, (GROUP, GROUP), 1) // width
    return jnp.where(r == c, 1.0, 0.0).astype(BF16)


def _group_rms(x, gain, width):
    ss = _split_dot(x * x, _group_ones(width), 2)
    return x * lax.rsqrt(ss * (1.0 / width) + EPS) * gain


def _log_sigmoid(x):
    return jnp.minimum(x, 0.0) - jnp.log1p(jnp.exp(-jnp.abs(x)))


def _head_mask(shape, h, axis=1, width=HEAD_DIM):
    lane = lax.broadcasted_iota(jnp.int32, shape, axis)
    return (lane // width) == h


def _ffn_half_step(x, g_ref, wg_ref, wu_ref, wd_ref):
    xn = x * lax.rsqrt(jnp.mean(x * x, axis=-1, keepdims=True) + EPS) * g_ref[...]
    xb = xn.astype(BF16)
    g = jnp.dot(xb, wg_ref[...], preferred_element_type=F32)
    u = jnp.dot(xb, wu_ref[...], preferred_element_type=F32)
    h = (g * jax.nn.sigmoid(g) * u).astype(BF16)
    return x + 0.5 * jnp.dot(h, wd_ref[...], preferred_element_type=F32)


def _ffn_kernel(x_ref, g_ref, wg_ref, wu_ref, wd_ref, o_ref):
    o_ref[...] = _ffn_half_step(x_ref[...], g_ref, wg_ref, wu_ref, wd_ref)


def _mix_out_ffn_kernel(x_ref, ya_ref, yb_ref, yc_ref, yd_ref, wo_ref, g_ref, wg_ref, wu_ref, wd_ref, o_ref):
    x = x_ref[...]
    for grp, y_ref in enumerate((ya_ref, yb_ref, yc_ref, yd_ref)):
        x = x + jnp.dot(y_ref[...], wo_ref[grp * GROUP:(grp + 1) * GROUP, :], preferred_element_type=F32)
    o_ref[...] = _ffn_half_step(x, g_ref, wg_ref, wu_ref, wd_ref)


def _ffn_specs():
    return [_const_spec((1, D_MODEL)), _const_spec((D_MODEL, D_FF)), _const_spec((D_MODEL, D_FF)),
            _const_spec((D_FF, D_MODEL))]


def _ffn(x, gain, wg, wu, wd):
    t = x.shape[0]
    tm = min(TM_FFN, t)
    row = pl.BlockSpec((tm, D_MODEL), lambda i: (i, 0))
    return pl.pallas_call(
        _ffn_kernel,
        grid=(t // tm,),
        in_specs=[row] + _ffn_specs(),
        out_specs=row,
        out_shape=jax.ShapeDtypeStruct((t, D_MODEL), F32),
        compiler_params=_cparams(("parallel",)),
        name="ffn_half_step",
    )(x, gain.reshape(1, D_MODEL), wg, wu, wd)


def _mix_out_ffn(x, ya, yb, yc, yd, w_out, gain, wg, wu, wd):
    t = x.shape[0]
    tm = min(TM_FFN, t)
    row = lambda w: pl.BlockSpec((tm, w), lambda i: (i, 0))
    return pl.pallas_call(
        _mix_out_ffn_kernel,
        grid=(t // tm,),
        in_specs=[row(D_MODEL), row(GROUP), row(GROUP), row(GROUP), row(GROUP),
                  _const_spec((D_MODEL, D_MODEL))] + _ffn_specs(),
        out_specs=row(D_MODEL),
        out_shape=jax.ShapeDtypeStruct((t, D_MODEL), F32),
        compiler_params=_cparams(("parallel",)),
        name="mix_out_proj_ffn",
    )(x, ya, yb, yc, yd, w_out, gain.reshape(1, D_MODEL), wg, wu, wd)


def _proj_kernel(x_ref, g_ref, w_ref, wct_ref, wgc_ref, gbc_ref, gbr_ref,
                 aqn_ref, akn_ref, cqn_ref, ckn_ref,
                 aq_ref, ak_ref, av_ref, bqk_ref, bv_ref, bo_ref, gc_ref, gr_ref,
                 cq_ref, ck_ref, cv_ref, dq_ref, dk_ref, dv_ref):
    x = x_ref[...]
    hn = x * lax.rsqrt(jnp.mean(x * x, axis=-1, keepdims=True) + EPS) * g_ref[...]
    hb = hn.astype(BF16)

    def cols(off, width=GROUP):
        return jnp.dot(hb, w_ref[:, off:off + width], preferred_element_type=F32)

    nt_dims = (((1,), (1,)), ((), ()))

    all_t = lax.dot_general(wct_ref[...], hb, nt_dims, preferred_element_type=F32)

    def rows_t(idx):
        return all_t[idx * GROUP:(idx + 1) * GROUP, :]

    def group_rms_t(x_t, gain_col, width):
        ss = _split_dot(x_t * x_t, _group_ones(width), 2, x_is_lhs=False)
        return x_t * lax.rsqrt(ss * (1.0 / width) + EPS) * gain_col

    aqt = (group_rms_t(rows_t(4), aqn_ref[...], HEAD_DIM) * (HEAD_DIM ** -0.5 * LOG2E)).astype(BF16)
    ak_ref[...] = _group_rms(cols(OFF_AK), akn_ref[...], HEAD_DIM).astype(BF16)
    avt = rows_t(5).astype(BF16)
    bqk_ref[...] = cols(OFF_BQK, 2 * GROUP)
    bv_ref[...] = cols(OFF_BV).astype(BF16)
    bo_ref[...] = jax.nn.sigmoid(cols(OFF_BO))
    gcol = jnp.dot(hb, wgc_ref[...], preferred_element_type=F32) + gbc_ref[...]
    lane = lax.broadcasted_iota(jnp.int32, gcol.shape, 1)
    gc_ref[...] = jnp.where(lane >= N_HEADS, _log_sigmoid(gcol), gcol)
    grow = all_t[N_T_GROUPS * GROUP:, :] + gbr_ref[...]
    row = lax.broadcasted_iota(jnp.int32, grow.shape, 0)
    gr_ref[...] = jnp.where(row >= N_HEADS, _log_sigmoid(grow), grow)
    ck_ref[...] = _group_rms(cols(OFF_CK), ckn_ref[...], DIFF_QK_DIM).astype(BF16)
    cqt = (group_rms_t(rows_t(0), cqn_ref[...], DIFF_QK_DIM) * (DIFF_QK_DIM ** -0.5 * LOG2E)).astype(BF16)
    cvt = rows_t(1).astype(BF16)
    dk_ref[...] = cols(OFF_DK).astype(BF16)
    dqt = (rows_t(2) * (HEAD_DIM ** -0.5 * LOG2E)).astype(BF16)
    dvt = rows_t(3).astype(BF16)
    for sub in range(cq_ref.shape[0]):
        tok = slice(sub * TILE_T, (sub + 1) * TILE_T)
        aq_ref[sub] = aqt[:, tok]
        av_ref[sub] = avt[:, tok]
        cq_ref[sub] = cqt[:, tok]
        dq_ref[sub] = dqt[:, tok]
        dv_ref[sub] = dvt[:, tok]
        for h in range(N_HEADS):
            cv_ref[sub, h * VROWS:h * VROWS + HEAD_DIM, :] = cvt[h * HEAD_DIM:(h + 1) * HEAD_DIM, tok]
            cv_ref[sub, h * VROWS + HEAD_DIM:(h + 1) * VROWS, :] = jnp.ones((ONES_ROWS, TILE_T), BF16)


def _proj(x, gain, w_main, w_ct, w_gc, gb_col, gb_row, aqn, akn, cqn_col, ckn):
    t = x.shape[0]
    tm = TM_PROJ
    row_spec = lambda w: pl.BlockSpec((tm, w), lambda i: (i, 0))
    bf = lambda w: jax.ShapeDtypeStruct((t, w), BF16)
    f32 = lambda w: jax.ShapeDtypeStruct((t, w), F32)
    n_sub = tm // TILE_T
    tr_shape = jax.ShapeDtypeStruct((t // TILE_T, GROUP, TILE_T), BF16)
    tr_spec = pl.BlockSpec((n_sub, GROUP, TILE_T), lambda i: (i, 0, 0))
    vt_shape = jax.ShapeDtypeStruct((t // TILE_T, N_HEADS * VROWS, TILE_T), BF16)
    vt_spec = pl.BlockSpec((n_sub, N_HEADS * VROWS, TILE_T), lambda i: (i, 0, 0))
    out_shape = [tr_shape, bf(GROUP), tr_shape,
                 f32(2 * GROUP), bf(GROUP), f32(GROUP),
                 f32(GATE_LANES), jax.ShapeDtypeStruct((GATE_ROWS, t), F32),
                 tr_shape, bf(GROUP), vt_shape,
                 tr_shape, bf(GROUP), tr_shape]
    out_specs = ([tr_spec, row_spec(GROUP), tr_spec, row_spec(2 * GROUP), row_spec(GROUP), row_spec(GROUP),
                  row_spec(GATE_LANES), pl.BlockSpec((GATE_ROWS, tm), lambda i: (0, i))]
                 + [tr_spec, row_spec(GROUP), vt_spec, tr_spec, row_spec(GROUP), tr_spec])
    return pl.pallas_call(
        _proj_kernel,
        grid=(t // tm,),
        in_specs=[row_spec(D_MODEL), _const_spec((1, D_MODEL)),
                  _const_spec((D_MODEL, MAIN_COLS)), _const_spec((N_T_GROUPS * GROUP + GATE_ROWS, D_MODEL)),
                  _const_spec((D_MODEL, GATE_LANES)), _const_spec((1, GATE_LANES)),
                  _const_spec((GATE_ROWS, 1)),
                  _const_spec((GROUP, 1)), _const_spec((1, GROUP)),
                  _const_spec((GROUP, 1)), _const_spec((1, GROUP))],
        out_specs=out_specs,
        out_shape=out_shape,
        compiler_params=_cparams(("parallel",)),
        name="mix_in_proj",
    )(x, gain.reshape(1, D_MODEL), w_main, w_ct, w_gc, gb_col, gb_row, aqn, akn, cqn_col, ckn)


def _mix_a_kernel(qt_ref, k0_ref, k1_ref, k2_ref, v0_ref, v1_ref, v2_ref, bias_ref, o_ref):
    t = pl.program_id(1)
    q_t = qt_ref[0]
    k = jnp.concatenate([k0_ref[...], k1_ref[...], k2_ref[...]], axis=0)
    v_t = jnp.concatenate([v0_ref[0], v1_ref[0], v2_ref[0]], axis=1)
    nk = k.shape[0]

    def attend(mask_start):
        scores = [jnp.dot(k[:, h * HEAD_DIM:(h + 1) * HEAD_DIM], q_t[h * HEAD_DIM:(h + 1) * HEAD_DIM, :],
                          preferred_element_type=F32) for h in range(N_HEADS)]
        if mask_start:
            key_pos = t * TQ_A - LEFT_CHUNKS * CHUNK + lax.broadcasted_iota(jnp.int32, (nk, TQ_A), 0)
            valid = key_pos >= 0
        probs = []
        for h in range(N_HEADS):
            s = scores[h] + bias_ref[h]
            if mask_start:
                s = jnp.where(valid, s, NEG)
            e = jnp.exp2(s - jnp.max(s, axis=0, keepdims=True))
            probs.append((e * (1.0 / jnp.sum(e, axis=0, keepdims=True))).astype(BF16))
        heads = [jnp.dot(v_t[h * HEAD_DIM:(h + 1) * HEAD_DIM, :], probs[h], preferred_element_type=F32)
                 for h in range(N_HEADS)]
        o_ref[...] = jnp.concatenate(heads, axis=0).T.astype(BF16)

    n_start_tiles = LEFT_CHUNKS * CHUNK // TQ_A

    @pl.when(t < n_start_tiles)
    def _():
        attend(True)

    @pl.when(t >= n_start_tiles)
    def _():
        attend(False)


def _mix_a(q_t, k, v_t, bias_tile_t, nb, s):
    nt = s // TQ_A
    back_idx = lambda b, t, back: b * nt + jnp.maximum(t - back, 0)
    rows = lambda back: pl.BlockSpec((TQ_A, GROUP), lambda b, t: (back_idx(b, t, back), 0))
    tr = lambda back: pl.BlockSpec((1, GROUP, TQ_A), lambda b, t: (back_idx(b, t, back), 0, 0))
    return pl.pallas_call(
        _mix_a_kernel,
        grid=(nb, nt),
        in_specs=[tr(0), rows(2), rows(1), rows(0), tr(2), tr(1), tr(0), _const_spec(bias_tile_t.shape)],
        out_specs=rows(0),
        out_shape=jax.ShapeDtypeStruct((nb * s, GROUP), BF16),
        compiler_params=_cparams(("parallel", "parallel")),
        name="mix_a_chunk_attn",
    )(q_t, k, k, k, v_t, v_t, v_t, bias_tile_t)


def _mix_b_kernel(qk_ref, v_ref, og_ref, gc_ref, gr_ref, cw_ref, cb_ref, on_ref, o_ref,
                  xs_ref, ct_ref, n_ref, m_ref):
    c_idx = pl.program_id(1)

    @pl.when(c_idx == 0)
    def _():
        xs_ref[0:8, :] = jnp.zeros((8, 2 * GROUP), F32)
        ct_ref[...] = jnp.zeros_like(ct_ref)
        n_ref[...] = jnp.zeros_like(n_ref)
        m_ref[...] = jnp.zeros_like(m_ref)

    xs_ref[8:8 + LB, :] = qk_ref[...]
    acc = jnp.broadcast_to(cb_ref[...], (LB, 2 * GROUP))
    for j in range(CONV_WIDTH):
        start = 8 - (CONV_WIDTH - 1) + j
        acc = acc + xs_ref[start:start + LB, :] * cw_ref[j:j + 1, :]
    xs_ref[0:8, :] = xs_ref[LB:LB + 8, :]
    qk = acc * jax.nn.sigmoid(acc)
    q_t = qk[:, :GROUP].T.astype(BF16)
    kb = (qk[:, GROUP:] * (HEAD_DIM ** -0.5)).astype(BF16)
    v_t = v_ref[...].astype(F32).T.astype(BF16)

    r = lax.broadcasted_iota(jnp.int32, (LB, LB), 0)
    c = lax.broadcasted_iota(jnp.int32, (LB, LB), 1)
    tri = jnp.where(c <= r, 1.0, 0.0).astype(BF16)
    tri_t = jnp.where(r <= c, 1.0, 0.0).astype(BF16)
    visible = r <= c
    gcol = gc_ref[...]
    grow = gr_ref[...]
    bcum_col = _split_dot(gcol, tri, 3, x_is_lhs=False)
    bcum_row = _split_dot(grow, tri_t, 3)

    ct_state = ct_ref[...]
    n_state = n_ref[...]
    inter_num = jnp.dot(ct_state.astype(BF16), q_t, preferred_element_type=F32)
    n16 = jnp.concatenate([n_state, jnp.zeros_like(n_state)], axis=0).astype(BF16)
    inter_den = jnp.dot(n16, q_t, preferred_element_type=F32)

    row8 = lax.broadcasted_iota(jnp.int32, (8, GROUP), 0)
    col8 = lax.broadcasted_iota(jnp.int32, (8, GROUP), 1) // HEAD_DIM
    lane1 = lax.broadcasted_iota(jnp.int32, (1, GROUP), 1) // HEAD_DIM
    a_full = jnp.zeros((1, GROUP), F32)
    a8 = jnp.zeros((8, GROUP), F32)
    heads, vw_rows, wg_rows = [], [], []
    for h in range(N_HEADS):
        ch = slice(h * HEAD_DIM, (h + 1) * HEAD_DIM)
        i_row = grow[h:h + 1, :]
        b_row = bcum_row[N_HEADS + h:N_HEADS + h + 1, :]
        u_col = bcum_col[:, N_HEADS + h:N_HEADS + h + 1] - gcol[:, h:h + 1]
        m_prev = m_ref[h:h + 1, 0:1]
        dmat = jnp.where(visible, b_row - u_col, NEG)
        inter = b_row + m_prev
        m_t = jnp.maximum(inter, jnp.max(dmat, axis=0, keepdims=True))
        w_intra = jnp.exp(dmat - m_t)
        s_inter = jnp.exp(inter - m_t)
        sc = jnp.dot(kb[:, ch], q_t[ch, :], preferred_element_type=F32) * w_intra
        num = s_inter * inter_num[ch, :] + jnp.dot(v_t[ch, :], sc.astype(BF16), preferred_element_type=F32)
        den = s_inter * inter_den[h:h + 1, :] + jnp.sum(sc, axis=0, keepdims=True)
        h_t = num * (1.0 / jnp.maximum(jnp.abs(den), jnp.exp(-m_t)))
        heads.append(h_t * lax.rsqrt(jnp.mean(h_t * h_t, axis=0, keepdims=True) + EPS))
        b_tot = b_row[:, LB - 1:LB]
        g = b_tot - b_row + i_row
        m_new = jnp.maximum(b_tot + m_prev, jnp.max(g, axis=1, keepdims=True))
        a_h = jnp.exp(b_tot + m_prev - m_new)
        wg = jnp.exp(g - m_new)
        a_full = jnp.where(lane1 == h, a_h, a_full)
        a8 = jnp.where(row8 == h, a_h, a8)
        vw_rows.append((v_t[ch, :].astype(F32) * wg).astype(BF16))
        wg_rows.append(wg)
        m_ref[h:h + 1, :] = jnp.broadcast_to(m_new, (1, m_ref.shape[1]))

    out_t = jnp.concatenate(heads, axis=0) * on_ref[...]
    o_ref[...] = (out_t.T * og_ref[...]).astype(BF16)

    vw = jnp.concatenate(vw_rows, axis=0)
    kv_t = jnp.dot(vw, kb, preferred_element_type=F32)
    r2 = lax.broadcasted_iota(jnp.int32, (GROUP, GROUP), 0) // HEAD_DIM
    c2 = lax.broadcasted_iota(jnp.int32, (GROUP, GROUP), 1) // HEAD_DIM
    ct_ref[...] = a_full * ct_state + jnp.where(r2 == c2, kv_t, 0.0)
    wg16 = jnp.concatenate(wg_rows + [jnp.zeros((GATE_ROWS - N_HEADS, LB), F32)], axis=0).astype(BF16)
    n_add = jnp.dot(wg16, kb, preferred_element_type=F32)[0:8, :]
    n_ref[...] = a8 * n_state + jnp.where(row8 == col8, n_add, 0.0)


def _mix_b(bqk, bv, bo, gcol, grow, conv_w, conv_b, out_norm_col, nb, s):
    nc = s // LB
    row = lambda w: pl.BlockSpec((LB, w), lambda b, c: (b * nc + c, 0))
    return pl.pallas_call(
        _mix_b_kernel,
        grid=(nb, nc),
        in_specs=[row(2 * GROUP), row(GROUP), row(GROUP), row(GATE_LANES),
                  pl.BlockSpec((GATE_ROWS, LB), lambda b, c: (0, b * nc + c)),
                  _const_spec((CONV_WIDTH, 2 * GROUP)), _const_spec((1, 2 * GROUP)),
                  _const_spec((GROUP, 1))],
        out_specs=row(GROUP),
        out_shape=jax.ShapeDtypeStruct((nb * s, GROUP), BF16),
        scratch_shapes=[pltpu.VMEM((LB + 8, 2 * GROUP), F32),
                        pltpu.VMEM((GROUP, GROUP), F32),
                        pltpu.VMEM((8, GROUP), F32),
                        pltpu.VMEM((8, 128), F32)],
        compiler_params=_cparams(("parallel", "arbitrary")),
        name="mix_b_mlstm",
    )(bqk, bv, bo, gcol, grow, conv_w, conv_b, out_norm_col)


def _mix_c_kernel(lam_ref, qt_ref, k_ref, vt_ref, on_ref, o_ref, p_ref, acc_ref, m_ref, al_ref, *, lam_init):
    qt = pl.program_id(1)
    q_t = qt_ref[0]
    n_maps = 2 * N_HEADS
    acc_ref[...] = jnp.zeros(acc_ref.shape, F32)

    def key_tile(kt):
        return k_ref[pl.ds(pl.multiple_of(kt * TK_C, TK_C), TK_C), :]

    def score_map(k, j):
        ch = slice(j * DIFF_QK_DIM, (j + 1) * DIFF_QK_DIM)
        return jnp.dot(k[:, ch], q_t[ch, :], preferred_element_type=F32)

    def softmax(sc, slot, masked=False):
        if masked:
            key_chunk = lax.broadcasted_iota(jnp.int32, (TK_C, TQ_C), 0) // CHUNK
            q_chunk = lax.broadcasted_iota(jnp.int32, (TK_C, TQ_C), 1) // CHUNK
            vis = key_chunk <= q_chunk
        for j in range(n_maps):
            s = jnp.where(vis, sc[j], NEG) if masked else sc[j]
            m_old = m_ref[j]
            m_new = jnp.maximum(m_old, jnp.max(s, axis=0, keepdims=True))
            al_ref[j] = jnp.exp2(m_old - m_new)
            m_ref[j] = m_new
            p_ref[slot, j] = jnp.exp2(s - m_new).astype(BF16)

    def pv_update(kt, slot):
        v_ext = vt_ref[kt]
        for j in range(n_maps):
            h = j // 2
            pv = jnp.dot(v_ext[h * VROWS:(h + 1) * VROWS, :], p_ref[slot, j], preferred_element_type=F32)
            acc_ref[j] = al_ref[j] * acc_ref[j] + pv

    def step(kt, prev, slot):
        k = key_tile(kt)
        v_ext = vt_ref[prev]
        sc = []
        for j in range(n_maps):
            h = j // 2
            sc.append(score_map(k, j))
            pv = jnp.dot(v_ext[h * VROWS:(h + 1) * VROWS, :], p_ref[slot, j], preferred_element_type=F32)
            acc_ref[j] = al_ref[j] * acc_ref[j] + pv
        softmax(sc, 1 - slot)

    m_ref[...] = jnp.full(m_ref.shape, NEG, F32)
    k_diag = key_tile(qt)
    softmax([score_map(k_diag, j) for j in range(n_maps)], 0, masked=True)
    n_trips = qt // C_STEPS_PER_TRIP

    def body(it, carry):
        a = C_STEPS_PER_TRIP * it
        step(a, jnp.where(it == 0, qt, a - 1), 0)
        for n in range(1, C_STEPS_PER_TRIP):
            step(a + n, a + n - 1, n % 2)
        return carry

    lax.fori_loop(0, n_trips, body, 0)
    done = C_STEPS_PER_TRIP * n_trips
    rem = qt - done
    last = jnp.where(n_trips > 0, done - 1, qt)

    half = C_STEPS_PER_TRIP // 2
    has_half = rem >= half

    @pl.when(has_half)
    def _():
        step(done, last, 0)
        for n in range(1, half):
            step(done + n, done + n - 1, n % 2)

    done = jnp.where(has_half, done + half, done)
    last = jnp.where(has_half, done - 1, last)
    rem = jnp.where(has_half, rem - half, rem)

    for n in range(half - 1):
        @pl.when(rem > n)
        def _(n=n):
            step(done + n, last if n == 0 else done + n - 1, n % 2)

    final = jnp.where(rem == 0, last, done + rem - 1)
    for slot in range(2):
        @pl.when(rem % 2 == slot)
        def _(slot=slot):
            pv_update(final, slot)

    lam = lam_ref[0]
    heads = []
    for h in range(N_HEADS):
        a0 = acc_ref[2 * h]
        a1 = acc_ref[2 * h + 1]
        o_h = (a0[:HEAD_DIM] * (1.0 / a0[HEAD_DIM:HEAD_DIM + 1])
               - lam * (a1[:HEAD_DIM] * (1.0 / a1[HEAD_DIM:HEAD_DIM + 1])))
        ms = jnp.mean(o_h * o_h, axis=0, keepdims=True)
        heads.append(o_h * lax.rsqrt(ms + EPS))
    out_t = jnp.concatenate(heads, axis=0) * (on_ref[...] * (1.0 - lam_init))
    o_ref[...] = out_t.T.astype(BF16)


def _mix_c(lam, q_t, k, v_t, out_norm_col, nb, s, lam_init):
    nt = s // TQ_C
    tile_t = pl.BlockSpec((1, GROUP, TQ_C), lambda b, t, lam_ref: (b * nt + t, 0, 0))
    full_k = pl.BlockSpec((s, GROUP), lambda b, t, lam_ref: (b, 0), pipeline_mode=pl.Buffered(1))
    full_vt = pl.BlockSpec((nt, N_HEADS * VROWS, TK_C), lambda b, t, lam_ref: (b, 0, 0),
                           pipeline_mode=pl.Buffered(1))
    grid_spec = pltpu.PrefetchScalarGridSpec(
        num_scalar_prefetch=1,
        grid=(nb, nt),
        in_specs=[tile_t, full_k, full_vt,
                  pl.BlockSpec((GROUP, 1), lambda b, t, lam_ref: (0, 0), pipeline_mode=pl.Buffered(1))],
        out_specs=pl.BlockSpec((TQ_C, GROUP), lambda b, t, lam_ref: (b * nt + t, 0)),
        scratch_shapes=[pltpu.VMEM((2, 2 * N_HEADS, TK_C, TQ_C), BF16),
                        pltpu.VMEM((2 * N_HEADS, VROWS, TQ_C), F32),
                        pltpu.VMEM((2 * N_HEADS, 1, TQ_C), F32),
                        pltpu.VMEM((2 * N_HEADS, 1, TQ_C), F32)],
    )
    return pl.pallas_call(
        functools.partial(_mix_c_kernel, lam_init=lam_init),
        grid_spec=grid_spec,
        out_shape=jax.ShapeDtypeStruct((nb * s, GROUP), BF16),
        compiler_params=_cparams(("parallel", "parallel")),
        name="mix_c_diff_attn",
    )(lam, q_t, k, v_t, out_norm_col)


def _mix_d_kernel(qt_ref, k_ref, vt_ref, o_ref, acc_ref, run_ref):
    qt = pl.program_id(1)
    q_t = qt_ref[0]
    acc_ref[...] = jnp.zeros(acc_ref.shape, F32)
    run_ref[...] = jnp.zeros(run_ref.shape, F32)
    r = lax.broadcasted_iota(jnp.int32, (TK_D, TK_D), 0)
    c = lax.broadcasted_iota(jnp.int32, (TK_D, TK_D), 1)
    later = jnp.where(c > r, 1.0, 0.0).astype(BF16)

    before = (lax.broadcasted_iota(jnp.int32, (TK_D, TQ_D), 0)
              < lax.broadcasted_iota(jnp.int32, (TK_D, TQ_D), 1))

    def add_tiles(specs):
        loaded = []
        for kt, diagonal in specs:
            k = k_ref[pl.ds(pl.multiple_of(kt * TK_D, TK_D), TK_D), :]
            zs = [jnp.dot(k[:, h * HEAD_DIM:(h + 1) * HEAD_DIM], q_t[h * HEAD_DIM:(h + 1) * HEAD_DIM, :],
                          preferred_element_type=F32) for h in range(N_HEADS)]
            loaded.append((vt_ref[kt], zs, diagonal))
        pending = []
        for v_t, zs, diagonal in loaded:
            for h in range(N_HEADS):
                z = zs[h]
                log_keep = jnp.minimum(-z, 0.0) - jnp.log2(1.0 + jnp.exp2(-jnp.abs(z)))
                log_take = z + log_keep
                if diagonal:
                    log_keep = jnp.where(before, log_keep, 0.0)
                run = run_ref[h]
                pending.append((h, v_t, diagonal, log_take, run + _split_dot(log_keep, later, 2, x_is_lhs=False)))
                run_ref[h] = run + jnp.sum(log_keep, axis=0, keepdims=True)
        for h, v_t, diagonal, log_take, between in pending:
            a = jnp.exp2(log_take + between)
            if diagonal:
                a = jnp.where(before, a, 0.0)
            acc_ref[h] += jnp.dot(v_t[h * HEAD_DIM:(h + 1) * HEAD_DIM, :], a.astype(BF16),
                                  preferred_element_type=F32)

    def worst_run():
        return functools.reduce(jnp.maximum, [jnp.max(run_ref[h]) for h in range(N_HEADS)])

    @pl.when(qt == 0)
    def _():
        add_tiles([(qt, True)])

    @pl.when(qt > 0)
    def _():
        add_tiles([(qt, True), (qt - 1, False)])

    def cond(state):
        kt, worst = state
        return jnp.logical_and(kt >= 0, worst > SB_SKIP_LOG2)

    def body(state):
        kt, _ = state
        add_tiles([(kt, False)])
        return kt - 1, worst_run()

    lax.while_loop(cond, body, (qt - 2, worst_run()))
    out_t = jnp.concatenate([acc_ref[h] for h in range(N_HEADS)], axis=0)
    o_ref[...] = out_t.T.astype(BF16)


def _mix_d(q_t, k, v_t, nb, s):
    nt = s // TQ_D
    tile_t = pl.BlockSpec((1, GROUP, TQ_D), lambda b, t: (b * nt + t, 0, 0))
    full_k = pl.BlockSpec((s, GROUP), lambda b, t: (b, 0), pipeline_mode=pl.Buffered(1))
    full_vt = pl.BlockSpec((nt, GROUP, TK_D), lambda b, t: (b, 0, 0), pipeline_mode=pl.Buffered(1))
    return pl.pallas_call(
        _mix_d_kernel,
        grid=(nb, nt),
        in_specs=[tile_t, full_k, full_vt],
        out_specs=pl.BlockSpec((TQ_D, GROUP), lambda b, t: (b * nt + t, 0)),
        out_shape=jax.ShapeDtypeStruct((nb * s, GROUP), BF16),
        scratch_shapes=[pltpu.VMEM((N_HEADS, HEAD_DIM, TQ_D), F32),
                        pltpu.VMEM((N_HEADS, 1, TQ_D), F32)],
        compiler_params=_cparams(("parallel", "parallel")),
        name="mix_d_stick_breaking",
    )(q_t, k, v_t)


def _rel_bias_tile(rel_bias):
    nk = TQ_A + LEFT_CHUNKS * CHUNK
    i = np.arange(TQ_A)[:, None]
    j = np.arange(nk)[None, :]
    dc = j // CHUNK - i // CHUNK
    visible = (dc >= 0) & (dc <= LEFT_CHUNKS)
    p = TQ_A + nk
    e = np.arange(p)
    d = np.where(e < nk, e, e - p)
    rel = LEFT_CHUNKS * CHUNK - d
    idx = np.clip(rel, -REL_CLIP, REL_CLIP) + REL_CLIP
    diag = rel_bias[:, idx].astype(F32)
    nh = rel_bias.shape[0]
    skew = jnp.tile(diag, (1, TQ_A))[:, :TQ_A * (p - 1)].reshape(nh, TQ_A, p - 1)
    toeplitz = skew[:, :, :nk]
    return jnp.swapaxes(jnp.where(jnp.asarray(visible)[None], toeplitz * LOG2E, NEG), 1, 2)


def _permute_w_in(w_in):
    sizes = [GROUP, GROUP, GROUP, 2 * GROUP, GROUP, GROUP, N_HEADS, N_HEADS,
             GROUP, GROUP, GROUP, GROUP, GROUP, GROUP]
    offs = np.concatenate([[0], np.cumsum(sizes)])
    seg = [w_in[:, offs[n]:offs[n + 1]] for n in range(len(sizes))]
    main = jnp.concatenate([seg[1], seg[3], seg[4], seg[5], seg[9], seg[12]], axis=1).astype(BF16)
    gates = jnp.concatenate([seg[6], seg[7]], axis=1)
    g_col = jnp.pad(gates, ((0, 0), (0, GATE_LANES - 2 * N_HEADS))).astype(BF16)
    g_row = jnp.pad(gates.T, ((0, GATE_ROWS - 2 * N_HEADS), (0, 0)))
    c_t = jnp.concatenate([seg[n].T for n in (8, 10, 11, 13, 0, 2)] + [g_row], axis=0).astype(BF16)
    return main, c_t, g_col


def _layer(x, nb, s, lam_init, p):
    x = _ffn(x, p["ffn1_norm"], p["ffn1_wg"].astype(BF16), p["ffn1_wu"].astype(BF16), p["ffn1_wd"].astype(BF16))

    w_main, w_ct, w_gc = _permute_w_in(p["w_in"])
    gate_bias = p["b_gate_bias"].reshape(2 * N_HEADS).astype(F32)
    gb_col = jnp.pad(gate_bias, (0, GATE_LANES - 2 * N_HEADS)).reshape(1, GATE_LANES)
    gb_row = jnp.pad(gate_bias, (0, GATE_ROWS - 2 * N_HEADS)).reshape(GATE_ROWS, 1)
    tile4 = lambda g: jnp.tile(g.astype(F32), GROUP // g.shape[0]).reshape(1, GROUP)
    (aq_t, ak, av_t, bqk, bv, bo, gcol, grow, cq_t, ck, cv_t, dq_t, dk, dv_t) = _proj(
        x, p["mix_norm"], w_main, w_ct, w_gc, gb_col, gb_row,
        tile4(p["a_q_norm"]).reshape(GROUP, 1), tile4(p["a_k_norm"]),
        tile4(p["c_q_norm"]).reshape(GROUP, 1), tile4(p["c_k_norm"]))

    ya = _mix_a(aq_t, ak, av_t, _rel_bias_tile(p["a_rel_bias"]), nb, s)
    yb = _mix_b(bqk, bv, bo, gcol, grow, p["b_conv_w"].astype(F32), p["b_conv_b"].astype(F32).reshape(1, -1),
                p["b_out_norm"].astype(F32).reshape(GROUP, 1), nb, s)
    lv = p["c_lambda"].astype(F32)
    lam = jnp.exp(jnp.sum(lv[0] * lv[1])) - jnp.exp(jnp.sum(lv[2] * lv[3])) + lam_init
    yc = _mix_c(lam.reshape(1), cq_t, ck, cv_t, tile4(p["c_out_norm"]).reshape(GROUP, 1), nb, s, lam_init)
    yd = _mix_d(dq_t, dk, dv_t, nb, s)

    return _mix_out_ffn(x, ya, yb, yc, yd, p["w_out"].astype(BF16), p["ffn2_norm"], p["ffn2_wg"].astype(BF16),
                        p["ffn2_wu"].astype(BF16), p["ffn2_wd"].astype(BF16))


_PARAM_NAMES = ("ffn1_norm", "ffn1_wg", "ffn1_wu", "ffn1_wd", "mix_norm", "w_in", "a_q_norm", "a_k_norm",
                "a_rel_bias", "b_conv_w", "b_conv_b", "b_gate_bias", "b_out_norm", "c_q_norm", "c_k_norm",
                "c_lambda", "c_out_norm", "w_out", "ffn2_norm", "ffn2_wg", "ffn2_wu", "ffn2_wd")


def kernel(x, ffn1_norm, ffn1_wg, ffn1_wu, ffn1_wd, mix_norm, w_in, a_q_norm, a_k_norm, a_rel_bias,
           b_conv_w, b_conv_b, b_gate_bias, b_out_norm, c_q_norm, c_k_norm, c_lambda, c_out_norm,
           w_out, ffn2_norm, ffn2_wg, ffn2_wu, ffn2_wd):
    params = dict(zip(_PARAM_NAMES, (ffn1_norm, ffn1_wg, ffn1_wu, ffn1_wd, mix_norm, w_in, a_q_norm, a_k_norm,
                                     a_rel_bias, b_conv_w, b_conv_b, b_gate_bias, b_out_norm, c_q_norm,
                                     c_k_norm, c_lambda, c_out_norm, w_out, ffn2_norm, ffn2_wg, ffn2_wu,
                                     ffn2_wd)))
    nb, s, d = x.shape
    depth = ffn1_norm.shape[0]
    h = x.reshape(nb * s, d)
    for l in range(depth):
        lam_init = 0.8 - 0.6 * math.exp(-0.3 * l)
        h = _layer(h, nb, s, lam_init, {k: v[l] for k, v in params.items()})
    return h.reshape(nb, s, d)
```

```python
import functools
import math

import jax
import jax.numpy as jnp
import numpy as np
from jax import lax
from jax.experimental import pallas as pl
from jax.experimental.pallas import tpu as pltpu

F32 = jnp.float32
BF16 = jnp.bfloat16

D_MODEL = 1024
D_FF = 2816
CHUNK = 64
HEAD_DIM = 64
N_HEADS = 4
GROUP = N_HEADS * HEAD_DIM
LEFT_CHUNKS = 8
REL_CLIP = 128
CONV_WIDTH = 4
DIFF_QK_DIM = HEAD_DIM // 2
EPS = 1e-6
NEG = -1e30
LOG2E = 1.4426950408889634

OFF_AK = 0
OFF_BQK, OFF_BV, OFF_BO = 256, 768, 1024
OFF_CK = 1280
OFF_DK = 1536
MAIN_COLS = 1792
N_T_GROUPS = 6
GATE_LANES = 128
GATE_ROWS = 16
ONES_ROWS = 16
VROWS = HEAD_DIM + ONES_ROWS

VMEM_LIMIT = 56 * 1024 * 1024

TM_FFN = 512
TM_PROJ = 512
TILE_T = 256
TQ_A = 256
A_TILES_PER_STEP = 2
LB = 256
TQ_C = 256
TK_C = 256
C_STEPS_PER_TRIP = 8
assert C_STEPS_PER_TRIP % 4 == 0
TQ_D = 256
TK_D = 256
SB_SKIP_LOG2 = -160.0


def _cparams(sem):
    return pltpu.CompilerParams(dimension_semantics=sem, vmem_limit_bytes=VMEM_LIMIT)


def _const_spec(shape):
    nd = len(shape)
    return pl.BlockSpec(shape, lambda *_: (0,) * nd, pipeline_mode=pl.Buffered(1))


def _split_dot(x, mat, terms, x_is_lhs=True):
    acc = None
    rem = x
    for term in range(terms):
        part = rem.astype(BF16)
        if term + 1 < terms:
            rem = rem - part.astype(F32)
        d = (jnp.dot(part, mat, preferred_element_type=F32) if x_is_lhs
             else jnp.dot(mat, part, preferred_element_type=F32))
        acc = d if acc is None else acc + d
    return acc


def _group_ones(width):
    r = lax.broadcasted_iota(jnp.int32, (GROUP, GROUP), 0) // width
    c = lax.broadcasted_iota(jnp.int32, (GROUP, GROUP), 1) // width
    return jnp.where(r == c, 1.0, 0.0).astype(BF16)


def _group_rms(x, gain, width):
    ss = _split_dot(x * x, _group_ones(width), 2)
    return x * lax.rsqrt(ss * (1.0 / width) + EPS) * gain


def _log_sigmoid(x):
    return jnp.minimum(x, 0.0) - jnp.log1p(jnp.exp(-jnp.abs(x)))


def _head_mask(shape, h, axis=1, width=HEAD_DIM):
    lane = lax.broadcasted_iota(jnp.int32, shape, axis)
    return (lane // width) == h


def _ffn_half_step(x, g_ref, wg_ref, wu_ref, wd_ref):
    xn = x * lax.rsqrt(jnp.mean(x * x, axis=-1, keepdims=True) + EPS) * g_ref[...]
    xb = xn.astype(BF16)
    g = jnp.dot(xb, wg_ref[...], preferred_element_type=F32)
    u = jnp.dot(xb, wu_ref[...], preferred_element_type=F32)
    h = (g * jax.nn.sigmoid(g) * u).astype(BF16)
    return x + 0.5 * jnp.dot(h, wd_ref[...], preferred_element_type=F32)


def _ffn_kernel(x_ref, g_ref, wg_ref, wu_ref, wd_ref, o_ref):
    o_ref[...] = _ffn_half_step(x_ref[...], g_ref, wg_ref, wu_ref, wd_ref)


def _mix_out_ffn_kernel(x_ref, ya_ref, yb_ref, yc_ref, yd_ref, wo_ref, g_ref, wg_ref, wu_ref, wd_ref, o_ref):
    x = x_ref[...]
    for grp, y_ref in enumerate((ya_ref, yb_ref, yc_ref, yd_ref)):
        x = x + jnp.dot(y_ref[...], wo_ref[grp * GROUP:(grp + 1) * GROUP, :], preferred_element_type=F32)
    o_ref[...] = _ffn_half_step(x, g_ref, wg_ref, wu_ref, wd_ref)


def _ffn_specs():
    return [_const_spec((1, D_MODEL)), _const_spec((D_MODEL, D_FF)), _const_spec((D_MODEL, D_FF)),
            _const_spec((D_FF, D_MODEL))]


def _ffn(x, gain, wg, wu, wd):
    t = x.shape[0]
    tm = min(TM_FFN, t)
    row = pl.BlockSpec((tm, D_MODEL), lambda i: (i, 0))
    return pl.pallas_call(
        _ffn_kernel,
        grid=(t // tm,),
        in_specs=[row] + _ffn_specs(),
        out_specs=row,
        out_shape=jax.ShapeDtypeStruct((t, D_MODEL), F32),
        compiler_params=_cparams(("parallel",)),
        name="ffn_half_step",
    )(x, gain.reshape(1, D_MODEL), wg, wu, wd)


def _mix_out_ffn(x, ya, yb, yc, yd, w_out, gain, wg, wu, wd):
    t = x.shape[0]
    tm = min(TM_FFN, t)
    row = lambda w: pl.BlockSpec((tm, w), lambda i: (i, 0))
    return pl.pallas_call(
        _mix_out_ffn_kernel,
        grid=(t // tm,),
        in_specs=[row(D_MODEL), row(GROUP), row(GROUP), row(GROUP), row(GROUP),
                  _const_spec((D_MODEL, D_MODEL))] + _ffn_specs(),
        out_specs=row(D_MODEL),
        out_shape=jax.ShapeDtypeStruct((t, D_MODEL), F32),
        compiler_params=_cparams(("parallel",)),
        name="mix_out_proj_ffn",
    )(x, ya, yb, yc, yd, w_out, gain.reshape(1, D_MODEL), wg, wu, wd)


def _proj_kernel(x_ref, g_ref, w_ref, wct_ref, wgc_ref, gbc_ref, gbr_ref,
                 aqn_ref, akn_ref, cqn_ref, ckn_ref,
                 aq_ref, ak_ref, av_ref, bqk_ref, bv_ref, bo_ref, gc_ref, gr_ref,
                 cq_ref, ck_ref, cv_ref, dq_ref, dk_ref, dv_ref):
    x = x_ref[...]
    hn = x * lax.rsqrt(jnp.mean(x * x, axis=-1, keepdims=True) + EPS) * g_ref[...]
    hb = hn.astype(BF16)

    def cols(off, width=GROUP):
        return jnp.dot(hb, w_ref[:, off:off + width], preferred_element_type=F32)

    nt_dims = (((1,), (1,)), ((), ()))

    all_t = lax.dot_general(wct_ref[...], hb, nt_dims, preferred_element_type=F32)

    def rows_t(idx):
        return all_t[idx * GROUP:(idx + 1) * GROUP, :]

    def group_rms_t(x_t, gain_col, width):
        ss = _split_dot(x_t * x_t, _group_ones(width), 2, x_is_lhs=False)
        return x_t * lax.rsqrt(ss * (1.0 / width) + EPS) * gain_col

    aqt = (group_rms_t(rows_t(4), aqn_ref[...], HEAD_DIM) * (HEAD_DIM ** -0.5 * LOG2E)).astype(BF16)
    ak_ref[...] = _group_rms(cols(OFF_AK), akn_ref[...], HEAD_DIM).astype(BF16)
    avt = rows_t(5).astype(BF16)
    bqk_ref[...] = cols(OFF_BQK, 2 * GROUP)
    bv_ref[...] = cols(OFF_BV).astype(BF16)
    bo_ref[...] = jax.nn.sigmoid(cols(OFF_BO))
    gcol = jnp.dot(hb, wgc_ref[...], preferred_element_type=F32) + gbc_ref[...]
    lane = lax.broadcasted_iota(jnp.int32, gcol.shape, 1)
    gc_ref[...] = jnp.where(lane >= N_HEADS, _log_sigmoid(gcol), gcol)
    grow = all_t[N_T_GROUPS * GROUP:, :] + gbr_ref[...]
    row = lax.broadcasted_iota(jnp.int32, grow.shape, 0)
    gr_ref[...] = jnp.where(row >= N_HEADS, _log_sigmoid(grow), grow)
    ck_ref[...] = _group_rms(cols(OFF_CK), ckn_ref[...], DIFF_QK_DIM).astype(BF16)
    cqt = (group_rms_t(rows_t(0), cqn_ref[...], DIFF_QK_DIM) * (DIFF_QK_DIM ** -0.5 * LOG2E)).astype(BF16)
    cvt = rows_t(1).astype(BF16)
    dk_ref[...] = cols(OFF_DK).astype(BF16)
    dqt = (rows_t(2) * (HEAD_DIM ** -0.5 * LOG2E)).astype(BF16)
    dvt = rows_t(3).astype(BF16)
    for sub in range(cq_ref.shape[0]):
        tok = slice(sub * TILE_T, (sub + 1) * TILE_T)
        aq_ref[sub] = aqt[:, tok]
        av_ref[sub] = avt[:, tok]
        cq_ref[sub] = cqt[:, tok]
        dq_ref[sub] = dqt[:, tok]
        dv_ref[sub] = dvt[:, tok]
        for h in range(N_HEADS):
            cv_ref[sub, h * VROWS:h * VROWS + HEAD_DIM, :] = cvt[h * HEAD_DIM:(h + 1) * HEAD_DIM, tok]
            cv_ref[sub, h * VROWS + HEAD_DIM:(h + 1) * VROWS, :] = jnp.ones((ONES_ROWS, TILE_T), BF16)


def _proj(x, gain, w_main, w_ct, w_gc, gb_col, gb_row, aqn, akn, cqn_col, ckn):
    t = x.shape[0]
    tm = TM_PROJ
    row_spec = lambda w: pl.BlockSpec((tm, w), lambda i: (i, 0))
    bf = lambda w: jax.ShapeDtypeStruct((t, w), BF16)
    f32 = lambda w: jax.ShapeDtypeStruct((t, w), F32)
    n_sub = tm // TILE_T
    tr_shape = jax.ShapeDtypeStruct((t // TILE_T, GROUP, TILE_T), BF16)
    tr_spec = pl.BlockSpec((n_sub, GROUP, TILE_T), lambda i: (i, 0, 0))
    vt_shape = jax.ShapeDtypeStruct((t // TILE_T, N_HEADS * VROWS, TILE_T), BF16)
    vt_spec = pl.BlockSpec((n_sub, N_HEADS * VROWS, TILE_T), lambda i: (i, 0, 0))
    out_shape = [tr_shape, bf(GROUP), tr_shape,
                 f32(2 * GROUP), bf(GROUP), f32(GROUP),
                 f32(GATE_LANES), jax.ShapeDtypeStruct((GATE_ROWS, t), F32),
                 tr_shape, bf(GROUP), vt_shape,
                 tr_shape, bf(GROUP), tr_shape]
    out_specs = ([tr_spec, row_spec(GROUP), tr_spec, row_spec(2 * GROUP), row_spec(GROUP), row_spec(GROUP),
                  row_spec(GATE_LANES), pl.BlockSpec((GATE_ROWS, tm), lambda i: (0, i))]
                 + [tr_spec, row_spec(GROUP), vt_spec, tr_spec, row_spec(GROUP), tr_spec])
    return pl.pallas_call(
        _proj_kernel,
        grid=(t // tm,),
        in_specs=[row_spec(D_MODEL), _const_spec((1, D_MODEL)),
                  _const_spec((D_MODEL, MAIN_COLS)), _const_spec((N_T_GROUPS * GROUP + GATE_ROWS, D_MODEL)),
                  _const_spec((D_MODEL, GATE_LANES)), _const_spec((1, GATE_LANES)),
                  _const_spec((GATE_ROWS, 1)),
                  _const_spec((GROUP, 1)), _const_spec((1, GROUP)),
                  _const_spec((GROUP, 1)), _const_spec((1, GROUP))],
        out_specs=out_specs,
        out_shape=out_shape,
        compiler_params=_cparams(("parallel",)),
        name="mix_in_proj",
    )(x, gain.reshape(1, D_MODEL), w_main, w_ct, w_gc, gb_col, gb_row, aqn, akn, cqn_col, ckn)


def _mix_a_kernel(qt_ref, k0_ref, k1_ref, k2_ref, k3_ref, v0_ref, v1_ref, v2_ref, v3_ref, bias_ref, o_ref):
    tp = pl.program_id(1)
    k_all = jnp.concatenate([k0_ref[...], k1_ref[...], k2_ref[...], k3_ref[...]], axis=0)
    v_all = jnp.concatenate([v0_ref[0], v1_ref[0], v2_ref[0], v3_ref[0]], axis=1)
    nk = TQ_A + LEFT_CHUNKS * CHUNK

    def attend(mask_start):
        scores = []
        for sub in range(A_TILES_PER_STEP):
            k = k_all[sub * TQ_A:sub * TQ_A + nk]
            q_t = qt_ref[sub]
            scores.append([jnp.dot(k[:, h * HEAD_DIM:(h + 1) * HEAD_DIM], q_t[h * HEAD_DIM:(h + 1) * HEAD_DIM, :],
                                   preferred_element_type=F32) for h in range(N_HEADS)])
        probs = []
        for sub in range(A_TILES_PER_STEP):
            if mask_start:
                key_pos = ((A_TILES_PER_STEP * tp + sub) * TQ_A - LEFT_CHUNKS * CHUNK
                           + lax.broadcasted_iota(jnp.int32, (nk, TQ_A), 0))
                valid = key_pos >= 0
            row = []
            for h in range(N_HEADS):
                s = scores[sub][h] + bias_ref[h]
                if mask_start:
                    s = jnp.where(valid, s, NEG)
                e = jnp.exp2(s - jnp.max(s, axis=0, keepdims=True))
                row.append((e * (1.0 / jnp.sum(e, axis=0, keepdims=True))).astype(BF16))
            probs.append(row)
        for sub in range(A_TILES_PER_STEP):
            v_t = v_all[:, sub * TQ_A:sub * TQ_A + nk]
            heads = [jnp.dot(v_t[h * HEAD_DIM:(h + 1) * HEAD_DIM, :], probs[sub][h], preferred_element_type=F32)
                     for h in range(N_HEADS)]
            o_ref[sub * TQ_A:(sub + 1) * TQ_A, :] = jnp.concatenate(heads, axis=0).T.astype(BF16)

    @pl.when(tp == 0)
    def _():
        attend(True)

    @pl.when(tp > 0)
    def _():
        attend(False)


def _mix_a(q_t, k, v_t, bias_tile_t, nb, s):
    nt = s // TQ_A
    npairs = nt // A_TILES_PER_STEP
    key_idx = lambda b, tp, off: b * nt + jnp.maximum(A_TILES_PER_STEP * tp + off, 0)
    offsets = (-2, -1, 0, 1)
    rows = lambda off: pl.BlockSpec((TQ_A, GROUP), lambda b, tp: (key_idx(b, tp, off), 0))
    tr = lambda off: pl.BlockSpec((1, GROUP, TQ_A), lambda b, tp: (key_idx(b, tp, off), 0, 0))
    return pl.pallas_call(
        _mix_a_kernel,
        grid=(nb, npairs),
        in_specs=[pl.BlockSpec((A_TILES_PER_STEP, GROUP, TQ_A), lambda b, tp: (b * npairs + tp, 0, 0))]
                 + [rows(off) for off in offsets] + [tr(off) for off in offsets]
                 + [_const_spec(bias_tile_t.shape)],
        out_specs=pl.BlockSpec((A_TILES_PER_STEP * TQ_A, GROUP), lambda b, tp: (b * npairs + tp, 0)),
        out_shape=jax.ShapeDtypeStruct((nb * s, GROUP), BF16),
        compiler_params=_cparams(("parallel", "parallel")),
        name="mix_a_chunk_attn",
    )(q_t, *([k] * len(offsets)), *([v_t] * len(offsets)), bias_tile_t)


def _mix_b_kernel(qk_ref, v_ref, og_ref, gc_ref, gr_ref, cw_ref, cb_ref, on_ref, o_ref,
                  xs_ref, ct_ref, n_ref, m_ref):
    c_idx = pl.program_id(1)

    @pl.when(c_idx == 0)
    def _():
        xs_ref[0:8, :] = jnp.zeros((8, 2 * GROUP), F32)
        ct_ref[...] = jnp.zeros_like(ct_ref)
        n_ref[...] = jnp.zeros_like(n_ref)
        m_ref[...] = jnp.zeros_like(m_ref)

    xs_ref[8:8 + LB, :] = qk_ref[...]
    acc = jnp.broadcast_to(cb_ref[...], (LB, 2 * GROUP))
    for j in range(CONV_WIDTH):
        start = 8 - (CONV_WIDTH - 1) + j
        acc = acc + xs_ref[start:start + LB, :] * cw_ref[j:j + 1, :]
    xs_ref[0:8, :] = xs_ref[LB:LB + 8, :]
    qk = acc * jax.nn.sigmoid(acc)
    q_t = qk[:, :GROUP].T.astype(BF16)
    kb = (qk[:, GROUP:] * (HEAD_DIM ** -0.5)).astype(BF16)
    v_t = v_ref[...].astype(F32).T.astype(BF16)

    r = lax.broadcasted_iota(jnp.int32, (LB, LB), 0)
    c = lax.broadcasted_iota(jnp.int32, (LB, LB), 1)
    tri = jnp.where(c <= r, 1.0, 0.0).astype(BF16)
    tri_t = jnp.where(r <= c, 1.0, 0.0).astype(BF16)
    visible = r <= c
    gcol = gc_ref[...]
    grow = gr_ref[...]
    bcum_col = _split_dot(gcol, tri, 3, x_is_lhs=False)
    bcum_row = _split_dot(grow, tri_t, 3)

    ct_state = ct_ref[...]
    n_state = n_ref[...]
    inter_num = jnp.dot(ct_state.astype(BF16), q_t, preferred_element_type=F32)
    n16 = jnp.concatenate([n_state, jnp.zeros_like(n_state)], axis=0).astype(BF16)
    inter_den = jnp.dot(n16, q_t, preferred_element_type=F32)

    row8 = lax.broadcasted_iota(jnp.int32, (8, GROUP), 0)
    col8 = lax.broadcasted_iota(jnp.int32, (8, GROUP), 1) // HEAD_DIM
    lane1 = lax.broadcasted_iota(jnp.int32, (1, GROUP), 1) // HEAD_DIM
    a_full = jnp.zeros((1, GROUP), F32)
    a8 = jnp.zeros((8, GROUP), F32)
    heads, vw_rows, wg_rows = [], [], []
    for h in range(N_HEADS):
        ch = slice(h * HEAD_DIM, (h + 1) * HEAD_DIM)
        i_row = grow[h:h + 1, :]
        b_row = bcum_row[N_HEADS + h:N_HEADS + h + 1, :]
        u_col = bcum_col[:, N_HEADS + h:N_HEADS + h + 1] - gcol[:, h:h + 1]
        m_prev = m_ref[h:h + 1, 0:1]
        dmat = jnp.where(visible, b_row - u_col, NEG)
        inter = b_row + m_prev
        m_t = jnp.maximum(inter, jnp.max(dmat, axis=0, keepdims=True))
        w_intra = jnp.exp(dmat - m_t)
        s_inter = jnp.exp(inter - m_t)
        sc = jnp.dot(kb[:, ch], q_t[ch, :], preferred_element_type=F32) * w_intra
        num = s_inter * inter_num[ch, :] + jnp.dot(v_t[ch, :], sc.astype(BF16), preferred_element_type=F32)
        den = s_inter * inter_den[h:h + 1, :] + jnp.sum(sc, axis=0, keepdims=True)
        h_t = num * (1.0 / jnp.maximum(jnp.abs(den), jnp.exp(-m_t)))
        heads.append(h_t * lax.rsqrt(jnp.mean(h_t * h_t, axis=0, keepdims=True) + EPS))
        b_tot = b_row[:, LB - 1:LB]
        g = b_tot - b_row + i_row
        m_new = jnp.maximum(b_tot + m_prev, jnp.max(g, axis=1, keepdims=True))
        a_h = jnp.exp(b_tot + m_prev - m_new)
        wg = jnp.exp(g - m_new)
        a_full = jnp.where(lane1 == h, a_h, a_full)
        a8 = jnp.where(row8 == h, a_h, a8)
        vw_rows.append((v_t[ch, :].astype(F32) * wg).astype(BF16))
        wg_rows.append(wg)
        m_ref[h:h + 1, :] = jnp.broadcast_to(m_new, (1, m_ref.shape[1]))

    out_t = jnp.concatenate(heads, axis=0) * on_ref[...]
    o_ref[...] = (out_t.T * og_ref[...]).astype(BF16)

    vw = jnp.concatenate(vw_rows, axis=0)
    kv_t = jnp.dot(vw, kb, preferred_element_type=F32)
    r2 = lax.broadcasted_iota(jnp.int32, (GROUP, GROUP), 0) // HEAD_DIM
    c2 = lax.broadcasted_iota(jnp.int32, (GROUP, GROUP), 1) // HEAD_DIM
    ct_ref[...] = a_full * ct_state + jnp.where(r2 == c2, kv_t, 0.0)
    wg16 = jnp.concatenate(wg_rows + [jnp.zeros((GATE_ROWS - N_HEADS, LB), F32)], axis=0).astype(BF16)
    n_add = jnp.dot(wg16, kb, preferred_element_type=F32)[0:8, :]
    n_ref[...] = a8 * n_state + jnp.where(row8 == col8, n_add, 0.0)


def _mix_b(bqk, bv, bo, gcol, grow, conv_w, conv_b, out_norm_col, nb, s):
    nc = s // LB
    row = lambda w: pl.BlockSpec((LB, w), lambda b, c: (b * nc + c, 0))
    return pl.pallas_call(
        _mix_b_kernel,
        grid=(nb, nc),
        in_specs=[row(2 * GROUP), row(GROUP), row(GROUP), row(GATE_LANES),
                  pl.BlockSpec((GATE_ROWS, LB), lambda b, c: (0, b * nc + c)),
                  _const_spec((CONV_WIDTH, 2 * GROUP)), _const_spec((1, 2 * GROUP)),
                  _const_spec((GROUP, 1))],
        out_specs=row(GROUP),
        out_shape=jax.ShapeDtypeStruct((nb * s, GROUP), BF16),
        scratch_shapes=[pltpu.VMEM((LB + 8, 2 * GROUP), F32),
                        pltpu.VMEM((GROUP, GROUP), F32),
                        pltpu.VMEM((8, GROUP), F32),
                        pltpu.VMEM((8, 128), F32)],
        compiler_params=_cparams(("parallel", "arbitrary")),
        name="mix_b_mlstm",
    )(bqk, bv, bo, gcol, grow, conv_w, conv_b, out_norm_col)


def _mix_c_kernel(lam_ref, qt_ref, k_ref, vt_ref, on_ref, o_ref, p_ref, acc_ref, m_ref, al_ref, *, lam_init):
    qt = pl.program_id(1)
    q_t = qt_ref[0]
    n_maps = 2 * N_HEADS
    acc_ref[...] = jnp.zeros(acc_ref.shape, F32)

    def key_tile(kt):
        return k_ref[pl.ds(pl.multiple_of(kt * TK_C, TK_C), TK_C), :]

    def score_map(k, j):
        ch = slice(j * DIFF_QK_DIM, (j + 1) * DIFF_QK_DIM)
        return jnp.dot(k[:, ch], q_t[ch, :], preferred_element_type=F32)

    def softmax(sc, slot, masked=False):
        if masked:
            key_chunk = lax.broadcasted_iota(jnp.int32, (TK_C, TQ_C), 0) // CHUNK
            q_chunk = lax.broadcasted_iota(jnp.int32, (TK_C, TQ_C), 1) // CHUNK
            vis = key_chunk <= q_chunk
        for j in range(n_maps):
            s = jnp.where(vis, sc[j], NEG) if masked else sc[j]
            m_old = m_ref[j]
            m_new = jnp.maximum(m_old, jnp.max(s, axis=0, keepdims=True))
            al_ref[j] = jnp.exp2(m_old - m_new)
            m_ref[j] = m_new
            p_ref[slot, j] = jnp.exp2(s - m_new).astype(BF16)

    def pv_update(kt, slot):
        v_ext = vt_ref[kt]
        for j in range(n_maps):
            h = j // 2
            pv = jnp.dot(v_ext[h * VROWS:(h + 1) * VROWS, :], p_ref[slot, j], preferred_element_type=F32)
            acc_ref[j] = al_ref[j] * acc_ref[j] + pv

    def step(kt, prev, slot):
        k = key_tile(kt)
        v_ext = vt_ref[prev]
        sc = []
        for j in range(n_maps):
            h = j // 2
            sc.append(score_map(k, j))
            pv = jnp.dot(v_ext[h * VROWS:(h + 1) * VROWS, :], p_ref[slot, j], preferred_element_type=F32)
            acc_ref[j] = al_ref[j] * acc_ref[j] + pv
        softmax(sc, 1 - slot)

    m_ref[...] = jnp.full(m_ref.shape, NEG, F32)
    k_diag = key_tile(qt)
    softmax([score_map(k_diag, j) for j in range(n_maps)], 0, masked=True)
    n_trips = qt // C_STEPS_PER_TRIP

    def body(it, carry):
        a = C_STEPS_PER_TRIP * it
        step(a, jnp.where(it == 0, qt, a - 1), 0)
        for n in range(1, C_STEPS_PER_TRIP):
            step(a + n, a + n - 1, n % 2)
        return carry

    lax.fori_loop(0, n_trips, body, 0)
    done = C_STEPS_PER_TRIP * n_trips
    rem = qt - done
    last = jnp.where(n_trips > 0, done - 1, qt)

    half = C_STEPS_PER_TRIP // 2
    has_half = rem >= half

    @pl.when(has_half)
    def _():
        step(done, last, 0)
        for n in range(1, half):
            step(done + n, done + n - 1, n % 2)

    done = jnp.where(has_half, done + half, done)
    last = jnp.where(has_half, done - 1, last)
    rem = jnp.where(has_half, rem - half, rem)

    for n in range(half - 1):
        @pl.when(rem > n)
        def _(n=n):
            step(done + n, last if n == 0 else done + n - 1, n % 2)

    final = jnp.where(rem == 0, last, done + rem - 1)
    for slot in range(2):
        @pl.when(rem % 2 == slot)
        def _(slot=slot):
            pv_update(final, slot)

    lam = lam_ref[0]
    heads = []
    for h in range(N_HEADS):
        a0 = acc_ref[2 * h]
        a1 = acc_ref[2 * h + 1]
        o_h = (a0[:HEAD_DIM] * (1.0 / a0[HEAD_DIM:HEAD_DIM + 1])
               - lam * (a1[:HEAD_DIM] * (1.0 / a1[HEAD_DIM:HEAD_DIM + 1])))
        ms = jnp.mean(o_h * o_h, axis=0, keepdims=True)
        heads.append(o_h * lax.rsqrt(ms + EPS))
    out_t = jnp.concatenate(heads, axis=0) * (on_ref[...] * (1.0 - lam_init))
    o_ref[...] = out_t.T.astype(BF16)


def _mix_c(lam, q_t, k, v_t, out_norm_col, nb, s, lam_init):
    nt = s // TQ_C
    tile_t = pl.BlockSpec((1, GROUP, TQ_C), lambda b, t, lam_ref: (b * nt + t, 0, 0))
    full_k = pl.BlockSpec((s, GROUP), lambda b, t, lam_ref: (b, 0), pipeline_mode=pl.Buffered(1))
    full_vt = pl.BlockSpec((nt, N_HEADS * VROWS, TK_C), lambda b, t, lam_ref: (b, 0, 0),
                           pipeline_mode=pl.Buffered(1))
    grid_spec = pltpu.PrefetchScalarGridSpec(
        num_scalar_prefetch=1,
        grid=(nb, nt),
        in_specs=[tile_t, full_k, full_vt,
                  pl.BlockSpec((GROUP, 1), lambda b, t, lam_ref: (0, 0), pipeline_mode=pl.Buffered(1))],
        out_specs=pl.BlockSpec((TQ_C, GROUP), lambda b, t, lam_ref: (b * nt + t, 0)),
        scratch_shapes=[pltpu.VMEM((2, 2 * N_HEADS, TK_C, TQ_C), BF16),
                        pltpu.VMEM((2 * N_HEADS, VROWS, TQ_C), F32),
                        pltpu.VMEM((2 * N_HEADS, 1, TQ_C), F32),
                        pltpu.VMEM((2 * N_HEADS, 1, TQ_C), F32)],
    )
    return pl.pallas_call(
        functools.partial(_mix_c_kernel, lam_init=lam_init),
        grid_spec=grid_spec,
        out_shape=jax.ShapeDtypeStruct((nb * s, GROUP), BF16),
        compiler_params=_cparams(("parallel", "parallel")),
        name="mix_c_diff_attn",
    )(lam, q_t, k, v_t, out_norm_col)


def _mix_d_kernel(qt_ref, k_ref, vt_ref, o_ref, acc_ref, run_ref):
    qt = pl.program_id(1)
    q_t = qt_ref[0]
    acc_ref[...] = jnp.zeros(acc_ref.shape, F32)
    run_ref[...] = jnp.zeros(run_ref.shape, F32)
    r = lax.broadcasted_iota(jnp.int32, (TK_D, TK_D), 0)
    c = lax.broadcasted_iota(jnp.int32, (TK_D, TK_D), 1)
    later = jnp.where(c > r, 1.0, 0.0).astype(BF16)

    before = (lax.broadcasted_iota(jnp.int32, (TK_D, TQ_D), 0)
              < lax.broadcasted_iota(jnp.int32, (TK_D, TQ_D), 1))

    def add_tiles(specs):
        loaded = []
        for kt, diagonal in specs:
            k = k_ref[pl.ds(pl.multiple_of(kt * TK_D, TK_D), TK_D), :]
            zs = [jnp.dot(k[:, h * HEAD_DIM:(h + 1) * HEAD_DIM], q_t[h * HEAD_DIM:(h + 1) * HEAD_DIM, :],
                          preferred_element_type=F32) for h in range(N_HEADS)]
            loaded.append((vt_ref[kt], zs, diagonal))
        pending = []
        for v_t, zs, diagonal in loaded:
            for h in range(N_HEADS):
                z = zs[h]
                log_keep = jnp.minimum(-z, 0.0) - jnp.log2(1.0 + jnp.exp2(-jnp.abs(z)))
                log_take = z + log_keep
                if diagonal:
                    log_keep = jnp.where(before, log_keep, 0.0)
                run = run_ref[h]
                pending.append((h, v_t, diagonal, log_take, run + _split_dot(log_keep, later, 2, x_is_lhs=False)))
                run_ref[h] = run + jnp.sum(log_keep, axis=0, keepdims=True)
        for h, v_t, diagonal, log_take, between in pending:
            a = jnp.exp2(log_take + between)
            if diagonal:
                a = jnp.where(before, a, 0.0)
            acc_ref[h] += jnp.dot(v_t[h * HEAD_DIM:(h + 1) * HEAD_DIM, :], a.astype(BF16),
                                  preferred_element_type=F32)

    def worst_run():
        return functools.reduce(jnp.maximum, [jnp.max(run_ref[h]) for h in range(N_HEADS)])

    @pl.when(qt == 0)
    def _():
        add_tiles([(qt, True)])

    @pl.when(qt > 0)
    def _():
        add_tiles([(qt, True), (qt - 1, False)])

    def cond(state):
        kt, worst = state
        return jnp.logical_and(kt >= 0, worst > SB_SKIP_LOG2)

    def body(state):
        kt, _ = state
        add_tiles([(kt, False)])
        return kt - 1, worst_run()

    lax.while_loop(cond, body, (qt - 2, worst_run()))
    out_t = jnp.concatenate([acc_ref[h] for h in range(N_HEADS)], axis=0)
    o_ref[...] = out_t.T.astype(BF16)


def _mix_d(q_t, k, v_t, nb, s):
    nt = s // TQ_D
    tile_t = pl.BlockSpec((1, GROUP, TQ_D), lambda b, t: (b * nt + t, 0, 0))
    full_k = pl.BlockSpec((s, GROUP), lambda b, t: (b, 0), pipeline_mode=pl.Buffered(1))
    full_vt = pl.BlockSpec((nt, GROUP, TK_D), lambda b, t: (b, 0, 0), pipeline_mode=pl.Buffered(1))
    return pl.pallas_call(
        _mix_d_kernel,
        grid=(nb, nt),
        in_specs=[tile_t, full_k, full_vt],
        out_specs=pl.BlockSpec((TQ_D, GROUP), lambda b, t: (b * nt + t, 0)),
        out_shape=jax.ShapeDtypeStruct((nb * s, GROUP), BF16),
        scratch_shapes=[pltpu.VMEM((N_HEADS, HEAD_DIM, TQ_D), F32),
                        pltpu.VMEM((N_HEADS, 1, TQ_D), F32)],
        compiler_params=_cparams(("parallel", "parallel")),
        name="mix_d_stick_breaking",
    )(q_t, k, v_t)


def _rel_bias_tile(rel_bias):
    nk = TQ_A + LEFT_CHUNKS * CHUNK
    i = np.arange(TQ_A)[:, None]
    j = np.arange(nk)[None, :]
    dc = j // CHUNK - i // CHUNK
    visible = (dc >= 0) & (dc <= LEFT_CHUNKS)
    p = TQ_A + nk
    e = np.arange(p)
    d = np.where(e < nk, e, e - p)
    rel = LEFT_CHUNKS * CHUNK - d
    idx = np.clip(rel, -REL_CLIP, REL_CLIP) + REL_CLIP
    diag = rel_bias[:, idx].astype(F32)
    nh = rel_bias.shape[0]
    skew = jnp.tile(diag, (1, TQ_A))[:, :TQ_A * (p - 1)].reshape(nh, TQ_A, p - 1)
    toeplitz = skew[:, :, :nk]
    return jnp.swapaxes(jnp.where(jnp.asarray(visible)[None], toeplitz * LOG2E, NEG), 1, 2)


def _permute_w_in(w_in):
    sizes = [GROUP, GROUP, GROUP, 2 * GROUP, GROUP, GROUP, N_HEADS, N_HEADS,
             GROUP, GROUP, GROUP, GROUP, GROUP, GROUP]
    offs = np.concatenate([[0], np.cumsum(sizes)])
    seg = [w_in[:, offs[n]:offs[n + 1]] for n in range(len(sizes))]
    main = jnp.concatenate([seg[1], seg[3], seg[4], seg[5], seg[9], seg[12]], axis=1).astype(BF16)
    gates = jnp.concatenate([seg[6], seg[7]], axis=1)
    g_col = jnp.pad(gates, ((0, 0), (0, GATE_LANES - 2 * N_HEADS))).astype(BF16)
    g_row = jnp.pad(gates.T, ((0, GATE_ROWS - 2 * N_HEADS), (0, 0)))
    c_t = jnp.concatenate([seg[n].T for n in (8, 10, 11, 13, 0, 2)] + [g_row], axis=0).astype(BF16)
    return main, c_t, g_col


def _layer(x, nb, s, lam_init, p):
    x = _ffn(x, p["ffn1_norm"], p["ffn1_wg"].astype(BF16), p["ffn1_wu"].astype(BF16), p["ffn1_wd"].astype(BF16))

    w_main, w_ct, w_gc = _permute_w_in(p["w_in"])
    gate_bias = p["b_gate_bias"].reshape(2 * N_HEADS).astype(F32)
    gb_col = jnp.pad(gate_bias, (0, GATE_LANES - 2 * N_HEADS)).reshape(1, GATE_LANES)
    gb_row = jnp.pad(gate_bias, (0, GATE_ROWS - 2 * N_HEADS)).reshape(GATE_ROWS, 1)
    tile4 = lambda g: jnp.tile(g.astype(F32), GROUP // g.shape[0]).reshape(1, GROUP)
    (aq_t, ak, av_t, bqk, bv, bo, gcol, grow, cq_t, ck, cv_t, dq_t, dk, dv_t) = _proj(
        x, p["mix_norm"], w_main, w_ct, w_gc, gb_col, gb_row,
        tile4(p["a_q_norm"]).reshape(GROUP, 1), tile4(p["a_k_norm"]),
        tile4(p["c_q_norm"]).reshape(GROUP, 1), tile4(p["c_k_norm"]))

    ya = _mix_a(aq_t, ak, av_t, _rel_bias_tile(p["a_rel_bias"]), nb, s)
    yb = _mix_b(bqk, bv, bo, gcol, grow, p["b_conv_w"].astype(F32), p["b_conv_b"].astype(F32).reshape(1, -1),
                p["b_out_norm"].astype(F32).reshape(GROUP, 1), nb, s)
    lv = p["c_lambda"].astype(F32)
    lam = jnp.exp(jnp.sum(lv[0] * lv[1])) - jnp.exp(jnp.sum(lv[2] * lv[3])) + lam_init
    yc = _mix_c(lam.reshape(1), cq_t, ck, cv_t, tile4(p["c_out_norm"]).reshape(GROUP, 1), nb, s, lam_init)
    yd = _mix_d(dq_t, dk, dv_t, nb, s)

    return _mix_out_ffn(x, ya, yb, yc, yd, p["w_out"].astype(BF16), p["ffn2_norm"], p["ffn2_wg"].astype(BF16),
                        p["ffn2_wu"].astype(BF16), p["ffn2_wd"].astype(BF16))


_PARAM_NAMES = ("ffn1_norm", "ffn1_wg", "ffn1_wu", "ffn1_wd", "mix_norm", "w_in", "a_q_norm", "a_k_norm",
                "a_rel_bias", "b_conv_w", "b_conv_b", "b_gate_bias", "b_out_norm", "c_q_norm", "c_k_norm",
                "c_lambda", "c_out_norm", "w_out", "ffn2_norm", "ffn2_wg", "ffn2_wu", "ffn2_wd")


def kernel(x, ffn1_norm, ffn1_wg, ffn1_wu, ffn1_wd, mix_norm, w_in, a_q_norm, a_k_norm, a_rel_bias,
           b_conv_w, b_conv_b, b_gate_bias, b_out_norm, c_q_norm, c_k_norm, c_lambda, c_out_norm,
           w_out, ffn2_norm, ffn2_wg, ffn2_wu, ffn2_wd):
    params = dict(zip(_PARAM_NAMES, (ffn1_norm, ffn1_wg, ffn1_wu, ffn1_wd, mix_norm, w_in, a_q_norm, a_k_norm,
                                     a_rel_bias, b_conv_w, b_conv_b, b_gate_bias, b_out_norm, c_q_norm,
                                     c_k_norm, c_lambda, c_out_norm, w_out, ffn2_norm, ffn2_wg, ffn2_wu,
                                     ffn2_wd)))
    nb, s, d = x.shape
    depth = ffn1_norm.shape[0]
    h = x.reshape(nb * s, d)
    for l in range(depth):
        lam_init = 0.8 - 0.6 * math.exp(-0.3 * l)
        h = _layer(h, nb, s, lam_init, {k: v[l] for k, v in params.items()})
    return h.reshape(nb, s, d)
```

```python
import functools
import math

import jax
import jax.numpy as jnp
import numpy as np
from jax import lax
from jax.experimental import pallas as pl
from jax.experimental.pallas import tpu as pltpu

F32 = jnp.float32
BF16 = jnp.bfloat16

D_MODEL = 1024
D_FF = 2816
CHUNK = 64
HEAD_DIM = 64
N_HEADS = 4
GROUP = N_HEADS * HEAD_DIM
LEFT_CHUNKS = 8
REL_CLIP = 128
CONV_WIDTH = 4
DIFF_QK_DIM = HEAD_DIM // 2
EPS = 1e-6
NEG = -1e30
LOG2E = 1.4426950408889634

OFF_AK = 0
OFF_BQK, OFF_BV, OFF_BO = 256, 768, 1024
OFF_CK = 1280
OFF_DK = 1536
MAIN_COLS = 1792
N_T_GROUPS = 6
GATE_LANES = 128
GATE_ROWS = 16
ONES_ROWS = 16
VROWS = HEAD_DIM + ONES_ROWS

VMEM_LIMIT = 56 * 1024 * 1024

TM_FFN = 512
FF_CHUNK = 256
TM_PROJ = 512
TILE_T = 256
TQ_A = 256
A_TILES_PER_STEP = 2
LB = 256
TQ_C = 256
TK_C = 256
C_STEPS_PER_TRIP = 8
assert C_STEPS_PER_TRIP % 4 == 0
TQ_D = 256
TK_D = 256
SB_SKIP_LOG2 = -160.0


def _cparams(sem):
    return pltpu.CompilerParams(dimension_semantics=sem, vmem_limit_bytes=VMEM_LIMIT)


def _const_spec(shape):
    nd = len(shape)
    return pl.BlockSpec(shape, lambda *_: (0,) * nd, pipeline_mode=pl.Buffered(1))


def _split_dot(x, mat, terms, x_is_lhs=True):
    acc = None
    rem = x
    for term in range(terms):
        part = rem.astype(BF16)
        if term + 1 < terms:
            rem = rem - part.astype(F32)
        d = (jnp.dot(part, mat, preferred_element_type=F32) if x_is_lhs
             else jnp.dot(mat, part, preferred_element_type=F32))
        acc = d if acc is None else acc + d
    return acc


def _group_ones(width):
    r = lax.broadcasted_iota(jnp.int32, (GROUP, GROUP), 0) // width
    c = lax.broadcasted_iota(jnp.int32, (GROUP, GROUP), 1) // width
    return jnp.where(r == c, 1.0, 0.0).astype(BF16)


def _group_rms(x, gain, width):
    ss = _split_dot(x * x, _group_ones(width), 2)
    return x * lax.rsqrt(ss * (1.0 / width) + EPS) * gain


def _log_sigmoid(x):
    return jnp.minimum(x, 0.0) - jnp.log1p(jnp.exp(-jnp.abs(x)))


def _head_mask(shape, h, axis=1, width=HEAD_DIM):
    lane = lax.broadcasted_iota(jnp.int32, shape, axis)
    return (lane // width) == h


def _ffn_half_step(x, g_ref, wg_ref, wu_ref, wd_ref):
    xn = x * lax.rsqrt(jnp.mean(x * x, axis=-1, keepdims=True) + EPS) * g_ref[...]
    xb = xn.astype(BF16)
    y = None
    for f0 in range(0, D_FF, FF_CHUNK):
        g = jnp.dot(xb, wg_ref[:, f0:f0 + FF_CHUNK], preferred_element_type=F32)
        u = jnp.dot(xb, wu_ref[:, f0:f0 + FF_CHUNK], preferred_element_type=F32)
        h = (g * jax.nn.sigmoid(g) * u).astype(BF16)
        part = jnp.dot(h, wd_ref[f0:f0 + FF_CHUNK, :], preferred_element_type=F32)
        y = part if y is None else y + part
    return x + 0.5 * y


def _ffn_kernel(x_ref, g_ref, wg_ref, wu_ref, wd_ref, o_ref):
    o_ref[...] = _ffn_half_step(x_ref[...], g_ref, wg_ref, wu_ref, wd_ref)


def _mix_out_ffn_kernel(x_ref, ya_ref, yb_ref, yc_ref, yd_ref, wo_ref, g_ref, wg_ref, wu_ref, wd_ref, o_ref):
    x = x_ref[...]
    for grp, y_ref in enumerate((ya_ref, yb_ref, yc_ref, yd_ref)):
        x = x + jnp.dot(y_ref[...], wo_ref[grp * GROUP:(grp + 1) * GROUP, :], preferred_element_type=F32)
    o_ref[...] = _ffn_half_step(x, g_ref, wg_ref, wu_ref, wd_ref)


def _ffn_specs():
    return [_const_spec((1, D_MODEL)), _const_spec((D_MODEL, D_FF)), _const_spec((D_MODEL, D_FF)),
            _const_spec((D_FF, D_MODEL))]


def _ffn(x, gain, wg, wu, wd):
    t = x.shape[0]
    tm = min(TM_FFN, t)
    row = pl.BlockSpec((tm, D_MODEL), lambda i: (i, 0))
    return pl.pallas_call(
        _ffn_kernel,
        grid=(t // tm,),
        in_specs=[row] + _ffn_specs(),
        out_specs=row,
        out_shape=jax.ShapeDtypeStruct((t, D_MODEL), F32),
        compiler_params=_cparams(("parallel",)),
        name="ffn_half_step",
    )(x, gain.reshape(1, D_MODEL), wg, wu, wd)


def _mix_out_ffn(x, ya, yb, yc, yd, w_out, gain, wg, wu, wd):
    t = x.shape[0]
    tm = min(TM_FFN, t)
    row = lambda w: pl.BlockSpec((tm, w), lambda i: (i, 0))
    return pl.pallas_call(
        _mix_out_ffn_kernel,
        grid=(t // tm,),
        in_specs=[row(D_MODEL), row(GROUP), row(GROUP), row(GROUP), row(GROUP),
                  _const_spec((D_MODEL, D_MODEL))] + _ffn_specs(),
        out_specs=row(D_MODEL),
        out_shape=jax.ShapeDtypeStruct((t, D_MODEL), F32),
        compiler_params=_cparams(("parallel",)),
        name="mix_out_proj_ffn",
    )(x, ya, yb, yc, yd, w_out, gain.reshape(1, D_MODEL), wg, wu, wd)


def _proj_kernel(x_ref, g_ref, w_ref, wct_ref, wgc_ref, gbc_ref, gbr_ref,
                 aqn_ref, akn_ref, cqn_ref, ckn_ref,
                 aq_ref, ak_ref, av_ref, bqk_ref, bv_ref, bo_ref, gc_ref, gr_ref,
                 cq_ref, ck_ref, cv_ref, dq_ref, dk_ref, dv_ref):
    x = x_ref[...]
    hn = x * lax.rsqrt(jnp.mean(x * x, axis=-1, keepdims=True) + EPS) * g_ref[...]
    hb = hn.astype(BF16)

    def cols(off, width=GROUP):
        return jnp.dot(hb, w_ref[:, off:off + width], preferred_element_type=F32)

    nt_dims = (((1,), (1,)), ((), ()))

    all_t = lax.dot_general(wct_ref[...], hb, nt_dims, preferred_element_type=F32)

    def rows_t(idx):
        return all_t[idx * GROUP:(idx + 1) * GROUP, :]

    def group_rms_t(x_t, gain_col, width):
        ss = _split_dot(x_t * x_t, _group_ones(width), 2, x_is_lhs=False)
        return x_t * lax.rsqrt(ss * (1.0 / width) + EPS) * gain_col

    aqt = (group_rms_t(rows_t(4), aqn_ref[...], HEAD_DIM) * (HEAD_DIM ** -0.5 * LOG2E)).astype(BF16)
    ak_ref[...] = _group_rms(cols(OFF_AK), akn_ref[...], HEAD_DIM).astype(BF16)
    avt = rows_t(5).astype(BF16)
    bqk_ref[...] = cols(OFF_BQK, 2 * GROUP)
    bv_ref[...] = cols(OFF_BV).astype(BF16)
    bo_ref[...] = jax.nn.sigmoid(cols(OFF_BO))
    gcol = jnp.dot(hb, wgc_ref[...], preferred_element_type=F32) + gbc_ref[...]
    lane = lax.broadcasted_iota(jnp.int32, gcol.shape, 1)
    gc_ref[...] = jnp.where(lane >= N_HEADS, _log_sigmoid(gcol), gcol)
    grow = all_t[N_T_GROUPS * GROUP:, :] + gbr_ref[...]
    row = lax.broadcasted_iota(jnp.int32, grow.shape, 0)
    gr_ref[...] = jnp.where(row >= N_HEADS, _log_sigmoid(grow), grow)
    ck_ref[...] = _group_rms(cols(OFF_CK), ckn_ref[...], DIFF_QK_DIM).astype(BF16)
    cqt = (group_rms_t(rows_t(0), cqn_ref[...], DIFF_QK_DIM) * (DIFF_QK_DIM ** -0.5 * LOG2E)).astype(BF16)
    cvt = rows_t(1).astype(BF16)
    dk_ref[...] = cols(OFF_DK).astype(BF16)
    dqt = (rows_t(2) * (HEAD_DIM ** -0.5 * LOG2E)).astype(BF16)
    dvt = rows_t(3).astype(BF16)
    for sub in range(cq_ref.shape[0]):
        tok = slice(sub * TILE_T, (sub + 1) * TILE_T)
        aq_ref[sub] = aqt[:, tok]
        av_ref[sub] = avt[:, tok]
        cq_ref[sub] = cqt[:, tok]
        dq_ref[sub] = dqt[:, tok]
        dv_ref[sub] = dvt[:, tok]
        for h in range(N_HEADS):
            cv_ref[sub, h * VROWS:h * VROWS + HEAD_DIM, :] = cvt[h * HEAD_DIM:(h + 1) * HEAD_DIM, tok]
            cv_ref[sub, h * VROWS + HEAD_DIM:(h + 1) * VROWS, :] = jnp.ones((ONES_ROWS, TILE_T), BF16)


def _proj(x, gain, w_main, w_ct, w_gc, gb_col, gb_row, aqn, akn, cqn_col, ckn):
    t = x.shape[0]
    tm = TM_PROJ
    row_spec = lambda w: pl.BlockSpec((tm, w), lambda i: (i, 0))
    bf = lambda w: jax.ShapeDtypeStruct((t, w), BF16)
    f32 = lambda w: jax.ShapeDtypeStruct((t, w), F32)
    n_sub = tm // TILE_T
    tr_shape = jax.ShapeDtypeStruct((t // TILE_T, GROUP, TILE_T), BF16)
    tr_spec = pl.BlockSpec((n_sub, GROUP, TILE_T), lambda i: (i, 0, 0))
    vt_shape = jax.ShapeDtypeStruct((t // TILE_T, N_HEADS * VROWS, TILE_T), BF16)
    vt_spec = pl.BlockSpec((n_sub, N_HEADS * VROWS, TILE_T), lambda i: (i, 0, 0))
    out_shape = [tr_shape, bf(GROUP), tr_shape,
                 f32(2 * GROUP), bf(GROUP), f32(GROUP),
                 f32(GATE_LANES), jax.ShapeDtypeStruct((GATE_ROWS, t), F32),
                 tr_shape, bf(GROUP), vt_shape,
                 tr_shape, bf(GROUP), tr_shape]
    out_specs = ([tr_spec, row_spec(GROUP), tr_spec, row_spec(2 * GROUP), row_spec(GROUP), row_spec(GROUP),
                  row_spec(GATE_LANES), pl.BlockSpec((GATE_ROWS, tm), lambda i: (0, i))]
                 + [tr_spec, row_spec(GROUP), vt_spec, tr_spec, row_spec(GROUP), tr_spec])
    return pl.pallas_call(
        _proj_kernel,
        grid=(t // tm,),
        in_specs=[row_spec(D_MODEL), _const_spec((1, D_MODEL)),
                  _const_spec((D_MODEL, MAIN_COLS)), _const_spec((N_T_GROUPS * GROUP + GATE_ROWS, D_MODEL)),
                  _const_spec((D_MODEL, GATE_LANES)), _const_spec((1, GATE_LANES)),
                  _const_spec((GATE_ROWS, 1)),
                  _const_spec((GROUP, 1)), _const_spec((1, GROUP)),
                  _const_spec((GROUP, 1)), _const_spec((1, GROUP))],
        out_specs=out_specs,
        out_shape=out_shape,
        compiler_params=_cparams(("parallel",)),
        name="mix_in_proj",
    )(x, gain.reshape(1, D_MODEL), w_main, w_ct, w_gc, gb_col, gb_row, aqn, akn, cqn_col, ckn)


def _mix_a_kernel(qt_ref, k0_ref, k1_ref, k2_ref, k3_ref, v0_ref, v1_ref, v2_ref, v3_ref, bias_ref, o_ref):
    tp = pl.program_id(1)
    k_all = jnp.concatenate([k0_ref[...], k1_ref[...], k2_ref[...], k3_ref[...]], axis=0)
    v_all = jnp.concatenate([v0_ref[0], v1_ref[0], v2_ref[0], v3_ref[0]], axis=1)
    nk = TQ_A + LEFT_CHUNKS * CHUNK

    def attend(mask_start):
        scores = []
        for sub in range(A_TILES_PER_STEP):
            k = k_all[sub * TQ_A:sub * TQ_A + nk]
            q_t = qt_ref[sub]
            scores.append([jnp.dot(k[:, h * HEAD_DIM:(h + 1) * HEAD_DIM], q_t[h * HEAD_DIM:(h + 1) * HEAD_DIM, :],
                                   preferred_element_type=F32) for h in range(N_HEADS)])
        probs = []
        for sub in range(A_TILES_PER_STEP):
            if mask_start:
                key_pos = ((A_TILES_PER_STEP * tp + sub) * TQ_A - LEFT_CHUNKS * CHUNK
                           + lax.broadcasted_iota(jnp.int32, (nk, TQ_A), 0))
                valid = key_pos >= 0
            row = []
            for h in range(N_HEADS):
                s = scores[sub][h] + bias_ref[h]
                if mask_start:
                    s = jnp.where(valid, s, NEG)
                e = jnp.exp2(s - jnp.max(s, axis=0, keepdims=True))
                row.append((e * (1.0 / jnp.sum(e, axis=0, keepdims=True))).astype(BF16))
            probs.append(row)
        for sub in range(A_TILES_PER_STEP):
            v_t = v_all[:, sub * TQ_A:sub * TQ_A + nk]
            heads = [jnp.dot(v_t[h * HEAD_DIM:(h + 1) * HEAD_DIM, :], probs[sub][h], preferred_element_type=F32)
                     for h in range(N_HEADS)]
            o_ref[sub * TQ_A:(sub + 1) * TQ_A, :] = jnp.concatenate(heads, axis=0).T.astype(BF16)

    @pl.when(tp == 0)
    def _():
        attend(True)

    @pl.when(tp > 0)
    def _():
        attend(False)


def _mix_a(q_t, k, v_t, bias_tile_t, nb, s):
    nt = s // TQ_A
    npairs = nt // A_TILES_PER_STEP
    key_idx = lambda b, tp, off: b * nt + jnp.maximum(A_TILES_PER_STEP * tp + off, 0)
    offsets = (-2, -1, 0, 1)
    rows = lambda off: pl.BlockSpec((TQ_A, GROUP), lambda b, tp: (key_idx(b, tp, off), 0))
    tr = lambda off: pl.BlockSpec((1, GROUP, TQ_A), lambda b, tp: (key_idx(b, tp, off), 0, 0))
    return pl.pallas_call(
        _mix_a_kernel,
        grid=(nb, npairs),
        in_specs=[pl.BlockSpec((A_TILES_PER_STEP, GROUP, TQ_A), lambda b, tp: (b * npairs + tp, 0, 0))]
                 + [rows(off) for off in offsets] + [tr(off) for off in offsets]
                 + [_const_spec(bias_tile_t.shape)],
        out_specs=pl.BlockSpec((A_TILES_PER_STEP * TQ_A, GROUP), lambda b, tp: (b * npairs + tp, 0)),
        out_shape=jax.ShapeDtypeStruct((nb * s, GROUP), BF16),
        compiler_params=_cparams(("parallel", "parallel")),
        name="mix_a_chunk_attn",
    )(q_t, *([k] * len(offsets)), *([v_t] * len(offsets)), bias_tile_t)


def _mix_b_kernel(qk_ref, v_ref, og_ref, gc_ref, gr_ref, cw_ref, cb_ref, on_ref, o_ref,
                  xs_ref, ct_ref, n_ref, m_ref):
    c_idx = pl.program_id(1)

    @pl.when(c_idx == 0)
    def _():
        xs_ref[0:8, :] = jnp.zeros((8, 2 * GROUP), F32)
        ct_ref[...] = jnp.zeros_like(ct_ref)
        n_ref[...] = jnp.zeros_like(n_ref)
        m_ref[...] = jnp.zeros_like(m_ref)

    xs_ref[8:8 + LB, :] = qk_ref[...]
    acc = jnp.broadcast_to(cb_ref[...], (LB, 2 * GROUP))
    for j in range(CONV_WIDTH):
        start = 8 - (CONV_WIDTH - 1) + j
        acc = acc + xs_ref[start:start + LB, :] * cw_ref[j:j + 1, :]
    xs_ref[0:8, :] = xs_ref[LB:LB + 8, :]
    qk = acc * jax.nn.sigmoid(acc)
    q_t = qk[:, :GROUP].T.astype(BF16)
    kb = (qk[:, GROUP:] * (HEAD_DIM ** -0.5)).astype(BF16)
    v_t = v_ref[...].astype(F32).T.astype(BF16)

    r = lax.broadcasted_iota(jnp.int32, (LB, LB), 0)
    c = lax.broadcasted_iota(jnp.int32, (LB, LB), 1)
    tri = jnp.where(c <= r, 1.0, 0.0).astype(BF16)
    tri_t = jnp.where(r <= c, 1.0, 0.0).astype(BF16)
    visible = r <= c
    gcol = gc_ref[...]
    grow = gr_ref[...]
    bcum_col = _split_dot(gcol, tri, 3, x_is_lhs=False)
    bcum_row = _split_dot(grow, tri_t, 3)

    ct_state = ct_ref[...]
    n_state = n_ref[...]
    inter_num = jnp.dot(ct_state.astype(BF16), q_t, preferred_element_type=F32)
    n16 = jnp.concatenate([n_state, jnp.zeros_like(n_state)], axis=0).astype(BF16)
    inter_den = jnp.dot(n16, q_t, preferred_element_type=F32)

    row8 = lax.broadcasted_iota(jnp.int32, (8, GROUP), 0)
    col8 = lax.broadcasted_iota(jnp.int32, (8, GROUP), 1) // HEAD_DIM
    lane1 = lax.broadcasted_iota(jnp.int32, (1, GROUP), 1) // HEAD_DIM
    a_full = jnp.zeros((1, GROUP), F32)
    a8 = jnp.zeros((8, GROUP), F32)
    heads, vw_rows, wg_rows = [], [], []
    for h in range(N_HEADS):
        ch = slice(h * HEAD_DIM, (h + 1) * HEAD_DIM)
        i_row = grow[h:h + 1, :]
        b_row = bcum_row[N_HEADS + h:N_HEADS + h + 1, :]
        u_col = bcum_col[:, N_HEADS + h:N_HEADS + h + 1] - gcol[:, h:h + 1]
        m_prev = m_ref[h:h + 1, 0:1]
        dmat = jnp.where(visible, b_row - u_col, NEG)
        inter = b_row + m_prev
        m_t = jnp.maximum(inter, jnp.max(dmat, axis=0, keepdims=True))
        w_intra = jnp.exp(dmat - m_t)
        s_inter = jnp.exp(inter - m_t)
        sc = jnp.dot(kb[:, ch], q_t[ch, :], preferred_element_type=F32) * w_intra
        num = s_inter * inter_num[ch, :] + jnp.dot(v_t[ch, :], sc.astype(BF16), preferred_element_type=F32)
        den = s_inter * inter_den[h:h + 1, :] + jnp.sum(sc, axis=0, keepdims=True)
        h_t = num * (1.0 / jnp.maximum(jnp.abs(den), jnp.exp(-m_t)))
        heads.append(h_t * lax.rsqrt(jnp.mean(h_t * h_t, axis=0, keepdims=True) + EPS))
        b_tot = b_row[:, LB - 1:LB]
        g = b_tot - b_row + i_row
        m_new = jnp.maximum(b_tot + m_prev, jnp.max(g, axis=1, keepdims=True))
        a_h = jnp.exp(b_tot + m_prev - m_new)
        wg = jnp.exp(g - m_new)
        a_full = jnp.where(lane1 == h, a_h, a_full)
        a8 = jnp.where(row8 == h, a_h, a8)
        vw_rows.append((v_t[ch, :].astype(F32) * wg).astype(BF16))
        wg_rows.append(wg)
        m_ref[h:h + 1, :] = jnp.broadcast_to(m_new, (1, m_ref.shape[1]))

    out_t = jnp.concatenate(heads, axis=0) * on_ref[...]
    o_ref[...] = (out_t.T * og_ref[...]).astype(BF16)

    vw = jnp.concatenate(vw_rows, axis=0)
    kv_t = jnp.dot(vw, kb, preferred_element_type=F32)
    r2 = lax.broadcasted_iota(jnp.int32, (GROUP, GROUP), 0) // HEAD_DIM
    c2 = lax.broadcasted_iota(jnp.int32, (GROUP, GROUP), 1) // HEAD_DIM
    ct_ref[...] = a_full * ct_state + jnp.where(r2 == c2, kv_t, 0.0)
    wg16 = jnp.concatenate(wg_rows + [jnp.zeros((GATE_ROWS - N_HEADS, LB), F32)], axis=0).astype(BF16)
    n_add = jnp.dot(wg16, kb, preferred_element_type=F32)[0:8, :]
    n_ref[...] = a8 * n_state + jnp.where(row8 == col8, n_add, 0.0)


def _mix_b(bqk, bv, bo, gcol, grow, conv_w, conv_b, out_norm_col, nb, s):
    nc = s // LB
    row = lambda w: pl.BlockSpec((LB, w), lambda b, c: (b * nc + c, 0))
    return pl.pallas_call(
        _mix_b_kernel,
        grid=(nb, nc),
        in_specs=[row(2 * GROUP), row(GROUP), row(GROUP), row(GATE_LANES),
                  pl.BlockSpec((GATE_ROWS, LB), lambda b, c: (0, b * nc + c)),
                  _const_spec((CONV_WIDTH, 2 * GROUP)), _const_spec((1, 2 * GROUP)),
                  _const_spec((GROUP, 1))],
        out_specs=row(GROUP),
        out_shape=jax.ShapeDtypeStruct((nb * s, GROUP), BF16),
        scratch_shapes=[pltpu.VMEM((LB + 8, 2 * GROUP), F32),
                        pltpu.VMEM((GROUP, GROUP), F32),
                        pltpu.VMEM((8, GROUP), F32),
                        pltpu.VMEM((8, 128), F32)],
        compiler_params=_cparams(("parallel", "arbitrary")),
        name="mix_b_mlstm",
    )(bqk, bv, bo, gcol, grow, conv_w, conv_b, out_norm_col)


def _mix_c_kernel(lam_ref, qt_ref, k_ref, vt_ref, on_ref, o_ref, p_ref, acc_ref, m_ref, al_ref, *, lam_init):
    qt = pl.program_id(1)
    q_t = qt_ref[0]
    n_maps = 2 * N_HEADS
    acc_ref[...] = jnp.zeros(acc_ref.shape, F32)

    def key_tile(kt):
        return k_ref[pl.ds(pl.multiple_of(kt * TK_C, TK_C), TK_C), :]

    def score_map(k, j):
        ch = slice(j * DIFF_QK_DIM, (j + 1) * DIFF_QK_DIM)
        return jnp.dot(k[:, ch], q_t[ch, :], preferred_element_type=F32)

    def softmax(sc, slot, masked=False):
        if masked:
            key_chunk = lax.broadcasted_iota(jnp.int32, (TK_C, TQ_C), 0) // CHUNK
            q_chunk = lax.broadcasted_iota(jnp.int32, (TK_C, TQ_C), 1) // CHUNK
            vis = key_chunk <= q_chunk
        for j in range(n_maps):
            s = jnp.where(vis, sc[j], NEG) if masked else sc[j]
            m_old = m_ref[j]
            m_new = jnp.maximum(m_old, jnp.max(s, axis=0, keepdims=True))
            al_ref[j] = jnp.exp2(m_old - m_new)
            m_ref[j] = m_new
            p_ref[slot, j] = jnp.exp2(s - m_new).astype(BF16)

    def pv_update(kt, slot):
        v_ext = vt_ref[kt]
        for j in range(n_maps):
            h = j // 2
            pv = jnp.dot(v_ext[h * VROWS:(h + 1) * VROWS, :], p_ref[slot, j], preferred_element_type=F32)
            acc_ref[j] = al_ref[j] * acc_ref[j] + pv

    def step(kt, prev, slot):
        k = key_tile(kt)
        v_ext = vt_ref[prev]
        sc = []
        for j in range(n_maps):
            h = j // 2
            sc.append(score_map(k, j))
            pv = jnp.dot(v_ext[h * VROWS:(h + 1) * VROWS, :], p_ref[slot, j], preferred_element_type=F32)
            acc_ref[j] = al_ref[j] * acc_ref[j] + pv
        softmax(sc, 1 - slot)

    m_ref[...] = jnp.full(m_ref.shape, NEG, F32)
    k_diag = key_tile(qt)
    softmax([score_map(k_diag, j) for j in range(n_maps)], 0, masked=True)
    n_trips = qt // C_STEPS_PER_TRIP

    def body(it, carry):
        a = C_STEPS_PER_TRIP * it
        step(a, jnp.where(it == 0, qt, a - 1), 0)
        for n in range(1, C_STEPS_PER_TRIP):
            step(a + n, a + n - 1, n % 2)
        return carry

    lax.fori_loop(0, n_trips, body, 0)
    done = C_STEPS_PER_TRIP * n_trips
    rem = qt - done
    last = jnp.where(n_trips > 0, done - 1, qt)

    half = C_STEPS_PER_TRIP // 2
    has_half = rem >= half

    @pl.when(has_half)
    def _():
        step(done, last, 0)
        for n in range(1, half):
            step(done + n, done + n - 1, n % 2)

    done = jnp.where(has_half, done + half, done)
    last = jnp.where(has_half, done - 1, last)
    rem = jnp.where(has_half, rem - half, rem)

    for n in range(half - 1):
        @pl.when(rem > n)
        def _(n=n):
            step(done + n, last if n == 0 else done + n - 1, n % 2)

    final = jnp.where(rem == 0, last, done + rem - 1)
    for slot in range(2):
        @pl.when(rem % 2 == slot)
        def _(slot=slot):
            pv_update(final, slot)

    lam = lam_ref[0]
    heads = []
    for h in range(N_HEADS):
        a0 = acc_ref[2 * h]
        a1 = acc_ref[2 * h + 1]
        o_h = (a0[:HEAD_DIM] * (1.0 / a0[HEAD_DIM:HEAD_DIM + 1])
               - lam * (a1[:HEAD_DIM] * (1.0 / a1[HEAD_DIM:HEAD_DIM + 1])))
        ms = jnp.mean(o_h * o_h, axis=0, keepdims=True)
        heads.append(o_h * lax.rsqrt(ms + EPS))
    out_t = jnp.concatenate(heads, axis=0) * (on_ref[...] * (1.0 - lam_init))
    o_ref[...] = out_t.T.astype(BF16)


def _mix_c(lam, q_t, k, v_t, out_norm_col, nb, s, lam_init):
    nt = s // TQ_C
    tile_t = pl.BlockSpec((1, GROUP, TQ_C), lambda b, t, lam_ref: (b * nt + t, 0, 0))
    full_k = pl.BlockSpec((s, GROUP), lambda b, t, lam_ref: (b, 0), pipeline_mode=pl.Buffered(1))
    full_vt = pl.BlockSpec((nt, N_HEADS * VROWS, TK_C), lambda b, t, lam_ref: (b, 0, 0),
                           pipeline_mode=pl.Buffered(1))
    grid_spec = pltpu.PrefetchScalarGridSpec(
        num_scalar_prefetch=1,
        grid=(nb, nt),
        in_specs=[tile_t, full_k, full_vt,
                  pl.BlockSpec((GROUP, 1), lambda b, t, lam_ref: (0, 0), pipeline_mode=pl.Buffered(1))],
        out_specs=pl.BlockSpec((TQ_C, GROUP), lambda b, t, lam_ref: (b * nt + t, 0)),
        scratch_shapes=[pltpu.VMEM((2, 2 * N_HEADS, TK_C, TQ_C), BF16),
                        pltpu.VMEM((2 * N_HEADS, VROWS, TQ_C), F32),
                        pltpu.VMEM((2 * N_HEADS, 1, TQ_C), F32),
                        pltpu.VMEM((2 * N_HEADS, 1, TQ_C), F32)],
    )
    return pl.pallas_call(
        functools.partial(_mix_c_kernel, lam_init=lam_init),
        grid_spec=grid_spec,
        out_shape=jax.ShapeDtypeStruct((nb * s, GROUP), BF16),
        compiler_params=_cparams(("parallel", "parallel")),
        name="mix_c_diff_attn",
    )(lam, q_t, k, v_t, out_norm_col)


def _mix_d_kernel(qt_ref, k_ref, vt_ref, o_ref, acc_ref, run_ref):
    qt = pl.program_id(1)
    q_t = qt_ref[0]
    acc_ref[...] = jnp.zeros(acc_ref.shape, F32)
    run_ref[...] = jnp.zeros(run_ref.shape, F32)
    r = lax.broadcasted_iota(jnp.int32, (TK_D, TK_D), 0)
    c = lax.broadcasted_iota(jnp.int32, (TK_D, TK_D), 1)
    later = jnp.where(c > r, 1.0, 0.0).astype(BF16)

    before = (lax.broadcasted_iota(jnp.int32, (TK_D, TQ_D), 0)
              < lax.broadcasted_iota(jnp.int32, (TK_D, TQ_D), 1))

    def add_tiles(specs):
        loaded = []
        for kt, diagonal in specs:
            k = k_ref[pl.ds(pl.multiple_of(kt * TK_D, TK_D), TK_D), :]
            zs = [jnp.dot(k[:, h * HEAD_DIM:(h + 1) * HEAD_DIM], q_t[h * HEAD_DIM:(h + 1) * HEAD_DIM, :],
                          preferred_element_type=F32) for h in range(N_HEADS)]
            loaded.append((vt_ref[kt], zs, diagonal))
        pending = []
        for v_t, zs, diagonal in loaded:
            for h in range(N_HEADS):
                z = zs[h]
                log_keep = jnp.minimum(-z, 0.0) - jnp.log2(1.0 + jnp.exp2(-jnp.abs(z)))
                log_take = z + log_keep
                if diagonal:
                    log_keep = jnp.where(before, log_keep, 0.0)
                run = run_ref[h]
                pending.append((h, v_t, diagonal, log_take, run + _split_dot(log_keep, later, 2, x_is_lhs=False)))
                run_ref[h] = run + jnp.sum(log_keep, axis=0, keepdims=True)
        for h, v_t, diagonal, log_take, between in pending:
            a = jnp.exp2(log_take + between)
            if diagonal:
                a = jnp.where(before, a, 0.0)
            acc_ref[h] += jnp.dot(v_t[h * HEAD_DIM:(h + 1) * HEAD_DIM, :], a.astype(BF16),
                                  preferred_element_type=F32)

    def worst_run():
        return functools.reduce(jnp.maximum, [jnp.max(run_ref[h]) for h in range(N_HEADS)])

    @pl.when(qt == 0)
    def _():
        add_tiles([(qt, True)])

    @pl.when(qt > 0)
    def _():
        add_tiles([(qt, True), (qt - 1, False)])

    def cond(state):
        kt, worst = state
        return jnp.logical_and(kt >= 0, worst > SB_SKIP_LOG2)

    def body(state):
        kt, _ = state
        add_tiles([(kt, False)])
        return kt - 1, worst_run()

    lax.while_loop(cond, body, (qt - 2, worst_run()))
    out_t = jnp.concatenate([acc_ref[h] for h in range(N_HEADS)], axis=0)
    o_ref[...] = out_t.T.astype(BF16)


def _mix_d(q_t, k, v_t, nb, s):
    nt = s // TQ_D
    tile_t = pl.BlockSpec((1, GROUP, TQ_D), lambda b, t: (b * nt + t, 0, 0))
    full_k = pl.BlockSpec((s, GROUP), lambda b, t: (b, 0), pipeline_mode=pl.Buffered(1))
    full_vt = pl.BlockSpec((nt, GROUP, TK_D), lambda b, t: (b, 0, 0), pipeline_mode=pl.Buffered(1))
    return pl.pallas_call(
        _mix_d_kernel,
        grid=(nb, nt),
        in_specs=[tile_t, full_k, full_vt],
        out_specs=pl.BlockSpec((TQ_D, GROUP), lambda b, t: (b * nt + t, 0)),
        out_shape=jax.ShapeDtypeStruct((nb * s, GROUP), BF16),
        scratch_shapes=[pltpu.VMEM((N_HEADS, HEAD_DIM, TQ_D), F32),
                        pltpu.VMEM((N_HEADS, 1, TQ_D), F32)],
        compiler_params=_cparams(("parallel", "parallel")),
        name="mix_d_stick_breaking",
    )(q_t, k, v_t)


def _rel_bias_tile(rel_bias):
    nk = TQ_A + LEFT_CHUNKS * CHUNK
    i = np.arange(TQ_A)[:, None]
    j = np.arange(nk)[None, :]
    dc = j // CHUNK - i // CHUNK
    visible = (dc >= 0) & (dc <= LEFT_CHUNKS)
    p = TQ_A + nk
    e = np.arange(p)
    d = np.where(e < nk, e, e - p)
    rel = LEFT_CHUNKS * CHUNK - d
    idx = np.clip(rel, -REL_CLIP, REL_CLIP) + REL_CLIP
    diag = rel_bias[:, idx].astype(F32)
    nh = rel_bias.shape[0]
    skew = jnp.tile(diag, (1, TQ_A))[:, :TQ_A * (p - 1)].reshape(nh, TQ_A, p - 1)
    toeplitz = skew[:, :, :nk]
    return jnp.swapaxes(jnp.where(jnp.asarray(visible)[None], toeplitz * LOG2E, NEG), 1, 2)


def _permute_w_in(w_in):
    sizes = [GROUP, GROUP, GROUP, 2 * GROUP, GROUP, GROUP, N_HEADS, N_HEADS,
             GROUP, GROUP, GROUP, GROUP, GROUP, GROUP]
    offs = np.concatenate([[0], np.cumsum(sizes)])
    seg = [w_in[:, offs[n]:offs[n + 1]] for n in range(len(sizes))]
    main = jnp.concatenate([seg[1], seg[3], seg[4], seg[5], seg[9], seg[12]], axis=1).astype(BF16)
    gates = jnp.concatenate([seg[6], seg[7]], axis=1)
    g_col = jnp.pad(gates, ((0, 0), (0, GATE_LANES - 2 * N_HEADS))).astype(BF16)
    g_row = jnp.pad(gates.T, ((0, GATE_ROWS - 2 * N_HEADS), (0, 0)))
    c_t = jnp.concatenate([seg[n].T for n in (8, 10, 11, 13, 0, 2)] + [g_row], axis=0).astype(BF16)
    return main, c_t, g_col


def _layer(x, nb, s, lam_init, p):
    x = _ffn(x, p["ffn1_norm"], p["ffn1_wg"].astype(BF16), p["ffn1_wu"].astype(BF16), p["ffn1_wd"].astype(BF16))

    w_main, w_ct, w_gc = _permute_w_in(p["w_in"])
    gate_bias = p["b_gate_bias"].reshape(2 * N_HEADS).astype(F32)
    gb_col = jnp.pad(gate_bias, (0, GATE_LANES - 2 * N_HEADS)).reshape(1, GATE_LANES)
    gb_row = jnp.pad(gate_bias, (0, GATE_ROWS - 2 * N_HEADS)).reshape(GATE_ROWS, 1)
    tile4 = lambda g: jnp.tile(g.astype(F32), GROUP // g.shape[0]).reshape(1, GROUP)
    (aq_t, ak, av_t, bqk, bv, bo, gcol, grow, cq_t, ck, cv_t, dq_t, dk, dv_t) = _proj(
        x, p["mix_norm"], w_main, w_ct, w_gc, gb_col, gb_row,
        tile4(p["a_q_norm"]).reshape(GROUP, 1), tile4(p["a_k_norm"]),
        tile4(p["c_q_norm"]).reshape(GROUP, 1), tile4(p["c_k_norm"]))

    ya = _mix_a(aq_t, ak, av_t, _rel_bias_tile(p["a_rel_bias"]), nb, s)
    yb = _mix_b(bqk, bv, bo, gcol, grow, p["b_conv_w"].astype(F32), p["b_conv_b"].astype(F32).reshape(1, -1),
                p["b_out_norm"].astype(F32).reshape(GROUP, 1), nb, s)
    lv = p["c_lambda"].astype(F32)
    lam = jnp.exp(jnp.sum(lv[0] * lv[1])) - jnp.exp(jnp.sum(lv[2] * lv[3])) + lam_init
    yc = _mix_c(lam.reshape(1), cq_t, ck, cv_t, tile4(p["c_out_norm"]).reshape(GROUP, 1), nb, s, lam_init)
    yd = _mix_d(dq_t, dk, dv_t, nb, s)

    return _mix_out_ffn(x, ya, yb, yc, yd, p["w_out"].astype(BF16), p["ffn2_norm"], p["ffn2_wg"].astype(BF16),
                        p["ffn2_wu"].astype(BF16), p["ffn2_wd"].astype(BF16))


_PARAM_NAMES = ("ffn1_norm", "ffn1_wg", "ffn1_wu", "ffn1_wd", "mix_norm", "w_in", "a_q_norm", "a_k_norm",
                "a_rel_bias", "b_conv_w", "b_conv_b", "b_gate_bias", "b_out_norm", "c_q_norm", "c_k_norm",
                "c_lambda", "c_out_norm", "w_out", "ffn2_norm", "ffn2_wg", "ffn2_wu", "ffn2_wd")


def kernel(x, ffn1_norm, ffn1_wg, ffn1_wu, ffn1_wd, mix_norm, w_in, a_q_norm, a_k_norm, a_rel_bias,
           b_conv_w, b_conv_b, b_gate_bias, b_out_norm, c_q_norm, c_k_norm, c_lambda, c_out_norm,
           w_out, ffn2_norm, ffn2_wg, ffn2_wu, ffn2_wd):
    params = dict(zip(_PARAM_NAMES, (ffn1_norm, ffn1_wg, ffn1_wu, ffn1_wd, mix_norm, w_in, a_q_norm, a_k_norm,
                                     a_rel_bias, b_conv_w, b_conv_b, b_gate_bias, b_out_norm, c_q_norm,
                                     c_k_norm, c_lambda, c_out_norm, w_out, ffn2_norm, ffn2_wg, ffn2_wu,
                                     ffn2_wd)))
    nb, s, d = x.shape
    depth = ffn1_norm.shape[0]
    h = x.reshape(nb * s, d)
    for l in range(depth):
        lam_init = 0.8 - 0.6 * math.exp(-0.3 * l)
        h = _layer(h, nb, s, lam_init, {k: v[l] for k, v in params.items()})
    return h.reshape(nb, s, d)
```
